```python
import jax, jax.numpy as jnp
from jax import lax
import numpy as np

D_MODEL = 1024
BATCH = 8
SEQ = 8192
DEPTH = 1

RWKV_HEAD = 64
RWKV_HEADS = D_MODEL // RWKV_HEAD
RWKV_WIDTH = RWKV_HEADS * RWKV_HEAD
DECAY_LORA = 64
ICLR_LORA = 64
GATE_LORA = 128
LNX_EPS = 64e-5
ATTN_HEAD = 64
ATTN_Q_HEADS = 16
ATTN_KV_HEADS = 2
ATTN_GROUP = ATTN_Q_HEADS // ATTN_KV_HEADS
ATTN_WIDTH = ATTN_Q_HEADS * ATTN_HEAD
KV_WIDTH = ATTN_KV_HEADS * ATTN_HEAD
WINDOW = 128
BLOCK = 128
ATTN_SCALE = ATTN_HEAD ** -0.5
NEG_INF = -1e30
N_KEYS = 128
N_EXPERTS = N_KEYS * N_KEYS
PEER_HEADS = 8
PEER_QDIM = 256
PEER_HALF = PEER_QDIM // 2
PEER_TOPK = 16
PEER_CHUNK = 128
NORM_EPS = 1e-6
N_ADA = 6
N_SHIFT = 3 * RWKV_WIDTH + DECAY_LORA + ICLR_LORA + GATE_LORA
SHIFT_SPLITS = (RWKV_WIDTH, 2 * RWKV_WIDTH, 3 * RWKV_WIDTH,
                3 * RWKV_WIDTH + DECAY_LORA, 3 * RWKV_WIDTH + DECAY_LORA + ICLR_LORA)
REST_SPLITS = (ATTN_WIDTH, ATTN_WIDTH + KV_WIDTH, ATTN_WIDTH + 2 * KV_WIDTH,
               ATTN_WIDTH + 2 * KV_WIDTH + D_MODEL)
IN_WIDTH = N_SHIFT + ATTN_WIDTH + 2 * KV_WIDTH + 2 * D_MODEL

kernel_name = 'rwkv7_swa_sink_peer_hybrid'


def rms_norm(x, w):
    xf = x.astype(jnp.float32)
    y = xf * lax.rsqrt(jnp.mean(xf * xf, axis=-1, keepdims=True) + NORM_EPS)
    return (y * w.astype(jnp.float32)).astype(x.dtype)


def token_shift(p):
    return jnp.pad(p, ((0, 0), (1, 0), (0, 0)))[:, :-1, :]


def rwkv7_recurrence(r, decay, k, v, a, b):
    bsz, _, n_h, n = r.shape
    xs = tuple(jnp.moveaxis(t.astype(jnp.float32), 1, 0) for t in (r, decay, k, v, a, b))

    def step(state, inp):
        r_t, w_t, k_t, v_t, a_t, b_t = inp
        sa = jnp.einsum('bhvk,bhk->bhv', state, a_t)
        state = (state * w_t[:, :, None, :] + sa[..., None] * b_t[:, :, None, :]
                 + v_t[..., None] * k_t[:, :, None, :])
        return state, jnp.einsum('bhvk,bhk->bhv', state, r_t)

    state0 = jnp.zeros((bsz, n_h, n, n), jnp.float32)
    _, ys = lax.scan(step, state0, xs)
    return jnp.moveaxis(ys, 0, 1)


def rwkv7_mixer(pr, pk, pv, pw, pa, pg, w0, w_up, a0, a_up, g_up, k_k, k_a, r_k, lnx_w, lnx_b):
    bsz, seq, _ = pr.shape
    f32 = jnp.float32
    heads = lambda t: t.reshape(bsz, seq, RWKV_HEADS, RWKV_HEAD)
    w_log = -jax.nn.softplus(-(w0 + jnp.tanh(pw) @ w_up).astype(f32)) - 0.5
    decay = jnp.exp(-jnp.exp(w_log))
    a = jax.nn.sigmoid((a0 + pa @ a_up).astype(f32))
    g = jax.nn.sigmoid(pg) @ g_up
    kk = heads((pk * k_k).astype(f32))
    kk = kk / jnp.maximum(jnp.sqrt(jnp.sum(kk * kk, axis=-1, keepdims=True)), 1e-12)
    k = pk.astype(f32) * (1.0 + (a - 1.0) * k_a)
    r_h, k_h, v_h, a_h = heads(pr.astype(f32)), heads(k), heads(pv.astype(f32)), heads(a)
    y = rwkv7_recurrence(r_h, heads(decay), k_h, v_h, -kk, kk * a_h)
    mu = jnp.mean(y, axis=-1, keepdims=True)
    var = jnp.mean(jnp.square(y - mu), axis=-1, keepdims=True)
    y = ((y - mu) * lax.rsqrt(var + LNX_EPS)).reshape(bsz, seq, RWKV_WIDTH) * lnx_w + lnx_b
    bonus = jnp.sum(r_h * k_h * r_k, axis=-1, keepdims=True) * v_h
    y = (y + bonus.reshape(bsz, seq, RWKV_WIDTH)) * g
    return y.astype(pr.dtype)


def swa_sink_attention(q, k, v, q_norm_w, k_norm_w, sinks):
    bsz, seq, _ = q.shape
    nb = seq // BLOCK
    f32 = jnp.float32
    q = rms_norm(q.reshape(bsz, seq, ATTN_Q_HEADS, ATTN_HEAD), q_norm_w)
    k = rms_norm(k.reshape(bsz, seq, ATTN_KV_HEADS, ATTN_HEAD), k_norm_w)
    v = v.reshape(bsz, seq, ATTN_KV_HEADS, ATTN_HEAD)
    qb = q.reshape(bsz, nb, BLOCK, ATTN_KV_HEADS, ATTN_GROUP, ATTN_HEAD)

    def band(t):
        tb = t.reshape(bsz, nb, BLOCK, ATTN_KV_HEADS, ATTN_HEAD)
        prev = jnp.pad(tb, ((0, 0), (1, 0), (0, 0), (0, 0), (0, 0)))[:, :-1]
        return jnp.concatenate([prev, tb], axis=2)

    kb, vb = band(k), band(v)
    sink = sinks.astype(f32).reshape(ATTN_KV_HEADS, ATTN_GROUP)[None, :, :, None]
    qi = jnp.arange(BLOCK)[:, None]
    kj = jnp.arange(2 * BLOCK)[None, :]
    rel = kj - qi
    in_window = (rel >= BLOCK - WINDOW + 1) & (rel <= BLOCK)

    def one_block(args):
        q_blk, k_blk, v_blk, blk = args
        s = jnp.einsum('bqgnd,bkgd->bgnqk', q_blk, k_blk, preferred_element_type=f32) * ATTN_SCALE
        valid = in_window & (blk * BLOCK - BLOCK + kj >= 0)
        s = jnp.where(valid, s, NEG_INF)
        m = jnp.maximum(jnp.max(s, axis=-1), sink)
        p = jnp.exp(s - m[..., None])
        denom = jnp.sum(p, axis=-1) + jnp.exp(sink - m)
        p = p / denom[..., None]
        return jnp.einsum('bgnqk,bkgd->bqgnd', p.astype(v_blk.dtype), v_blk)

    out = lax.map(one_block, (jnp.moveaxis(qb, 1, 0), jnp.moveaxis(kb, 1, 0),
                              jnp.moveaxis(vb, 1, 0), jnp.arange(nb)))
    return jnp.moveaxis(out, 0, 1).reshape(bsz, seq, ATTN_WIDTH)


def peer_ffn(h, w_q, keys_1, keys_2, u_tab, v_tab):
    bsz, seq, d = h.shape
    chunks = h.reshape(bsz * seq // PEER_CHUNK, PEER_CHUNK, d)

    def one_chunk(xc):
        q = (xc @ w_q).reshape(PEER_CHUNK, PEER_HEADS, 2, PEER_HALF)
        s1 = jnp.einsum('chd,nd->chn', q[:, :, 0], keys_1)
        s2 = jnp.einsum('chd,nd->chn', q[:, :, 1], keys_2)
        v1, i1 = lax.top_k(s1, PEER_TOPK)
        v2, i2 = lax.top_k(s2, PEER_TOPK)
        cand = (v1[..., :, None] + v2[..., None, :]).reshape(PEER_CHUNK, PEER_HEADS, PEER_TOPK * PEER_TOPK)
        cidx = (i1[..., :, None] * N_KEYS + i2[..., None, :]).reshape(PEER_CHUNK, PEER_HEADS, PEER_TOPK * PEER_TOPK)
        sc, pos = lax.top_k(cand, PEER_TOPK)
        idx = jnp.take_along_axis(cidx, pos, axis=-1)
        gate = jax.nn.softmax(sc.astype(jnp.float32), axis=-1)
        u = jnp.take(u_tab, idx, axis=0)
        act = jax.nn.gelu(jnp.einsum('chkd,cd->chk', u, xc).astype(jnp.float32), approximate=False)
        wts = (gate * act).astype(xc.dtype)
        return jnp.einsum('chk,chkd->cd', wts, jnp.take(v_tab, idx, axis=0))

    return lax.map(one_chunk, chunks).reshape(bsz, seq, d)


def setup_inputs(seed: int = 0) -> dict:
    key = jax.random.key(seed)
    ks = jax.random.split(key, 27)
    f32 = jnp.float32
    nrm = lambda k, shape, s: jax.random.normal(k, shape, f32) * s
    L = DEPTH
    return {
        'x': nrm(ks[0], (BATCH, SEQ, D_MODEL), 1.0),
        'c': nrm(ks[1], (BATCH, D_MODEL), 1.0),
        'ada_w': nrm(ks[2], (L, D_MODEL, N_ADA * D_MODEL), 0.5 * D_MODEL ** -0.5),
        'ada_b': nrm(ks[3], (L, N_ADA * D_MODEL), 0.02),
        'norm1_w': 1.0 + nrm(ks[4], (L, D_MODEL), 0.05),
        'norm2_w': 1.0 + nrm(ks[5], (L, D_MODEL), 0.05),
        'w_in': nrm(ks[6], (L, D_MODEL, IN_WIDTH), D_MODEL ** -0.5),
        'shift_mu': jax.random.uniform(ks[7], (L, N_SHIFT), f32),
        'w0': jax.random.uniform(ks[8], (L, RWKV_WIDTH), f32, -5.0, 1.0),
        'w_lora_up': nrm(ks[9], (L, DECAY_LORA, RWKV_WIDTH), DECAY_LORA ** -0.5),
        'a0': nrm(ks[10], (L, RWKV_WIDTH), 0.5),
        'a_lora_up': nrm(ks[11], (L, ICLR_LORA, RWKV_WIDTH), ICLR_LORA ** -0.5),
        'g_lora_up': nrm(ks[12], (L, GATE_LORA, RWKV_WIDTH), GATE_LORA ** -0.5),
        'k_k': 0.85 + nrm(ks[13], (L, RWKV_WIDTH), 0.1),
        'k_a': 1.0 + nrm(ks[14], (L, RWKV_WIDTH), 0.1),
        'r_k': nrm(ks[15], (L, RWKV_HEADS, RWKV_HEAD), 0.1),
        'lnx_w': 1.0 + nrm(ks[16], (L, RWKV_WIDTH), 0.05),
        'lnx_b': nrm(ks[17], (L, RWKV_WIDTH), 0.02),
        'q_norm_w': 1.0 + nrm(ks[18], (L, ATTN_HEAD), 0.05),
        'k_norm_w': 1.0 + nrm(ks[19], (L, ATTN_HEAD), 0.05),
        'sinks': nrm(ks[20], (L, ATTN_Q_HEADS), 0.5),
        'w_out': nrm(ks[21], (L, D_MODEL, D_MODEL), D_MODEL ** -0.5),
        'peer_w_q': nrm(ks[22], (L, D_MODEL, PEER_HEADS * PEER_QDIM), D_MODEL ** -0.5),
        'peer_keys_1': nrm(ks[23], (L, N_KEYS, PEER_HALF), PEER_HALF ** -0.5),
        'peer_keys_2': nrm(ks[24], (L, N_KEYS, PEER_HALF), PEER_HALF ** -0.5),
        'peer_u': nrm(ks[25], (L, N_EXPERTS, D_MODEL), D_MODEL ** -0.5),
        'peer_v': nrm(ks[26], (L, N_EXPERTS, D_MODEL), PEER_HEADS ** -0.5),
    }


def reference(x, c, ada_w, ada_b, norm1_w, norm2_w, w_in, shift_mu, w0, w_lora_up, a0,
              a_lora_up, g_lora_up, k_k, k_a, r_k, lnx_w, lnx_b, q_norm_w, k_norm_w, sinks,
              w_out, peer_w_q, peer_keys_1, peer_keys_2, peer_u, peer_v):
    cond = jax.nn.silu(c)
    for l in range(DEPTH):
        ada = cond @ ada_w[l] + ada_b[l]
        sh1, sc1, gt1, sh2, sc2, gt2 = [t[:, None, :] for t in jnp.split(ada, N_ADA, axis=-1)]

        h = rms_norm(x, norm1_w[l]) * (1.0 + sc1) + sh1
        proj = h @ w_in[l]
        shifted, rest = proj[..., :N_SHIFT], proj[..., N_SHIFT:]
        shifted = shifted + (token_shift(shifted) - shifted) * shift_mu[l]
        pr, pk, pv, pw, pa, pg = jnp.split(shifted, SHIFT_SPLITS, axis=-1)
        aq, ak, av, gate_a, gate_b = jnp.split(rest, REST_SPLITS, axis=-1)
        y_a = rwkv7_mixer(pr, pk, pv, pw, pa, pg, w0[l], w_lora_up[l], a0[l], a_lora_up[l],
                          g_lora_up[l], k_k[l], k_a[l], r_k[l], lnx_w[l], lnx_b[l])
        y_b = swa_sink_attention(aq, ak, av, q_norm_w[l], k_norm_w[l], sinks[l])
        mixed = jax.nn.sigmoid(gate_a) * y_a + jax.nn.sigmoid(gate_b) * y_b
        x = x + gt1 * (mixed @ w_out[l])

        h2 = rms_norm(x, norm2_w[l]) * (1.0 + sc2) + sh2
        x = x + gt2 * peer_ffn(h2, peer_w_q[l], peer_keys_1[l], peer_keys_2[l], peer_u[l], peer_v[l])
    return x
```

```python
import functools

import numpy as np
import jax
import jax.numpy as jnp
from jax import lax
from jax.experimental import pallas as pl
from jax.experimental.pallas import tpu as pltpu

F32 = jnp.float32
BF16 = jnp.bfloat16
I32 = jnp.int32
HIGHEST = lax.Precision.HIGHEST

RWKV_HEAD = 64
DECAY_LORA = 64
ICLR_LORA = 64
GATE_LORA = 128
LNX_EPS = 64e-5
ATTN_HEAD = 64
ATTN_Q_HEADS = 16
ATTN_KV_HEADS = 2
ATTN_GROUP = ATTN_Q_HEADS // ATTN_KV_HEADS
WINDOW = 128
BLOCK = 128
ATTN_SCALE = ATTN_HEAD ** -0.5
NEG_INF = -1e30
N_KEYS = 128
PEER_HEADS = 8
PEER_HALF = 128
PEER_TOPK = 16
NORM_EPS = 1e-6
N_ADA = 6

LANES = 128
SUBLANES = 8
VMEM_LIMIT = 56 * 1024 * 1024

RWKV_CHUNK = 64
ROW_TILE = 256
TOPK_TILE = 128
PEER_TILE = 128
PACK_TILE = 512


def _nt(a, b):
    return lax.dot_general(a.astype(BF16), b.astype(BF16), (((1,), (1,)), ((), ())),
                           preferred_element_type=F32)


def _nn(a, b):
    return jnp.dot(a.astype(BF16), b.astype(BF16), preferred_element_type=F32)


def _tn(a, b):
    return lax.dot_general(a.astype(BF16), b.astype(BF16), (((0,), (0,)), ((), ())),
                           preferred_element_type=F32)


def _dot_f32(a, b):
    return jnp.dot(a, b, precision=HIGHEST, preferred_element_type=F32)


def _ada_kernel(c_ref, w_ref, b_ref, o_ref):
    c = c_ref[...]
    cond = c * jax.nn.sigmoid(c)
    o_ref[...] = _dot_f32(cond, w_ref[...]) + b_ref[...]


def _ada(c, ada_w, ada_b):
    bsz, d = c.shape
    n = ada_w.shape[1]
    return pl.pallas_call(
        _ada_kernel,
        grid=(n // d,),
        in_specs=[pl.BlockSpec((bsz, d), lambda j: (0, 0)),
                  pl.BlockSpec((d, d), lambda j: (0, j)),
                  pl.BlockSpec((1, d), lambda j: (0, j))],
        out_specs=pl.BlockSpec((bsz, d), lambda j: (0, j)),
        out_shape=jax.ShapeDtypeStruct((bsz, n), F32),
        name="ada",
    )(c, ada_w, ada_b.reshape(1, n))


def _col_chunks(width, step):
    return [(c0, min(step, width - c0)) for c0 in range(0, width, step)]


def _inproj_kernel(x_ref, ada_ref, nw_ref, w_ref, mu_ref, sh_ref, rest_ref, carry_ref,
                   *, n_shift, n_rest, tiles_per_seq):
    i = pl.program_id(0)
    x = x_ref[...]
    tm = x.shape[0]
    ms = jnp.mean(x * x, axis=-1, keepdims=True)
    ada = ada_ref[0]
    h = x * lax.rsqrt(ms + NORM_EPS) * nw_ref[...] * (1.0 + ada[1:2]) + ada[0:1]
    hb = h.astype(BF16)
    row0 = lax.broadcasted_iota(I32, (tm, 1), 0) == 0
    seq_start = (i % tiles_per_seq) == 0
    for c0, cw in _col_chunks(n_shift, 1024):
        p = jnp.dot(hb, w_ref[:, c0:c0 + cw], preferred_element_type=F32)
        prev_last = jnp.where(seq_start, 0.0, carry_ref[:, c0:c0 + cw])
        carry_ref[:, c0:c0 + cw] = p[tm - 1:tm, :]
        prev = jnp.where(row0, prev_last, pltpu.roll(p, 1, axis=0))
        sh_ref[:, c0:c0 + cw] = p + (prev - p) * mu_ref[:, c0:c0 + cw]
    for c0, cw in _col_chunks(n_rest, 1024):
        rest_ref[:, c0:c0 + cw] = jnp.dot(hb, w_ref[:, n_shift + c0:n_shift + c0 + cw],
                                          preferred_element_type=F32)


def _inproj(x2, ada3, norm_w, w_bf, mu, seq, n_shift):
    t, d = x2.shape
    n_rest = w_bf.shape[1] - n_shift
    tm = ROW_TILE
    tiles_per_seq = seq // tm
    kern = functools.partial(_inproj_kernel, n_shift=n_shift, n_rest=n_rest,
                             tiles_per_seq=tiles_per_seq)
    return pl.pallas_call(
        kern,
        grid=(t // tm,),
        in_specs=[pl.BlockSpec((tm, d), lambda i: (i, 0)),
                  pl.BlockSpec((1, N_ADA, d), lambda i: (i // tiles_per_seq, 0, 0)),
                  pl.BlockSpec((1, d), lambda i: (0, 0)),
                  pl.BlockSpec(w_bf.shape, lambda i: (0, 0), pipeline_mode=pl.Buffered(1)),
                  pl.BlockSpec((1, n_shift), lambda i: (0, 0))],
        out_specs=[pl.BlockSpec((tm, n_shift), lambda i: (i, 0)),
                   pl.BlockSpec((tm, n_rest), lambda i: (i, 0))],
        out_shape=[jax.ShapeDtypeStruct((t, n_shift), F32),
                   jax.ShapeDtypeStruct((t, n_rest), F32)],
        scratch_shapes=[pltpu.VMEM((1, n_shift), F32)],
        compiler_params=pltpu.CompilerParams(dimension_semantics=("arbitrary",),
                                             vmem_limit_bytes=VMEM_LIMIT),
        name="inproj",
    )(x2, ada3, norm_w.reshape(1, d), w_bf, mu.reshape(1, n_shift))


def _softplus(z):
    return jnp.maximum(z, 0.0) + jnp.log1p(jnp.exp(-jnp.abs(z)))


def _rwkv_head(r, k, v, kkraw, a, cum, lw, s0, masks):
    strict, incl, diag, levels = masks
    c = r.shape[0]
    nrm = jnp.sqrt(jnp.sum(kkraw * kkraw, axis=-1, keepdims=True))
    kk = kkraw / jnp.maximum(nrm, 1e-12)
    w_in = jnp.exp(cum)
    w_ex = jnp.exp(cum - lw)
    w_inv = jnp.exp(-cum)
    w_end = jnp.exp(cum[c - 1:c, :] - cum)
    bb = kk * a
    rt = r * w_in
    at = -kk * w_ex
    bt = bb * w_inv
    kt = k * w_inv
    ab = _nt(at, bt)
    ak = _nt(at, kt)
    rb = _nt(rt, bt)
    rk = _nt(rt, kt)
    lab = jnp.where(strict, ab, 0.0)
    inv = jnp.where(levels[0], lab, 0.0) + jnp.where(diag, 1.0, 0.0)
    for lvl in levels[1:]:
        inv = inv + _nn(_nn(inv, jnp.where(lvl, lab, 0.0)), inv)
    rhs = _nt(at, s0) + _nn(jnp.where(strict, ak, 0.0), v)
    u = _nn(inv, rhs)
    y = _nt(rt, s0) + _nn(jnp.where(incl, rb, 0.0), u) + _nn(jnp.where(incl, rk, 0.0), v)
    s_new = s0 * w_in[c - 1:c, :] + _tn(u, bb * w_end) + _tn(v, k * w_end)
    return y, s_new


def _rwkv_kernel(pr_ref, pk_ref, pv_ref, tail_ref, w0_ref, wup_ref, a0_ref, aup_ref, gup_ref,
                 kk_ref, ka_ref, rk_ref, lnw_ref, lnb_ref, o_ref,
                 state_ref, r_s, k_s, v_s, kkraw_s, a_s, cum_s, lwd_s, g_s, o_s):
    j = pl.program_id(1)
    c = pr_ref.shape[0]
    n_pairs = pr_ref.shape[1] // LANES
    heads_per_pair = LANES // RWKV_HEAD

    @pl.when(j == 0)
    def _():
        state_ref[...] = jnp.zeros_like(state_ref)

    tail = tail_ref[...]
    pw = tail[:, :DECAY_LORA]
    pa = tail[:, DECAY_LORA:DECAY_LORA + ICLR_LORA]
    pg = tail[:, DECAY_LORA + ICLR_LORA:]
    pk = pk_ref[...]
    w_log = -_softplus(-(w0_ref[...] + _dot_f32(jnp.tanh(pw), wup_ref[...]))) - 0.5
    lw = -jnp.exp(w_log)
    a = jax.nn.sigmoid(a0_ref[...] + _dot_f32(pa, aup_ref[...]))
    row = lax.broadcasted_iota(I32, (c, c), 0)
    col = lax.broadcasted_iota(I32, (c, c), 1)
    strict = row > col
    incl = row >= col
    diag = row == col
    cum = _dot_f32(incl.astype(F32), lw)
    k = pk * (1.0 + (a - 1.0) * ka_ref[...])
    kkraw = pk * kk_ref[...]
    g = _dot_f32(jax.nn.sigmoid(pg), gup_ref[...])
    pr = pr_ref[...]
    pv = pv_ref[...]
    for p in range(n_pairs):
        ls = slice(p * LANES, (p + 1) * LANES)
        for dst, src in ((r_s, pr), (k_s, k), (v_s, pv), (kkraw_s, kkraw), (a_s, a),
                         (cum_s, cum), (lwd_s, lw), (g_s, g)):
            dst[p] = src[:, ls]

    levels = []
    shift = 0
    while (1 << shift) < c:
        levels.append((jnp.right_shift(row, shift + 1) == jnp.right_shift(col, shift + 1))
                      & ((jnp.right_shift(row, shift) & 1) == 1)
                      & ((jnp.right_shift(col, shift) & 1) == 0))
        shift += 1
    masks = (strict, incl, diag, levels)

    def pair_body(hp, carry):
        r2, k2, v2 = r_s[hp], k_s[hp], v_s[hp]
        kk2, a2, cum2, lw2, g2 = kkraw_s[hp], a_s[hp], cum_s[hp], lwd_s[hp], g_s[hp]
        rk2, lnw2, lnb2 = rk_ref[hp], lnw_ref[hp], lnb_ref[hp]
        outs = []
        for q in range(heads_per_pair):
            hs = slice(q * RWKV_HEAD, (q + 1) * RWKV_HEAD)
            hidx = hp * heads_per_pair + q
            r, kh, v = r2[:, hs], k2[:, hs], v2[:, hs]
            y, s_new = _rwkv_head(r, kh, v, kk2[:, hs], a2[:, hs], cum2[:, hs], lw2[:, hs],
                                  state_ref[hidx], masks)
            state_ref[hidx] = s_new
            mu = jnp.mean(y, axis=-1, keepdims=True)
            var = jnp.mean(jnp.square(y - mu), axis=-1, keepdims=True)
            yn = (y - mu) * lax.rsqrt(var + LNX_EPS) * lnw2[:, hs] + lnb2[:, hs]
            bonus = jnp.sum(r * kh * rk2[:, hs], axis=-1, keepdims=True) * v
            outs.append((yn + bonus) * g2[:, hs])
        o_s[hp] = jnp.concatenate(outs, axis=1)
        return carry

    lax.fori_loop(0, n_pairs, pair_body, 0)
    for p in range(n_pairs):
        o_ref[:, p * LANES:(p + 1) * LANES] = o_s[p]


def _rwkv(sh, bsz, seq, width, params):
    (w0, w_up, a0, a_up, g_up, k_k, k_a, r_k, lnx_w, lnx_b) = params
    c = RWKV_CHUNK
    nc = seq // c
    heads = width // RWKV_HEAD
    n_pairs = width // LANES
    tail_w = DECAY_LORA + ICLR_LORA + GATE_LORA
    tail_blk = (3 * width) // tail_w
    row = lambda b, j: b * nc + j
    vec = lambda a: a.reshape(1, width)
    pairs = lambda a: a.reshape(n_pairs, 1, LANES)
    const = lambda shape: pl.BlockSpec(shape, lambda b, j: (0,) * len(shape))
    return pl.pallas_call(
        _rwkv_kernel,
        grid=(bsz, nc),
        in_specs=[pl.BlockSpec((c, width), lambda b, j: (row(b, j), 0)),
                  pl.BlockSpec((c, width), lambda b, j: (row(b, j), 1)),
                  pl.BlockSpec((c, width), lambda b, j: (row(b, j), 2)),
                  pl.BlockSpec((c, tail_w), lambda b, j: (row(b, j), tail_blk)),
                  const((1, width)), const((DECAY_LORA, width)),
                  const((1, width)), const((ICLR_LORA, width)), const((GATE_LORA, width)),
                  const((1, width)), const((1, width)),
                  const((n_pairs, 1, LANES)), const((n_pairs, 1, LANES)), const((n_pairs, 1, LANES))],
        out_specs=pl.BlockSpec((c, width), lambda b, j: (row(b, j), 0)),
        out_shape=jax.ShapeDtypeStruct((bsz * seq, width), F32),
        scratch_shapes=[pltpu.VMEM((heads, RWKV_HEAD, RWKV_HEAD), F32)]
                       + [pltpu.VMEM((n_pairs, c, LANES), F32)] * 9,
        compiler_params=pltpu.CompilerParams(dimension_semantics=("arbitrary", "arbitrary"),
                                             vmem_limit_bytes=VMEM_LIMIT),
        name="rwkv",
    )(sh, sh, sh, sh, vec(w0), w_up, vec(a0), a_up, g_up, vec(k_k), vec(k_a), pairs(r_k),
      pairs(lnx_w), pairs(lnx_b))


def _head_rms(x, w):
    return x * lax.rsqrt(jnp.mean(x * x, axis=-1, keepdims=True) + NORM_EPS) * w


def _attn_kernel(sink_ref, q_ref, kvc_ref, kvp_ref, qw_ref, kw_ref, o_ref):
    blk = pl.program_id(1)
    q = q_ref[...]
    kvc = kvc_ref[...]
    kvp = kvp_ref[...]
    kv_w = ATTN_KV_HEADS * ATTN_HEAD
    qi = lax.broadcasted_iota(I32, (BLOCK, 2 * BLOCK), 0)
    kj = lax.broadcasted_iota(I32, (BLOCK, 2 * BLOCK), 1)
    rel = kj - qi
    valid = (rel >= BLOCK - WINDOW + 1) & (rel <= BLOCK) & (blk * BLOCK - BLOCK + kj >= 0)
    for g in range(ATTN_KV_HEADS):
        ks = slice(g * ATTN_HEAD, (g + 1) * ATTN_HEAD)
        vs = slice(kv_w + g * ATTN_HEAD, kv_w + (g + 1) * ATTN_HEAD)
        kband = _head_rms(jnp.concatenate([kvp[:, ks], kvc[:, ks]], axis=0), kw_ref[...])
        vband = jnp.concatenate([kvp[:, vs], kvc[:, vs]], axis=0)
        for n in range(ATTN_GROUP):
            hq = g * ATTN_GROUP + n
            qh = _head_rms(q[:, hq * ATTN_HEAD:(hq + 1) * ATTN_HEAD], qw_ref[...])
            s = jnp.where(valid, _nt(qh, kband) * ATTN_SCALE, NEG_INF)
            sink = sink_ref[hq]
            m = jnp.maximum(jnp.max(s, axis=-1, keepdims=True), sink)
            p = jnp.exp(s - m)
            denom = jnp.sum(p, axis=-1, keepdims=True) + jnp.exp(sink - m)
            o_ref[:, hq * ATTN_HEAD:(hq + 1) * ATTN_HEAD] = _nn(p, vband) / denom


def _attn(rest, bsz, seq, width, q_norm_w, k_norm_w, sinks):
    nb = seq // BLOCK
    kv_w = 2 * ATTN_KV_HEADS * ATTN_HEAD
    kv_blk = (3 * width) // kv_w
    return pl.pallas_call(
        _attn_kernel,
        grid=(bsz, nb),
        in_specs=[pl.BlockSpec(memory_space=pltpu.SMEM),
                  pl.BlockSpec((BLOCK, width), lambda b, j: (b * nb + j, 0)),
                  pl.BlockSpec((BLOCK, kv_w), lambda b, j: (b * nb + j, kv_blk)),
                  pl.BlockSpec((BLOCK, kv_w), lambda b, j: (b * nb + jnp.maximum(j - 1, 0), kv_blk)),
                  pl.BlockSpec((1, ATTN_HEAD), lambda b, j: (0, 0)),
                  pl.BlockSpec((1, ATTN_HEAD), lambda b, j: (0, 0))],
        out_specs=pl.BlockSpec((BLOCK, width), lambda b, j: (b * nb + j, 0)),
        out_shape=jax.ShapeDtypeStruct((bsz * seq, width), F32),
        compiler_params=pltpu.CompilerParams(dimension_semantics=("arbitrary", "arbitrary"),
                                             vmem_limit_bytes=VMEM_LIMIT),
        name="attn",
    )(sinks, rest, rest, rest, q_norm_w.reshape(1, ATTN_HEAD), k_norm_w.reshape(1, ATTN_HEAD))


def _outproj_kernel(ya_ref, yb_ref, ga_ref, gb_ref, x_ref, ada_ref, wo_ref, nw_ref, wq_ref,
                    k1_ref, k2_ref, x1_ref, h2_ref, sct_ref):
    ada = ada_ref[0]
    mixed = jax.nn.sigmoid(ga_ref[...]) * ya_ref[...] + jax.nn.sigmoid(gb_ref[...]) * yb_ref[...]
    x1 = x_ref[...] + ada[2:3] * _nn(mixed, wo_ref[...])
    x1_ref[...] = x1
    ms = jnp.mean(x1 * x1, axis=-1, keepdims=True)
    h2 = x1 * lax.rsqrt(ms + NORM_EPS) * nw_ref[...] * (1.0 + ada[4:5]) + ada[3:4]
    h2_ref[...] = h2
    q = _nn(h2, wq_ref[...])
    for ch in range(q.shape[1] // PEER_HALF):
        keys = k1_ref if ch % 2 == 0 else k2_ref
        cs = slice(ch * PEER_HALF, (ch + 1) * PEER_HALF)
        sct_ref[cs, :] = _nt(keys[...], q[:, cs])


def _outproj(ya, yb, rest, x2, ada3, w_out_bf, norm_w, wq_bf, keys_1, keys_2, seq):
    t, d = x2.shape
    tm = ROW_TILE
    tiles_per_seq = seq // tm
    qd = wq_bf.shape[1]
    rowblk = lambda c: pl.BlockSpec((tm, d), lambda i: (i, c))
    const = lambda shape: pl.BlockSpec(shape, lambda i: (0, 0))
    return pl.pallas_call(
        _outproj_kernel,
        grid=(t // tm,),
        in_specs=[rowblk(0), rowblk(0), rowblk(1), rowblk(2), rowblk(0),
                  pl.BlockSpec((1, N_ADA, d), lambda i: (i // tiles_per_seq, 0, 0)),
                  const((d, d)), const((1, d)), const((d, qd)),
                  const(keys_1.shape), const(keys_2.shape)],
        out_specs=[rowblk(0), rowblk(0), pl.BlockSpec((qd, tm), lambda i: (0, i))],
        out_shape=[jax.ShapeDtypeStruct((t, d), F32), jax.ShapeDtypeStruct((t, d), F32),
                   jax.ShapeDtypeStruct((qd, t), F32)],
        compiler_params=pltpu.CompilerParams(dimension_semantics=("arbitrary",),
                                             vmem_limit_bytes=VMEM_LIMIT),
        name="outproj",
    )(ya, yb, rest, rest, x2, ada3, w_out_bf, norm_w.reshape(1, d), wq_bf, keys_1, keys_2)


def _top16_rows(s):
    n = s.shape[0]
    iota = lax.broadcasted_iota(I32, s.shape, 0)
    vals, idxs = [], []
    for _ in range(PEER_TOPK):
        m = jnp.max(s, axis=0, keepdims=True)
        i = jnp.min(jnp.where(s == m, iota, n), axis=0, keepdims=True)
        s = jnp.where(iota == i, -jnp.inf, s)
        vals.append(m)
        idxs.append(i)
    return jnp.concatenate(vals, axis=0), jnp.concatenate(idxs, axis=0)


def _pick_rows(table, sel):
    iota = lax.broadcasted_iota(I32, table.shape, 0)
    rows = []
    for k in range(sel.shape[0]):
        rows.append(jnp.max(jnp.where(iota == sel[k:k + 1, :], table, -1), axis=0, keepdims=True))
    return jnp.concatenate(rows, axis=0)


def _topk_kernel(sct_ref, idx_ref, gate_ref):
    idx_all, gate_all = [], []
    for h in range(PEER_HEADS):
        base = h * 2 * N_KEYS
        v1, i1 = _top16_rows(sct_ref[base:base + N_KEYS, :])
        v2, i2 = _top16_rows(sct_ref[base + N_KEYS:base + 2 * N_KEYS, :])
        cand = jnp.concatenate([v1[i:i + 1, :] + v2 for i in range(PEER_TOPK)], axis=0)
        sc, pos = _top16_rows(cand)
        e1 = _pick_rows(i1, jnp.right_shift(pos, 4))
        e2 = _pick_rows(i2, pos & (PEER_TOPK - 1))
        idx_all.append(e1 * N_KEYS + e2)
        ex = jnp.exp(sc - sc[0:1, :])
        gate_all.append(ex / jnp.sum(ex, axis=0, keepdims=True))
    idx_ref[...] = jnp.concatenate(idx_all, axis=0)
    gate_ref[...] = jnp.concatenate(gate_all, axis=0).T


def _topk(sct):
    qd, t = sct.shape
    tk = TOPK_TILE
    ne = PEER_HEADS * PEER_TOPK
    return pl.pallas_call(
        _topk_kernel,
        grid=(t // tk,),
        in_specs=[pl.BlockSpec((qd, tk), lambda i: (0, i))],
        out_specs=[pl.BlockSpec((ne, tk), lambda i: (0, i)),
                   pl.BlockSpec((tk, ne), lambda i: (i, 0))],
        out_shape=[jax.ShapeDtypeStruct((ne, t), I32), jax.ShapeDtypeStruct((t, ne), F32)],
        compiler_params=pltpu.CompilerParams(dimension_semantics=("arbitrary",),
                                             vmem_limit_bytes=VMEM_LIMIT),
        name="topk",
    )(sct)


def _cast_kernel(x_ref, o_ref):
    o_ref[...] = x_ref[...].astype(BF16)


def _table_bf16(tab):
    n, d = tab.shape
    out = pl.pallas_call(
        _cast_kernel,
        grid=(n // PACK_TILE,),
        in_specs=[pl.BlockSpec((PACK_TILE, d), lambda i: (i, 0))],
        out_specs=pl.BlockSpec((PACK_TILE, d), lambda i: (i, 0)),
        out_shape=jax.ShapeDtypeStruct((n, d), BF16),
        name="table_cast",
    )(tab)
    return out.reshape(n, d // LANES, LANES)


def _gelu_exact(x):
    return 0.5 * x * (1.0 + lax.erf(x * (2.0 ** -0.5)))


def _peer_u_kernel(idx_ref, x_ref, gate_ref, tab_ref, wts_ref, p_s, act_s):
    tb = x_ref.shape[0]
    ne = gate_ref.shape[1]
    ones = jnp.ones((SUBLANES, LANES), F32)

    def token(t, carry):
        xv = x_ref[t]
        for e in range(ne):
            row = tab_ref[idx_ref[e, t]].astype(F32)
            p_s[e * SUBLANES:(e + 1) * SUBLANES, :] = row * xv
        part = p_s[pl.ds(0, ne, stride=SUBLANES), :]
        for q in range(1, SUBLANES):
            part = part + p_s[pl.ds(q, ne, stride=SUBLANES), :]
        dots = lax.dot_general(ones, part, (((1,), (1,)), ((), ())), precision=HIGHEST,
                               preferred_element_type=F32)
        act_s[pl.ds(t, 1), :] = dots[0:1, :]
        return carry

    lax.fori_loop(0, tb, token, 0)
    wts_ref[...] = gate_ref[...] * _gelu_exact(act_s[...])


def _peer_u(idx_t, h3, gate, tab3):
    t = h3.shape[0]
    ne = gate.shape[1]
    tb = PEER_TILE
    return pl.pallas_call(
        _peer_u_kernel,
        grid=(t // tb,),
        in_specs=[pl.BlockSpec((ne, tb), lambda i: (0, i), memory_space=pltpu.SMEM),
                  pl.BlockSpec((tb, SUBLANES, LANES), lambda i: (i, 0, 0)),
                  pl.BlockSpec((tb, ne), lambda i: (i, 0)),
                  pl.BlockSpec(tab3.shape, lambda i: (0, 0, 0), pipeline_mode=pl.Buffered(1))],
        out_specs=pl.BlockSpec((tb, ne), lambda i: (i, 0)),
        out_shape=jax.ShapeDtypeStruct((t, ne), F32),
        scratch_shapes=[pltpu.VMEM((ne * SUBLANES, LANES), F32), pltpu.VMEM((tb, ne), F32)],
        compiler_params=pltpu.CompilerParams(dimension_semantics=("arbitrary",),
                                             vmem_limit_bytes=VMEM_LIMIT),
        name="peer_u",
    )(idx_t, h3, gate, tab3)


def _peer_v_kernel(idx_ref, wts_ref, x1_ref, gt_ref, tab_ref, o_ref):
    tb = x1_ref.shape[0]
    ne = wts_ref.shape[1]
    n_acc = 4
    gt = gt_ref[0]

    def token(t, carry):
        accs = [jnp.zeros((SUBLANES, LANES), F32) for _ in range(n_acc)]
        for e in range(ne):
            row = tab_ref[idx_ref[e, t]].astype(F32)
            accs[e % n_acc] = accs[e % n_acc] + wts_ref[t, e] * row
        peer = (accs[0] + accs[1]) + (accs[2] + accs[3])
        o_ref[t] = x1_ref[t] + gt * peer
        return carry

    lax.fori_loop(0, tb, token, 0)


def _peer_v(idx_t, wts, x13, gt3, tab3, seq):
    t = x13.shape[0]
    ne = wts.shape[1]
    tb = PEER_TILE
    tiles_per_seq = seq // tb
    return pl.pallas_call(
        _peer_v_kernel,
        grid=(t // tb,),
        in_specs=[pl.BlockSpec((ne, tb), lambda i: (0, i), memory_space=pltpu.SMEM),
                  pl.BlockSpec((tb, ne), lambda i: (i, 0), memory_space=pltpu.SMEM),
                  pl.BlockSpec((tb, SUBLANES, LANES), lambda i: (i, 0, 0)),
                  pl.BlockSpec((1, SUBLANES, LANES), lambda i: (i // tiles_per_seq, 0, 0)),
                  pl.BlockSpec(tab3.shape, lambda i: (0, 0, 0), pipeline_mode=pl.Buffered(1))],
        out_specs=pl.BlockSpec((tb, SUBLANES, LANES), lambda i: (i, 0, 0)),
        out_shape=jax.ShapeDtypeStruct((t, SUBLANES, LANES), F32),
        compiler_params=pltpu.CompilerParams(dimension_semantics=("arbitrary",),
                                             vmem_limit_bytes=VMEM_LIMIT),
        name="peer_v",
    )(idx_t, wts, x13, gt3, tab3)


def _perm_in_columns(width, kv_width):
    n_shift = 3 * width + DECAY_LORA + ICLR_LORA + GATE_LORA
    aq = n_shift
    akv = aq + width
    ga = akv + 2 * kv_width
    gb = ga + width
    return n_shift, np.concatenate([np.arange(0, n_shift), np.arange(aq, aq + width),
                                    np.arange(ga, ga + width), np.arange(gb, gb + width),
                                    np.arange(akv, akv + 2 * kv_width)])


def _layer(x, ada, norm1_w, norm2_w, w_in, shift_mu, w0, w_lora_up, a0, a_lora_up, g_lora_up,
           k_k, k_a, r_k, lnx_w, lnx_b, q_norm_w, k_norm_w, sinks, w_out, peer_w_q,
           peer_keys_1, peer_keys_2, peer_u, peer_v):
    bsz, seq, d = x.shape
    t = bsz * seq
    x2 = x.reshape(t, d)
    ada3 = ada.reshape(bsz, N_ADA, d)
    n_shift, perm = _perm_in_columns(d, ATTN_KV_HEADS * ATTN_HEAD)
    w_bf = w_in[:, perm].astype(BF16)
    sh, rest = _inproj(x2, ada3, norm1_w, w_bf, shift_mu, seq, n_shift)
    ya = _rwkv(sh, bsz, seq, d, (w0, w_lora_up, a0, a_lora_up, g_lora_up, k_k, k_a,
                                 r_k.reshape(-1), lnx_w, lnx_b))
    yb = _attn(rest, bsz, seq, d, q_norm_w, k_norm_w, sinks)
    x1, h2, sct = _outproj(ya, yb, rest, x2, ada3, w_out.astype(BF16), norm2_w,
                           peer_w_q.astype(BF16), peer_keys_1, peer_keys_2, seq)
    idx_t, gate = _topk(sct)
    sub = d // LANES
    wts = _peer_u(idx_t, h2.reshape(t, sub, LANES), gate, _table_bf16(peer_u))
    gt2 = ada3[:, 5, :].reshape(bsz, sub, LANES)
    out = _peer_v(idx_t, wts, x1.reshape(t, sub, LANES), gt2, _table_bf16(peer_v), seq)
    return out.reshape(bsz, seq, d)


def kernel(x, c, ada_w, ada_b, norm1_w, norm2_w, w_in, shift_mu, w0, w_lora_up, a0, a_lora_up, g_lora_up, k_k, k_a, r_k, lnx_w, lnx_b, q_norm_w, k_norm_w, sinks, w_out, peer_w_q, peer_keys_1, peer_keys_2, peer_u, peer_v):
    depth = ada_w.shape[0]
    for l in range(depth):
        ada = _ada(c, ada_w[l], ada_b[l])
        x = _layer(x, ada, norm1_w[l], norm2_w[l], w_in[l], shift_mu[l], w0[l], w_lora_up[l],
                   a0[l], a_lora_up[l], g_lora_up[l], k_k[l], k_a[l], r_k[l], lnx_w[l], lnx_b[l],
                   q_norm_w[l], k_norm_w[l], sinks[l], w_out[l], peer_w_q[l], peer_keys_1[l],
                   peer_keys_2[l], peer_u[l], peer_v[l])
    return x
```

```python
import functools

import numpy as np
import jax
import jax.numpy as jnp
from jax import lax
from jax.experimental import pallas as pl
from jax.experimental.pallas import tpu as pltpu

F32 = jnp.float32
BF16 = jnp.bfloat16
I32 = jnp.int32
HIGHEST = lax.Precision.HIGHEST

RWKV_HEAD = 64
DECAY_LORA = 64
ICLR_LORA = 64
GATE_LORA = 128
LNX_EPS = 64e-5
ATTN_HEAD = 64
ATTN_Q_HEADS = 16
ATTN_KV_HEADS = 2
ATTN_GROUP = ATTN_Q_HEADS // ATTN_KV_HEADS
WINDOW = 128
BLOCK = 128
ATTN_SCALE = ATTN_HEAD ** -0.5
NEG_INF = -1e30
N_KEYS = 128
PEER_HEADS = 8
PEER_HALF = 128
PEER_TOPK = 16
NORM_EPS = 1e-6
N_ADA = 6

LANES = 128
SUBLANES = 8
VMEM_LIMIT = 56 * 1024 * 1024

RWKV_CHUNK = 64
ROW_TILE = 256
TOPK_TILE = 128
PEER_TILE = 128
PEER_GROUP = 4
PACK_TILE = 512


def _nt(a, b):
    return lax.dot_general(a.astype(BF16), b.astype(BF16), (((1,), (1,)), ((), ())),
                           preferred_element_type=F32)


def _nn(a, b):
    return jnp.dot(a.astype(BF16), b.astype(BF16), preferred_element_type=F32)


def _dot_f32(a, b):
    return jnp.dot(a, b, precision=HIGHEST, preferred_element_type=F32)


def _bnt(a, b):
    return lax.dot_general(a.astype(BF16), b.astype(BF16), (((2,), (2,)), ((0,), (0,))),
                           preferred_element_type=F32)


def _bnn(a, b):
    return lax.dot_general(a.astype(BF16), b.astype(BF16), (((2,), (1,)), ((0,), (0,))),
                           preferred_element_type=F32)


def _btn(a, b):
    return lax.dot_general(a.astype(BF16), b.astype(BF16), (((1,), (1,)), ((0,), (0,))),
                           preferred_element_type=F32)


def _split2(x):
    hi = x.astype(BF16)
    lo = (x - hi.astype(F32)).astype(BF16)
    return hi, lo


def _split3(x):
    hi = x.astype(BF16)
    r1 = x - hi.astype(F32)
    mid = r1.astype(BF16)
    lo = (r1 - mid.astype(F32)).astype(BF16)
    return hi, mid, lo


def _dot_x3(a, b):
    ah, al = _split2(a)
    bh, bl = _split2(b)
    d = lambda p, q: jnp.dot(p, q, preferred_element_type=F32)
    return d(ah, bh) + (d(ah, bl) + d(al, bh))


def _dot_exact_lhs(a_bf, b):
    return sum(jnp.dot(a_bf, p, preferred_element_type=F32) for p in _split3(b))


def _ada_kernel(c_ref, w_ref, b_ref, o_ref):
    c = c_ref[...]
    cond = c * jax.nn.sigmoid(c)
    o_ref[...] = _dot_f32(cond, w_ref[...]) + b_ref[...]


def _ada(c, ada_w, ada_b):
    bsz, d = c.shape
    n = ada_w.shape[1]
    return pl.pallas_call(
        _ada_kernel,
        grid=(n // d,),
        in_specs=[pl.BlockSpec((bsz, d), lambda j: (0, 0)),
                  pl.BlockSpec((d, d), lambda j: (0, j)),
                  pl.BlockSpec((1, d), lambda j: (0, j))],
        out_specs=pl.BlockSpec((bsz, d), lambda j: (0, j)),
        out_shape=jax.ShapeDtypeStruct((bsz, n), F32),
        name="ada",
    )(c, ada_w, ada_b.reshape(1, n))


def _col_chunks(width, step):
    return [(c0, min(step, width - c0)) for c0 in range(0, width, step)]


def _inproj_kernel(x_ref, ada_ref, nw_ref, w_ref, mu_ref, sh_ref, rest_ref, carry_ref,
                   *, n_shift, n_rest, tiles_per_seq):
    i = pl.program_id(0)
    x = x_ref[...]
    tm = x.shape[0]
    ms = jnp.mean(x * x, axis=-1, keepdims=True)
    ada = ada_ref[0]
    h = x * lax.rsqrt(ms + NORM_EPS) * nw_ref[...] * (1.0 + ada[1:2]) + ada[0:1]
    hb = h.astype(BF16)
    row0 = lax.broadcasted_iota(I32, (tm, 1), 0) == 0
    seq_start = (i % tiles_per_seq) == 0
    for c0, cw in _col_chunks(n_shift, 1024):
        p = jnp.dot(hb, w_ref[:, c0:c0 + cw], preferred_element_type=F32)
        prev_last = jnp.where(seq_start, 0.0, carry_ref[:, c0:c0 + cw])
        carry_ref[:, c0:c0 + cw] = p[tm - 1:tm, :]
        prev = jnp.where(row0, prev_last, pltpu.roll(p, 1, axis=0))
        sh_ref[:, c0:c0 + cw] = p + (prev - p) * mu_ref[:, c0:c0 + cw]
    for c0, cw in _col_chunks(n_rest, 1024):
        rest_ref[:, c0:c0 + cw] = jnp.dot(hb, w_ref[:, n_shift + c0:n_shift + c0 + cw],
                                          preferred_element_type=F32)


def _inproj(x2, ada3, norm_w, w_bf, mu, seq, n_shift):
    t, d = x2.shape
    n_rest = w_bf.shape[1] - n_shift
    tm = ROW_TILE
    tiles_per_seq = seq // tm
    kern = functools.partial(_inproj_kernel, n_shift=n_shift, n_rest=n_rest,
                             tiles_per_seq=tiles_per_seq)
    return pl.pallas_call(
        kern,
        grid=(t // tm,),
        in_specs=[pl.BlockSpec((tm, d), lambda i: (i, 0)),
                  pl.BlockSpec((1, N_ADA, d), lambda i: (i // tiles_per_seq, 0, 0)),
                  pl.BlockSpec((1, d), lambda i: (0, 0)),
                  pl.BlockSpec(w_bf.shape, lambda i: (0, 0), pipeline_mode=pl.Buffered(1)),
                  pl.BlockSpec((1, n_shift), lambda i: (0, 0))],
        out_specs=[pl.BlockSpec((tm, n_shift), lambda i: (i, 0)),
                   pl.BlockSpec((tm, n_rest), lambda i: (i, 0))],
        out_shape=[jax.ShapeDtypeStruct((t, n_shift), F32),
                   jax.ShapeDtypeStruct((t, n_rest), F32)],
        scratch_shapes=[pltpu.VMEM((1, n_shift), F32)],
        compiler_params=pltpu.CompilerParams(dimension_semantics=("arbitrary",),
                                             vmem_limit_bytes=VMEM_LIMIT),
        name="inproj",
    )(x2, ada3, norm_w.reshape(1, d), w_bf, mu.reshape(1, n_shift))


def _softplus(z):
    return jnp.maximum(z, 0.0) + jnp.log1p(jnp.exp(-jnp.abs(z)))


def _rwkv_heads(r, k, v, kkraw, a, cum, lw, s0, masks):
    strict, incl, diag, levels = masks
    c = r.shape[1]
    nrm = jnp.sqrt(jnp.sum(kkraw * kkraw, axis=-1, keepdims=True))
    kk = kkraw / jnp.maximum(nrm, 1e-12)
    w_in = jnp.exp(cum)
    w_ex = jnp.exp(cum - lw)
    w_inv = jnp.exp(-cum)
    w_end = jnp.exp(cum[:, c - 1:c, :] - cum)
    bb = kk * a
    rt = r * w_in
    at = -kk * w_ex
    bt = bb * w_inv
    kt = k * w_inv
    ab = _bnt(at, bt)
    ak = _bnt(at, kt)
    rb = _bnt(rt, bt)
    rk = _bnt(rt, kt)
    lab = jnp.where(strict, ab, 0.0)
    inv = jnp.where(levels[0], lab, 0.0) + jnp.where(diag, 1.0, 0.0)
    for lvl in levels[1:]:
        inv = inv + _bnn(_bnn(inv, jnp.where(lvl, lab, 0.0)), inv)
    rhs = _bnt(at, s0) + _bnn(jnp.where(strict, ak, 0.0), v)
    u = _bnn(inv, rhs)
    y = _bnt(rt, s0) + _bnn(jnp.where(incl, rb, 0.0), u) + _bnn(jnp.where(incl, rk, 0.0), v)
    s_new = s0 * w_in[:, c - 1:c, :] + _btn(u, bb * w_end) + _btn(v, k * w_end)
    return y, s_new


def _rwkv_kernel(pr_ref, pk_ref, pv_ref, tail_ref, w0_ref, wup_ref, a0_ref, aup_ref, gup_ref,
                 kk_ref, ka_ref, rk_ref, lnw_ref, lnb_ref, o_ref, state_ref):
    j = pl.program_id(1)
    c = pr_ref.shape[0]
    n_heads = pr_ref.shape[1] // RWKV_HEAD

    @pl.when(j == 0)
    def _():
        state_ref[...] = jnp.zeros_like(state_ref)

    tail = tail_ref[...]
    pw = tail[:, :DECAY_LORA]
    pa = tail[:, DECAY_LORA:DECAY_LORA + ICLR_LORA]
    pg = tail[:, DECAY_LORA + ICLR_LORA:]
    pk = pk_ref[...]
    w_log = -_softplus(-(w0_ref[...] + _dot_x3(jnp.tanh(pw), wup_ref[...]))) - 0.5
    lw = -jnp.exp(w_log)
    a = jax.nn.sigmoid(a0_ref[...] + _dot_x3(pa, aup_ref[...]))
    row = lax.broadcasted_iota(I32, (c, c), 0)
    col = lax.broadcasted_iota(I32, (c, c), 1)
    strict = row > col
    incl = row >= col
    diag = row == col
    cum = _dot_exact_lhs(incl.astype(BF16), lw)
    k = pk * (1.0 + (a - 1.0) * ka_ref[...])
    kkraw = pk * kk_ref[...]
    g = _dot_x3(jax.nn.sigmoid(pg), gup_ref[...])

    levels = []
    shift = 0
    while (1 << shift) < c:
        levels.append((jnp.right_shift(row, shift + 1) == jnp.right_shift(col, shift + 1))
                      & ((jnp.right_shift(row, shift) & 1) == 1)
                      & ((jnp.right_shift(col, shift) & 1) == 0))
        shift += 1
    masks = (strict, incl, diag, levels)

    def heads(x):
        return jnp.stack([x[:, h * RWKV_HEAD:(h + 1) * RWKV_HEAD] for h in range(n_heads)], axis=0)

    r3, k3, v3 = heads(pr_ref[...]), heads(k), heads(pv_ref[...])
    y, s_new = _rwkv_heads(r3, k3, v3, heads(kkraw), heads(a), heads(cum), heads(lw),
                           state_ref[...], masks)
    state_ref[...] = s_new
    mu = jnp.mean(y, axis=-1, keepdims=True)
    var = jnp.mean(jnp.square(y - mu), axis=-1, keepdims=True)
    yn = (y - mu) * lax.rsqrt(var + LNX_EPS) * lnw_ref[...] + lnb_ref[...]
    bonus = jnp.sum(r3 * k3 * rk_ref[...], axis=-1, keepdims=True) * v3
    out = (yn + bonus) * heads(g)
    for h in range(n_heads):
        o_ref[:, h * RWKV_HEAD:(h + 1) * RWKV_HEAD] = out[h]


def _rwkv(sh, bsz, seq, width, params):
    (w0, w_up, a0, a_up, g_up, k_k, k_a, r_k, lnx_w, lnx_b) = params
    c = RWKV_CHUNK
    nc = seq // c
    heads = width // RWKV_HEAD
    tail_w = DECAY_LORA + ICLR_LORA + GATE_LORA
    tail_blk = (3 * width) // tail_w
    row = lambda b, j: b * nc + j
    vec = lambda a: a.reshape(1, width)
    per_head = lambda a: a.reshape(heads, 1, RWKV_HEAD)
    const = lambda shape: pl.BlockSpec(shape, lambda b, j: (0,) * len(shape))
    return pl.pallas_call(
        _rwkv_kernel,
        grid=(bsz, nc),
        in_specs=[pl.BlockSpec((c, width), lambda b, j: (row(b, j), 0)),
                  pl.BlockSpec((c, width), lambda b, j: (row(b, j), 1)),
                  pl.BlockSpec((c, width), lambda b, j: (row(b, j), 2)),
                  pl.BlockSpec((c, tail_w), lambda b, j: (row(b, j), tail_blk)),
                  const((1, width)), const((DECAY_LORA, width)),
                  const((1, width)), const((ICLR_LORA, width)), const((GATE_LORA, width)),
                  const((1, width)), const((1, width)),
                  const((heads, 1, RWKV_HEAD)), const((heads, 1, RWKV_HEAD)),
                  const((heads, 1, RWKV_HEAD))],
        out_specs=pl.BlockSpec((c, width), lambda b, j: (row(b, j), 0)),
        out_shape=jax.ShapeDtypeStruct((bsz * seq, width), F32),
        scratch_shapes=[pltpu.VMEM((heads, RWKV_HEAD, RWKV_HEAD), F32)],
        compiler_params=pltpu.CompilerParams(dimension_semantics=("arbitrary", "arbitrary"),
                                             vmem_limit_bytes=VMEM_LIMIT),
        name="rwkv",
    )(sh, sh, sh, sh, vec(w0), w_up, vec(a0), a_up, g_up, vec(k_k), vec(k_a), per_head(r_k),
      per_head(lnx_w), per_head(lnx_b))


def _head_rms(x, w):
    return x * lax.rsqrt(jnp.mean(x * x, axis=-1, keepdims=True) + NORM_EPS) * w


def _attn_kernel(sink_ref, q_ref, kvc_ref, kvp_ref, qw_ref, kw_ref, o_ref):
    blk = pl.program_id(1)
    q = q_ref[...]
    kvc = kvc_ref[...]
    kvp = kvp_ref[...]
    kv_w = ATTN_KV_HEADS * ATTN_HEAD
    qi = lax.broadcasted_iota(I32, (BLOCK, 2 * BLOCK), 0)
    kj = lax.broadcasted_iota(I32, (BLOCK, 2 * BLOCK), 1)
    rel = kj - qi
    valid = (rel >= BLOCK - WINDOW + 1) & (rel <= BLOCK) & (blk * BLOCK - BLOCK + kj >= 0)
    for g in range(ATTN_KV_HEADS):
        ks = slice(g * ATTN_HEAD, (g + 1) * ATTN_HEAD)
        vs = slice(kv_w + g * ATTN_HEAD, kv_w + (g + 1) * ATTN_HEAD)
        kband = _head_rms(jnp.concatenate([kvp[:, ks], kvc[:, ks]], axis=0), kw_ref[...])
        vband = jnp.concatenate([kvp[:, vs], kvc[:, vs]], axis=0)
        for n in range(ATTN_GROUP):
            hq = g * ATTN_GROUP + n
            qh = _head_rms(q[:, hq * ATTN_HEAD:(hq + 1) * ATTN_HEAD], qw_ref[...])
            s = jnp.where(valid, _nt(qh, kband) * ATTN_SCALE, NEG_INF)
            sink = sink_ref[hq]
            m = jnp.maximum(jnp.max(s, axis=-1, keepdims=True), sink)
            p = jnp.exp(s - m)
            denom = jnp.sum(p, axis=-1, keepdims=True) + jnp.exp(sink - m)
            o_ref[:, hq * ATTN_HEAD:(hq + 1) * ATTN_HEAD] = _nn(p, vband) / denom


def _attn(rest, bsz, seq, width, q_norm_w, k_norm_w, sinks):
    nb = seq // BLOCK
    kv_w = 2 * ATTN_KV_HEADS * ATTN_HEAD
    kv_blk = (3 * width) // kv_w
    return pl.pallas_call(
        _attn_kernel,
        grid=(bsz, nb),
        in_specs=[pl.BlockSpec(memory_space=pltpu.SMEM),
                  pl.BlockSpec((BLOCK, width), lambda b, j: (b * nb + j, 0)),
                  pl.BlockSpec((BLOCK, kv_w), lambda b, j: (b * nb + j, kv_blk)),
                  pl.BlockSpec((BLOCK, kv_w), lambda b, j: (b * nb + jnp.maximum(j - 1, 0), kv_blk)),
                  pl.BlockSpec((1, ATTN_HEAD), lambda b, j: (0, 0)),
                  pl.BlockSpec((1, ATTN_HEAD), lambda b, j: (0, 0))],
        out_specs=pl.BlockSpec((BLOCK, width), lambda b, j: (b * nb + j, 0)),
        out_shape=jax.ShapeDtypeStruct((bsz * seq, width), F32),
        compiler_params=pltpu.CompilerParams(dimension_semantics=("arbitrary", "arbitrary"),
                                             vmem_limit_bytes=VMEM_LIMIT),
        name="attn",
    )(sinks, rest, rest, rest, q_norm_w.reshape(1, ATTN_HEAD), k_norm_w.reshape(1, ATTN_HEAD))


def _outproj_kernel(ya_ref, yb_ref, ga_ref, gb_ref, x_ref, ada_ref, wo_ref, nw_ref, wq_ref,
                    k1_ref, k2_ref, x1_ref, h2_ref, sct_ref):
    ada = ada_ref[0]
    mixed = jax.nn.sigmoid(ga_ref[...]) * ya_ref[...] + jax.nn.sigmoid(gb_ref[...]) * yb_ref[...]
    x1 = x_ref[...] + ada[2:3] * _nn(mixed, wo_ref[...])
    x1_ref[...] = x1
    ms = jnp.mean(x1 * x1, axis=-1, keepdims=True)
    h2 = x1 * lax.rsqrt(ms + NORM_EPS) * nw_ref[...] * (1.0 + ada[4:5]) + ada[3:4]
    h2_ref[...] = h2
    q = _nn(h2, wq_ref[...])
    for ch in range(q.shape[1] // PEER_HALF):
        keys = k1_ref if ch % 2 == 0 else k2_ref
        cs = slice(ch * PEER_HALF, (ch + 1) * PEER_HALF)
        sct_ref[cs, :] = _nt(keys[...], q[:, cs])


def _outproj(ya, yb, rest, x2, ada3, w_out_bf, norm_w, wq_bf, keys_1, keys_2, seq):
    t, d = x2.shape
    tm = ROW_TILE
    tiles_per_seq = seq // tm
    qd = wq_bf.shape[1]
    rowblk = lambda c: pl.BlockSpec((tm, d), lambda i: (i, c))
    const = lambda shape: pl.BlockSpec(shape, lambda i: (0, 0))
    return pl.pallas_call(
        _outproj_kernel,
        grid=(t // tm,),
        in_specs=[rowblk(0), rowblk(0), rowblk(1), rowblk(2), rowblk(0),
                  pl.BlockSpec((1, N_ADA, d), lambda i: (i // tiles_per_seq, 0, 0)),
                  const((d, d)), const((1, d)), const((d, qd)),
                  const(keys_1.shape), const(keys_2.shape)],
        out_specs=[rowblk(0), rowblk(0), pl.BlockSpec((qd, tm), lambda i: (0, i))],
        out_shape=[jax.ShapeDtypeStruct((t, d), F32), jax.ShapeDtypeStruct((t, d), F32),
                   jax.ShapeDtypeStruct((qd, t), F32)],
        compiler_params=pltpu.CompilerParams(dimension_semantics=("arbitrary",),
                                             vmem_limit_bytes=VMEM_LIMIT),
        name="outproj",
    )(ya, yb, rest, rest, x2, ada3, w_out_bf, norm_w.reshape(1, d), wq_bf, keys_1, keys_2)


def _top16_rows(s, payload=None):
    n = s.shape[0]
    iota = lax.broadcasted_iota(I32, s.shape, 0)
    vals, picks = [], []
    for _ in range(PEER_TOPK):
        m = jnp.max(s, axis=0, keepdims=True)
        i = jnp.min(jnp.where(s == m, iota, n), axis=0, keepdims=True)
        hit = iota == i
        s = jnp.where(hit, -jnp.inf, s)
        vals.append(m)
        picks.append(i if payload is None else
                     jnp.max(jnp.where(hit, payload, -1), axis=0, keepdims=True))
    return jnp.concatenate(vals, axis=0), jnp.concatenate(picks, axis=0)


def _topk_kernel(sct_ref, idx_ref, gate_ref):
    keep = [PEER_TOPK // (i + 1) for i in range(PEER_TOPK)]
    pad = (-sum(keep)) % SUBLANES
    idx_all, gate_all = [], []
    for h in range(PEER_HEADS):
        base = h * 2 * N_KEYS
        v1, i1 = _top16_rows(sct_ref[base:base + N_KEYS, :])
        v2, i2 = _top16_rows(sct_ref[base + N_KEYS:base + 2 * N_KEYS, :])
        cand = [v1[i:i + 1, :] + v2[0:keep[i], :] for i in range(PEER_TOPK)]
        cidx = [i1[i:i + 1, :] * N_KEYS + i2[0:keep[i], :] for i in range(PEER_TOPK)]
        if pad:
            cand.append(jnp.full((pad, v1.shape[1]), -jnp.inf, F32))
            cidx.append(jnp.zeros((pad, v1.shape[1]), I32))
        sc, idx = _top16_rows(jnp.concatenate(cand, axis=0), jnp.concatenate(cidx, axis=0))
        idx_all.append(idx)
        ex = jnp.exp(sc - sc[0:1, :])
        gate_all.append(ex / jnp.sum(ex, axis=0, keepdims=True))
    idx_ref[...] = jnp.concatenate(idx_all, axis=0)
    gate_ref[...] = jnp.concatenate(gate_all, axis=0).T


def _topk(sct):
    qd, t = sct.shape
    tk = TOPK_TILE
    ne = PEER_HEADS * PEER_TOPK
    return pl.pallas_call(
        _topk_kernel,
        grid=(t // tk,),
        in_specs=[pl.BlockSpec((qd, tk), lambda i: (0, i))],
        out_specs=[pl.BlockSpec((ne, tk), lambda i: (0, i)),
                   pl.BlockSpec((tk, ne), lambda i: (i, 0))],
        out_shape=[jax.ShapeDtypeStruct((ne, t), I32), jax.ShapeDtypeStruct((t, ne), F32)],
        compiler_params=pltpu.CompilerParams(dimension_semantics=("arbitrary",),
                                             vmem_limit_bytes=VMEM_LIMIT),
        name="topk",
    )(sct)


def _cast_kernel(x_ref, o_ref):
    o_ref[...] = x_ref[...].astype(BF16)


def _table_bf16(tab):
    n, d = tab.shape
    out = pl.pallas_call(
        _cast_kernel,
        grid=(n // PACK_TILE,),
        in_specs=[pl.BlockSpec((PACK_TILE, d), lambda i: (i, 0))],
        out_specs=pl.BlockSpec((PACK_TILE, d), lambda i: (i, 0)),
        out_shape=jax.ShapeDtypeStruct((n, d), BF16),
        name="table_cast",
    )(tab)
    return out.reshape(n, d // LANES, LANES)


def _gelu_exact(x):
    return 0.5 * x * (1.0 + lax.erf(x * (2.0 ** -0.5)))


def _group_mask(ne):
    sub = lax.broadcasted_iota(I32, (SUBLANES, ne * SUBLANES), 0)
    col = lax.broadcasted_iota(I32, (SUBLANES, ne * SUBLANES), 1)
    return sub == (col & (SUBLANES - 1))


def _gather_rows(idx_ref, tab_ref, t, rows_ref):
    picks = idx_ref.at[:, pl.ds(t, 1)]
    for e in range(idx_ref.shape[0]):
        rows_ref[e * SUBLANES:(e + 1) * SUBLANES, :] = tab_ref[picks[e, 0]]


def _peer_u_kernel(idx_ref, x_ref, gate_ref, tab_ref, wts_ref, *scratch):
    rows, d_s = scratch[:-1], scratch[-1]
    tb = x_ref.shape[0]
    ne = gate_ref.shape[1]
    mask = _group_mask(ne)

    def reduce(t, rows_ref):
        hi, lo = _split2(x_ref[t])
        lhs = jnp.concatenate([hi, lo], axis=0)
        z = lax.dot_general(lhs, rows_ref[...], (((1,), (1,)), ((), ())),
                            preferred_element_type=F32)
        zz = z[:SUBLANES] + z[SUBLANES:]
        d_s[pl.ds(t, 1), :] = jnp.sum(jnp.where(mask, zz, 0.0), axis=0, keepdims=True)

    def group(i, carry):
        for q, rows_ref in enumerate(rows):
            _gather_rows(idx_ref, tab_ref, len(rows) * i + q, rows_ref)
        for q, rows_ref in enumerate(rows):
            reduce(len(rows) * i + q, rows_ref)
        return carry

    lax.fori_loop(0, tb // len(rows), group, 0)
    r = lax.broadcasted_iota(I32, (ne * SUBLANES, ne), 0)
    c = lax.broadcasted_iota(I32, (ne * SUBLANES, ne), 1)
    fold = (jnp.right_shift(r, 3) == c).astype(BF16)
    dots = sum(jnp.dot(p, fold, preferred_element_type=F32) for p in _split3(d_s[...]))
    wts_ref[...] = gate_ref[...] * _gelu_exact(dots)


def _peer_u(idx_t, h3, gate, tab3):
    t = h3.shape[0]
    ne = gate.shape[1]
    tb = PEER_TILE
    return pl.pallas_call(
        _peer_u_kernel,
        grid=(t // tb,),
        in_specs=[pl.BlockSpec((ne, tb), lambda i: (0, i), memory_space=pltpu.SMEM),
                  pl.BlockSpec((tb, SUBLANES, LANES), lambda i: (i, 0, 0)),
                  pl.BlockSpec((tb, ne), lambda i: (i, 0)),
                  pl.BlockSpec(tab3.shape, lambda i: (0, 0, 0), pipeline_mode=pl.Buffered(1))],
        out_specs=pl.BlockSpec((tb, ne), lambda i: (i, 0)),
        out_shape=jax.ShapeDtypeStruct((t, ne), F32),
        scratch_shapes=[pltpu.VMEM((ne * SUBLANES, LANES), BF16)] * PEER_GROUP
                       + [pltpu.VMEM((tb, ne * SUBLANES), F32)],
        compiler_params=pltpu.CompilerParams(dimension_semantics=("arbitrary",),
                                             vmem_limit_bytes=VMEM_LIMIT),
        name="peer_u",
    )(idx_t, h3, gate, tab3)


def _peer_v_kernel(idx_ref, wts_ref, x1_ref, gt_ref, tab_ref, o_ref, *scratch):
    rows, w8_s = scratch[:-1], scratch[-1]
    tb = x1_ref.shape[0]
    ne = wts_ref.shape[1]
    mask = _group_mask(ne)
    gt = gt_ref[0]
    r = lax.broadcasted_iota(I32, (ne, ne * SUBLANES), 0)
    c = lax.broadcasted_iota(I32, (ne, ne * SUBLANES), 1)
    spread = (r == jnp.right_shift(c, 3)).astype(BF16)
    w8_s[...] = sum(jnp.dot(p, spread, preferred_element_type=F32) for p in _split3(wts_ref[...]))

    def combine(t, rows_ref):
        wm = jnp.where(mask, w8_s[pl.ds(t, 1), :], 0.0)
        hi, lo = _split2(wm)
        res = jnp.dot(jnp.concatenate([hi, lo], axis=0), rows_ref[...],
                      preferred_element_type=F32)
        o_ref[t] = x1_ref[t] + gt * (res[:SUBLANES] + res[SUBLANES:])

    def group(i, carry):
        for q, rows_ref in enumerate(rows):
            _gather_rows(idx_ref, tab_ref, len(rows) * i + q, rows_ref)
        for q, rows_ref in enumerate(rows):
            combine(len(rows) * i + q, rows_ref)
        return carry

    lax.fori_loop(0, tb // len(rows), group, 0)


def _peer_v(idx_t, wts, x13, gt3, tab3, seq):
    t = x13.shape[0]
    ne = wts.shape[1]
    tb = PEER_TILE
    tiles_per_seq = seq // tb
    return pl.pallas_call(
        _peer_v_kernel,
        grid=(t // tb,),
        in_specs=[pl.BlockSpec((ne, tb), lambda i: (0, i), memory_space=pltpu.SMEM),
                  pl.BlockSpec((tb, ne), lambda i: (i, 0)),
                  pl.BlockSpec((tb, SUBLANES, LANES), lambda i: (i, 0, 0)),
                  pl.BlockSpec((1, SUBLANES, LANES), lambda i: (i // tiles_per_seq, 0, 0)),
                  pl.BlockSpec(tab3.shape, lambda i: (0, 0, 0), pipeline_mode=pl.Buffered(1))],
        out_specs=pl.BlockSpec((tb, SUBLANES, LANES), lambda i: (i, 0, 0)),
        out_shape=jax.ShapeDtypeStruct((t, SUBLANES, LANES), F32),
        scratch_shapes=[pltpu.VMEM((ne * SUBLANES, LANES), BF16)] * PEER_GROUP
                       + [pltpu.VMEM((tb, ne * SUBLANES), F32)],
        compiler_params=pltpu.CompilerParams(dimension_semantics=("arbitrary",),
                                             vmem_limit_bytes=VMEM_LIMIT),
        name="peer_v",
    )(idx_t, wts, x13, gt3, tab3)


def _perm_in_columns(width, kv_width):
    n_shift = 3 * width + DECAY_LORA + ICLR_LORA + GATE_LORA
    aq = n_shift
    akv = aq + width
    ga = akv + 2 * kv_width
    gb = ga + width
    return n_shift, np.concatenate([np.arange(0, n_shift), np.arange(aq, aq + width),
                                    np.arange(ga, ga + width), np.arange(gb, gb + width),
                                    np.arange(akv, akv + 2 * kv_width)])


def _layer(x, ada, norm1_w, norm2_w, w_in, shift_mu, w0, w_lora_up, a0, a_lora_up, g_lora_up,
           k_k, k_a, r_k, lnx_w, lnx_b, q_norm_w, k_norm_w, sinks, w_out, peer_w_q,
           peer_keys_1, peer_keys_2, peer_u, peer_v):
    bsz, seq, d = x.shape
    t = bsz * seq
    x2 = x.reshape(t, d)
    ada3 = ada.reshape(bsz, N_ADA, d)
    n_shift, perm = _perm_in_columns(d, ATTN_KV_HEADS * ATTN_HEAD)
    w_bf = w_in[:, perm].astype(BF16)
    sh, rest = _inproj(x2, ada3, norm1_w, w_bf, shift_mu, seq, n_shift)
    ya = _rwkv(sh, bsz, seq, d, (w0, w_lora_up, a0, a_lora_up, g_lora_up, k_k, k_a,
                                 r_k.reshape(-1), lnx_w, lnx_b))
    yb = _attn(rest, bsz, seq, d, q_norm_w, k_norm_w, sinks)
    x1, h2, sct = _outproj(ya, yb, rest, x2, ada3, w_out.astype(BF16), norm2_w,
                           peer_w_q.astype(BF16), peer_keys_1, peer_keys_2, seq)
    idx_t, gate = _topk(sct)
    sub = d // LANES
    wts = _peer_u(idx_t, h2.reshape(t, sub, LANES), gate, _table_bf16(peer_u))
    gt2 = ada3[:, 5, :].reshape(bsz, sub, LANES)
    out = _peer_v(idx_t, wts, x1.reshape(t, sub, LANES), gt2, _table_bf16(peer_v), seq)
    return out.reshape(bsz, seq, d)


def kernel(x, c, ada_w, ada_b, norm1_w, norm2_w, w_in, shift_mu, w0, w_lora_up, a0, a_lora_up, g_lora_up, k_k, k_a, r_k, lnx_w, lnx_b, q_norm_w, k_norm_w, sinks, w_out, peer_w_q, peer_keys_1, peer_keys_2, peer_u, peer_v):
    depth = ada_w.shape[0]
    for l in range(depth):
        ada = _ada(c, ada_w[l], ada_b[l])
        x = _layer(x, ada, norm1_w[l], norm2_w[l], w_in[l], shift_mu[l], w0[l], w_lora_up[l],
                   a0[l], a_lora_up[l], g_lora_up[l], k_k[l], k_a[l], r_k[l], lnx_w[l], lnx_b[l],
                   q_norm_w[l], k_norm_w[l], sinks[l], w_out[l], peer_w_q[l], peer_keys_1[l],
                   peer_keys_2[l], peer_u[l], peer_v[l])
    return x
```

```python
import functools

import numpy as np
import jax
import jax.numpy as jnp
from jax import lax
from jax.experimental import pallas as pl
from jax.experimental.pallas import tpu as pltpu

F32 = jnp.float32
BF16 = jnp.bfloat16
I32 = jnp.int32
HIGHEST = lax.Precision.HIGHEST

RWKV_HEAD = 64
DECAY_LORA = 64
ICLR_LORA = 64
GATE_LORA = 128
LNX_EPS = 64e-5
ATTN_HEAD = 64
ATTN_Q_HEADS = 16
ATTN_KV_HEADS = 2
ATTN_GROUP = ATTN_Q_HEADS // ATTN_KV_HEADS
WINDOW = 128
BLOCK = 128
ATTN_SCALE = ATTN_HEAD ** -0.5
NEG_INF = -1e30
N_KEYS = 128
PEER_HEADS = 8
PEER_HALF = 128
PEER_TOPK = 16
NORM_EPS = 1e-6
N_ADA = 6

LANES = 128
SUBLANES = 8
VMEM_LIMIT = 56 * 1024 * 1024

RWKV_CHUNK = 64
ROW_TILE = 256
TOPK_TILE = 128
PEER_TILE = 128
PEER_GROUP = 8
PACK_TILE = 512


def _nt(a, b):
    return lax.dot_general(a.astype(BF16), b.astype(BF16), (((1,), (1,)), ((), ())),
                           preferred_element_type=F32)


def _nn(a, b):
    return jnp.dot(a.astype(BF16), b.astype(BF16), preferred_element_type=F32)


def _dot_f32(a, b):
    return jnp.dot(a, b, precision=HIGHEST, preferred_element_type=F32)


def _bnt(a, b):
    return lax.dot_general(a.astype(BF16), b.astype(BF16), (((2,), (2,)), ((0,), (0,))),
                           preferred_element_type=F32)


def _bnn(a, b):
    return lax.dot_general(a.astype(BF16), b.astype(BF16), (((2,), (1,)), ((0,), (0,))),
                           preferred_element_type=F32)


def _btn(a, b):
    return lax.dot_general(a.astype(BF16), b.astype(BF16), (((1,), (1,)), ((0,), (0,))),
                           preferred_element_type=F32)


def _split2(x):
    hi = x.astype(BF16)
    lo = (x - hi.astype(F32)).astype(BF16)
    return hi, lo


def _split3(x):
    hi = x.astype(BF16)
    r1 = x - hi.astype(F32)
    mid = r1.astype(BF16)
    lo = (r1 - mid.astype(F32)).astype(BF16)
    return hi, mid, lo


def _dot_x3(a, b):
    ah, al = _split2(a)
    bh, bl = _split2(b)
    d = lambda p, q: jnp.dot(p, q, preferred_element_type=F32)
    return d(ah, bh) + (d(ah, bl) + d(al, bh))


def _dot_exact_lhs(a_bf, b):
    return sum(jnp.dot(a_bf, p, preferred_element_type=F32) for p in _split3(b))


def _ada_kernel(c_ref, w_ref, b_ref, o_ref):
    c = c_ref[...]
    cond = c * jax.nn.sigmoid(c)
    o_ref[...] = _dot_f32(cond, w_ref[...]) + b_ref[...]


def _ada(c, ada_w, ada_b):
    bsz, d = c.shape
    n = ada_w.shape[1]
    return pl.pallas_call(
        _ada_kernel,
        grid=(n // d,),
        in_specs=[pl.BlockSpec((bsz, d), lambda j: (0, 0)),
                  pl.BlockSpec((d, d), lambda j: (0, j)),
                  pl.BlockSpec((1, d), lambda j: (0, j))],
        out_specs=pl.BlockSpec((bsz, d), lambda j: (0, j)),
        out_shape=jax.ShapeDtypeStruct((bsz, n), F32),
        name="ada",
    )(c, ada_w, ada_b.reshape(1, n))


def _col_chunks(width, step):
    return [(c0, min(step, width - c0)) for c0 in range(0, width, step)]


def _inproj_kernel(x_ref, ada_ref, nw_ref, w_ref, mu_ref, sh_ref, rest_ref, carry_ref,
                   *, n_shift, n_rest, tiles_per_seq):
    i = pl.program_id(0)
    x = x_ref[...]
    tm = x.shape[0]
    ms = jnp.mean(x * x, axis=-1, keepdims=True)
    ada = ada_ref[0]
    h = x * lax.rsqrt(ms + NORM_EPS) * nw_ref[...] * (1.0 + ada[1:2]) + ada[0:1]
    hb = h.astype(BF16)
    row0 = lax.broadcasted_iota(I32, (tm, 1), 0) == 0
    seq_start = (i % tiles_per_seq) == 0
    for c0, cw in _col_chunks(n_shift, 1024):
        p = jnp.dot(hb, w_ref[:, c0:c0 + cw], preferred_element_type=F32)
        prev_last = jnp.where(seq_start, 0.0, carry_ref[:, c0:c0 + cw])
        carry_ref[:, c0:c0 + cw] = p[tm - 1:tm, :]
        prev = jnp.where(row0, prev_last, pltpu.roll(p, 1, axis=0))
        sh_ref[:, c0:c0 + cw] = p + (prev - p) * mu_ref[:, c0:c0 + cw]
    for c0, cw in _col_chunks(n_rest, 1024):
        rest_ref[:, c0:c0 + cw] = jnp.dot(hb, w_ref[:, n_shift + c0:n_shift + c0 + cw],
                                          preferred_element_type=F32)


def _inproj(x2, ada3, norm_w, w_bf, mu, seq, n_shift):
    t, d = x2.shape
    n_rest = w_bf.shape[1] - n_shift
    tm = ROW_TILE
    tiles_per_seq = seq // tm
    kern = functools.partial(_inproj_kernel, n_shift=n_shift, n_rest=n_rest,
                             tiles_per_seq=tiles_per_seq)
    return pl.pallas_call(
        kern,
        grid=(t // tm,),
        in_specs=[pl.BlockSpec((tm, d), lambda i: (i, 0)),
                  pl.BlockSpec((1, N_ADA, d), lambda i: (i // tiles_per_seq, 0, 0)),
                  pl.BlockSpec((1, d), lambda i: (0, 0)),
                  pl.BlockSpec(w_bf.shape, lambda i: (0, 0), pipeline_mode=pl.Buffered(1)),
                  pl.BlockSpec((1, n_shift), lambda i: (0, 0))],
        out_specs=[pl.BlockSpec((tm, n_shift), lambda i: (i, 0)),
                   pl.BlockSpec((tm, n_rest), lambda i: (i, 0))],
        out_shape=[jax.ShapeDtypeStruct((t, n_shift), F32),
                   jax.ShapeDtypeStruct((t, n_rest), F32)],
        scratch_shapes=[pltpu.VMEM((1, n_shift), F32)],
        compiler_params=pltpu.CompilerParams(dimension_semantics=("arbitrary",),
                                             vmem_limit_bytes=VMEM_LIMIT),
        name="inproj",
    )(x2, ada3, norm_w.reshape(1, d), w_bf, mu.reshape(1, n_shift))


def _softplus(z):
    return jnp.maximum(z, 0.0) + jnp.log1p(jnp.exp(-jnp.abs(z)))


def _rwkv_heads(r, k, v, kkraw, a, cum, lw, s0, masks):
    strict, incl, diag, levels = masks
    c = r.shape[1]
    nrm = jnp.sqrt(jnp.sum(kkraw * kkraw, axis=-1, keepdims=True))
    kk = kkraw / jnp.maximum(nrm, 1e-12)
    w_in = jnp.exp(cum)
    w_ex = jnp.exp(cum - lw)
    w_inv = jnp.exp(-cum)
    w_end = jnp.exp(cum[:, c - 1:c, :] - cum)
    bb = kk * a
    rt = r * w_in
    at = -kk * w_ex
    bt = bb * w_inv
    kt = k * w_inv
    ab = _bnt(at, bt)
    ak = _bnt(at, kt)
    rb = _bnt(rt, bt)
    rk = _bnt(rt, kt)
    lab = jnp.where(strict, ab, 0.0)
    inv = jnp.where(levels[0], lab, 0.0) + jnp.where(diag, 1.0, 0.0)
    for lvl in levels[1:]:
        inv = inv + _bnn(_bnn(inv, jnp.where(lvl, lab, 0.0)), inv)
    rhs = _bnt(at, s0) + _bnn(jnp.where(strict, ak, 0.0), v)
    u = _bnn(inv, rhs)
    y = _bnt(rt, s0) + _bnn(jnp.where(incl, rb, 0.0), u) + _bnn(jnp.where(incl, rk, 0.0), v)
    s_new = s0 * w_in[:, c - 1:c, :] + _btn(u, bb * w_end) + _btn(v, k * w_end)
    return y, s_new


def _rwkv_kernel(pr_ref, pk_ref, pv_ref, tail_ref, w0_ref, wup_ref, a0_ref, aup_ref, gup_ref,
                 kk_ref, ka_ref, rk_ref, lnw_ref, lnb_ref, o_ref, state_ref):
    j = pl.program_id(1)
    c = pr_ref.shape[0]
    n_heads = pr_ref.shape[1] // RWKV_HEAD

    @pl.when(j == 0)
    def _():
        state_ref[...] = jnp.zeros_like(state_ref)

    tail = tail_ref[...]
    pw = tail[:, :DECAY_LORA]
    pa = tail[:, DECAY_LORA:DECAY_LORA + ICLR_LORA]
    pg = tail[:, DECAY_LORA + ICLR_LORA:]
    pk = pk_ref[...]
    w_log = -_softplus(-(w0_ref[...] + _dot_x3(jnp.tanh(pw), wup_ref[...]))) - 0.5
    lw = -jnp.exp(w_log)
    a = jax.nn.sigmoid(a0_ref[...] + _dot_x3(pa, aup_ref[...]))
    row = lax.broadcasted_iota(I32, (c, c), 0)
    col = lax.broadcasted_iota(I32, (c, c), 1)
    strict = row > col
    incl = row >= col
    diag = row == col
    cum = _dot_exact_lhs(incl.astype(BF16), lw)
    k = pk * (1.0 + (a - 1.0) * ka_ref[...])
    kkraw = pk * kk_ref[...]
    g = _dot_x3(jax.nn.sigmoid(pg), gup_ref[...])

    levels = []
    shift = 0
    while (1 << shift) < c:
        levels.append((jnp.right_shift(row, shift + 1) == jnp.right_shift(col, shift + 1))
                      & ((jnp.right_shift(row, shift) & 1) == 1)
                      & ((jnp.right_shift(col, shift) & 1) == 0))
        shift += 1
    masks = (strict, incl, diag, levels)

    def heads(x):
        return jnp.stack([x[:, h * RWKV_HEAD:(h + 1) * RWKV_HEAD] for h in range(n_heads)], axis=0)

    r3, k3, v3 = heads(pr_ref[...]), heads(k), heads(pv_ref[...])
    y, s_new = _rwkv_heads(r3, k3, v3, heads(kkraw), heads(a), heads(cum), heads(lw),
                           state_ref[...], masks)
    state_ref[...] = s_new
    mu = jnp.mean(y, axis=-1, keepdims=True)
    var = jnp.mean(jnp.square(y - mu), axis=-1, keepdims=True)
    yn = (y - mu) * lax.rsqrt(var + LNX_EPS) * lnw_ref[...] + lnb_ref[...]
    bonus = jnp.sum(r3 * k3 * rk_ref[...], axis=-1, keepdims=True) * v3
    out = (yn + bonus) * heads(g)
    for h in range(n_heads):
        o_ref[:, h * RWKV_HEAD:(h + 1) * RWKV_HEAD] = out[h]


def _rwkv(sh, bsz, seq, width, params):
    (w0, w_up, a0, a_up, g_up, k_k, k_a, r_k, lnx_w, lnx_b) = params
    c = RWKV_CHUNK
    nc = seq // c
    heads = width // RWKV_HEAD
    tail_w = DECAY_LORA + ICLR_LORA + GATE_LORA
    tail_blk = (3 * width) // tail_w
    row = lambda b, j: b * nc + j
    vec = lambda a: a.reshape(1, width)
    per_head = lambda a: a.reshape(heads, 1, RWKV_HEAD)
    const = lambda shape: pl.BlockSpec(shape, lambda b, j: (0,) * len(shape))
    return pl.pallas_call(
        _rwkv_kernel,
        grid=(bsz, nc),
        in_specs=[pl.BlockSpec((c, width), lambda b, j: (row(b, j), 0)),
                  pl.BlockSpec((c, width), lambda b, j: (row(b, j), 1)),
                  pl.BlockSpec((c, width), lambda b, j: (row(b, j), 2)),
                  pl.BlockSpec((c, tail_w), lambda b, j: (row(b, j), tail_blk)),
                  const((1, width)), const((DECAY_LORA, width)),
                  const((1, width)), const((ICLR_LORA, width)), const((GATE_LORA, width)),
                  const((1, width)), const((1, width)),
                  const((heads, 1, RWKV_HEAD)), const((heads, 1, RWKV_HEAD)),
                  const((heads, 1, RWKV_HEAD))],
        out_specs=pl.BlockSpec((c, width), lambda b, j: (row(b, j), 0)),
        out_shape=jax.ShapeDtypeStruct((bsz * seq, width), F32),
        scratch_shapes=[pltpu.VMEM((heads, RWKV_HEAD, RWKV_HEAD), F32)],
        compiler_params=pltpu.CompilerParams(dimension_semantics=("arbitrary", "arbitrary"),
                                             vmem_limit_bytes=VMEM_LIMIT),
        name="rwkv",
    )(sh, sh, sh, sh, vec(w0), w_up, vec(a0), a_up, g_up, vec(k_k), vec(k_a), per_head(r_k),
      per_head(lnx_w), per_head(lnx_b))


def _head_rms(x, w):
    return x * lax.rsqrt(jnp.mean(x * x, axis=-1, keepdims=True) + NORM_EPS) * w


def _attn_kernel(sink_ref, q_ref, kvc_ref, kvp_ref, qw_ref, kw_ref, o_ref):
    blk = pl.program_id(1)
    q = q_ref[...]
    kvc = kvc_ref[...]
    kvp = kvp_ref[...]
    kv_w = ATTN_KV_HEADS * ATTN_HEAD
    qi = lax.broadcasted_iota(I32, (BLOCK, 2 * BLOCK), 0)
    kj = lax.broadcasted_iota(I32, (BLOCK, 2 * BLOCK), 1)
    rel = kj - qi
    valid = (rel >= BLOCK - WINDOW + 1) & (rel <= BLOCK) & (blk * BLOCK - BLOCK + kj >= 0)
    for g in range(ATTN_KV_HEADS):
        ks = slice(g * ATTN_HEAD, (g + 1) * ATTN_HEAD)
        vs = slice(kv_w + g * ATTN_HEAD, kv_w + (g + 1) * ATTN_HEAD)
        kband = _head_rms(jnp.concatenate([kvp[:, ks], kvc[:, ks]], axis=0), kw_ref[...])
        vband = jnp.concatenate([kvp[:, vs], kvc[:, vs]], axis=0)
        for n in range(ATTN_GROUP):
            hq = g * ATTN_GROUP + n
            qh = _head_rms(q[:, hq * ATTN_HEAD:(hq + 1) * ATTN_HEAD], qw_ref[...])
            s = jnp.where(valid, _nt(qh, kband) * ATTN_SCALE, NEG_INF)
            sink = sink_ref[hq]
            m = jnp.maximum(jnp.max(s, axis=-1, keepdims=True), sink)
            p = jnp.exp(s - m)
            denom = jnp.sum(p, axis=-1, keepdims=True) + jnp.exp(sink - m)
            o_ref[:, hq * ATTN_HEAD:(hq + 1) * ATTN_HEAD] = _nn(p, vband) / denom


def _attn(rest, bsz, seq, width, q_norm_w, k_norm_w, sinks):
    nb = seq // BLOCK
    kv_w = 2 * ATTN_KV_HEADS * ATTN_HEAD
    kv_blk = (3 * width) // kv_w
    return pl.pallas_call(
        _attn_kernel,
        grid=(bsz, nb),
        in_specs=[pl.BlockSpec(memory_space=pltpu.SMEM),
                  pl.BlockSpec((BLOCK, width), lambda b, j: (b * nb + j, 0)),
                  pl.BlockSpec((BLOCK, kv_w), lambda b, j: (b * nb + j, kv_blk)),
                  pl.BlockSpec((BLOCK, kv_w), lambda b, j: (b * nb + jnp.maximum(j - 1, 0), kv_blk)),
                  pl.BlockSpec((1, ATTN_HEAD), lambda b, j: (0, 0)),
                  pl.BlockSpec((1, ATTN_HEAD), lambda b, j: (0, 0))],
        out_specs=pl.BlockSpec((BLOCK, width), lambda b, j: (b * nb + j, 0)),
        out_shape=jax.ShapeDtypeStruct((bsz * seq, width), F32),
        compiler_params=pltpu.CompilerParams(dimension_semantics=("arbitrary", "arbitrary"),
                                             vmem_limit_bytes=VMEM_LIMIT),
        name="attn",
    )(sinks, rest, rest, rest, q_norm_w.reshape(1, ATTN_HEAD), k_norm_w.reshape(1, ATTN_HEAD))


def _outproj_kernel(ya_ref, yb_ref, ga_ref, gb_ref, x_ref, ada_ref, wo_ref, nw_ref, wq_ref,
                    k1_ref, k2_ref, x1_ref, h2_ref, sct_ref):
    ada = ada_ref[0]
    mixed = jax.nn.sigmoid(ga_ref[...]) * ya_ref[...] + jax.nn.sigmoid(gb_ref[...]) * yb_ref[...]
    x1 = x_ref[...] + ada[2:3] * _nn(mixed, wo_ref[...])
    x1_ref[...] = x1
    ms = jnp.mean(x1 * x1, axis=-1, keepdims=True)
    h2 = x1 * lax.rsqrt(ms + NORM_EPS) * nw_ref[...] * (1.0 + ada[4:5]) + ada[3:4]
    h2_ref[...] = h2
    q = _nn(h2, wq_ref[...])
    for ch in range(q.shape[1] // PEER_HALF):
        keys = k1_ref if ch % 2 == 0 else k2_ref
        cs = slice(ch * PEER_HALF, (ch + 1) * PEER_HALF)
        sct_ref[cs, :] = _nt(keys[...], q[:, cs])


def _outproj(ya, yb, rest, x2, ada3, w_out_bf, norm_w, wq_bf, keys_1, keys_2, seq):
    t, d = x2.shape
    tm = ROW_TILE
    tiles_per_seq = seq // tm
    qd = wq_bf.shape[1]
    rowblk = lambda c: pl.BlockSpec((tm, d), lambda i: (i, c))
    const = lambda shape: pl.BlockSpec(shape, lambda i: (0, 0))
    return pl.pallas_call(
        _outproj_kernel,
        grid=(t // tm,),
        in_specs=[rowblk(0), rowblk(0), rowblk(1), rowblk(2), rowblk(0),
                  pl.BlockSpec((1, N_ADA, d), lambda i: (i // tiles_per_seq, 0, 0)),
                  const((d, d)), const((1, d)), const((d, qd)),
                  const(keys_1.shape), const(keys_2.shape)],
        out_specs=[rowblk(0), rowblk(0), pl.BlockSpec((qd, tm), lambda i: (0, i))],
        out_shape=[jax.ShapeDtypeStruct((t, d), F32), jax.ShapeDtypeStruct((t, d), F32),
                   jax.ShapeDtypeStruct((qd, t), F32)],
        compiler_params=pltpu.CompilerParams(dimension_semantics=("arbitrary",),
                                             vmem_limit_bytes=VMEM_LIMIT),
        name="outproj",
    )(ya, yb, rest, rest, x2, ada3, w_out_bf, norm_w.reshape(1, d), wq_bf, keys_1, keys_2)


def _top16_rows(s, payload=None):
    n = s.shape[0]
    iota = lax.broadcasted_iota(I32, s.shape, 0).astype(F32)
    vals, picks = [], []
    for _ in range(PEER_TOPK):
        m = jnp.max(s, axis=0, keepdims=True)
        i = jnp.min(jnp.where(s == m, iota, float(n)), axis=0, keepdims=True)
        hit = iota == i
        s = jnp.where(hit, -jnp.inf, s)
        vals.append(m)
        picks.append(i if payload is None else
                     jnp.max(jnp.where(hit, payload, -1.0), axis=0, keepdims=True))
    return jnp.concatenate(vals, axis=0), jnp.concatenate(picks, axis=0)


def _topk_kernel(sct_ref, idx_ref, gate_ref):
    keep = [PEER_TOPK // (i + 1) for i in range(PEER_TOPK)]
    pad = (-sum(keep)) % SUBLANES
    idx_all, gate_all = [], []
    for h in range(PEER_HEADS):
        base = h * 2 * N_KEYS
        v1, i1 = _top16_rows(sct_ref[base:base + N_KEYS, :])
        v2, i2 = _top16_rows(sct_ref[base + N_KEYS:base + 2 * N_KEYS, :])
        cand = [v1[i:i + 1, :] + v2[0:keep[i], :] for i in range(PEER_TOPK)]
        cidx = [i1[i:i + 1, :] * float(N_KEYS) + i2[0:keep[i], :] for i in range(PEER_TOPK)]
        if pad:
            cand.append(jnp.full((pad, v1.shape[1]), -jnp.inf, F32))
            cidx.append(jnp.zeros((pad, v1.shape[1]), F32))
        sc, idx = _top16_rows(jnp.concatenate(cand, axis=0), jnp.concatenate(cidx, axis=0))
        idx_all.append(idx.astype(I32))
        ex = jnp.exp(sc - sc[0:1, :])
        gate_all.append(ex / jnp.sum(ex, axis=0, keepdims=True))
    idx_ref[...] = jnp.concatenate(idx_all, axis=0)
    gate_ref[...] = jnp.concatenate(gate_all, axis=0).T


def _topk(sct):
    qd, t = sct.shape
    tk = TOPK_TILE
    ne = PEER_HEADS * PEER_TOPK
    return pl.pallas_call(
        _topk_kernel,
        grid=(t // tk,),
        in_specs=[pl.BlockSpec((qd, tk), lambda i: (0, i))],
        out_specs=[pl.BlockSpec((ne, tk), lambda i: (0, i)),
                   pl.BlockSpec((tk, ne), lambda i: (i, 0))],
        out_shape=[jax.ShapeDtypeStruct((ne, t), I32), jax.ShapeDtypeStruct((t, ne), F32)],
        compiler_params=pltpu.CompilerParams(dimension_semantics=("arbitrary",),
                                             vmem_limit_bytes=VMEM_LIMIT),
        name="topk",
    )(sct)


def _cast_kernel(x_ref, o_ref):
    o_ref[...] = x_ref[...].astype(BF16)


def _table_bf16(tab):
    n, d = tab.shape
    out = pl.pallas_call(
        _cast_kernel,
        grid=(n // PACK_TILE,),
        in_specs=[pl.BlockSpec((PACK_TILE, d), lambda i: (i, 0))],
        out_specs=pl.BlockSpec((PACK_TILE, d), lambda i: (i, 0)),
        out_shape=jax.ShapeDtypeStruct((n, d), BF16),
        name="table_cast",
    )(tab)
    return out.reshape(n, d // LANES, LANES)


def _gelu_exact(x):
    return 0.5 * x * (1.0 + lax.erf(x * (2.0 ** -0.5)))


def _group_mask(ne):
    sub = lax.broadcasted_iota(I32, (SUBLANES, ne * SUBLANES), 0)
    col = lax.broadcasted_iota(I32, (SUBLANES, ne * SUBLANES), 1)
    return sub == (col & (SUBLANES - 1))


def _gather_rows(idx_ref, tab_ref, t, rows_ref):
    picks = idx_ref.at[:, pl.ds(t, 1)]
    for e in range(idx_ref.shape[0]):
        rows_ref[e * SUBLANES:(e + 1) * SUBLANES, :] = tab_ref[picks[e, 0]]


def _peer_u_kernel(idx_ref, x_ref, gate_ref, tab_ref, wts_ref, *scratch):
    rows, d_s = scratch[:-1], scratch[-1]
    tb = x_ref.shape[0]
    ne = gate_ref.shape[1]
    mask = _group_mask(ne)

    def reduce(t, rows_ref):
        hi, lo = _split2(x_ref[t])
        lhs = jnp.concatenate([hi, lo], axis=0)
        z = lax.dot_general(lhs, rows_ref[...], (((1,), (1,)), ((), ())),
                            preferred_element_type=F32)
        zz = z[:SUBLANES] + z[SUBLANES:]
        d_s[pl.ds(t, 1), :] = jnp.sum(jnp.where(mask, zz, 0.0), axis=0, keepdims=True)

    def group(i, carry):
        for q, rows_ref in enumerate(rows):
            _gather_rows(idx_ref, tab_ref, len(rows) * i + q, rows_ref)
        for q, rows_ref in enumerate(rows):
            reduce(len(rows) * i + q, rows_ref)
        return carry

    lax.fori_loop(0, tb // len(rows), group, 0)
    r = lax.broadcasted_iota(I32, (ne * SUBLANES, ne), 0)
    c = lax.broadcasted_iota(I32, (ne * SUBLANES, ne), 1)
    fold = (jnp.right_shift(r, 3) == c).astype(BF16)
    dots = sum(jnp.dot(p, fold, preferred_element_type=F32) for p in _split3(d_s[...]))
    wts_ref[...] = gate_ref[...] * _gelu_exact(dots)


def _peer_u(idx_t, h3, gate, tab3):
    t = h3.shape[0]
    ne = gate.shape[1]
    tb = PEER_TILE
    return pl.pallas_call(
        _peer_u_kernel,
        grid=(t // tb,),
        in_specs=[pl.BlockSpec((ne, tb), lambda i: (0, i), memory_space=pltpu.SMEM),
                  pl.BlockSpec((tb, SUBLANES, LANES), lambda i: (i, 0, 0)),
                  pl.BlockSpec((tb, ne), lambda i: (i, 0)),
                  pl.BlockSpec(tab3.shape, lambda i: (0, 0, 0), pipeline_mode=pl.Buffered(1))],
        out_specs=pl.BlockSpec((tb, ne), lambda i: (i, 0)),
        out_shape=jax.ShapeDtypeStruct((t, ne), F32),
        scratch_shapes=[pltpu.VMEM((ne * SUBLANES, LANES), BF16)] * PEER_GROUP
                       + [pltpu.VMEM((tb, ne * SUBLANES), F32)],
        compiler_params=pltpu.CompilerParams(dimension_semantics=("arbitrary",),
                                             vmem_limit_bytes=VMEM_LIMIT),
        name="peer_u",
    )(idx_t, h3, gate, tab3)


def _peer_v_kernel(idx_ref, wts_ref, x1_ref, gt_ref, tab_ref, o_ref, *scratch):
    rows, w8_s = scratch[:-1], scratch[-1]
    tb = x1_ref.shape[0]
    ne = wts_ref.shape[1]
    mask = _group_mask(ne)
    gt = gt_ref[0]
    r = lax.broadcasted_iota(I32, (ne, ne * SUBLANES), 0)
    c = lax.broadcasted_iota(I32, (ne, ne * SUBLANES), 1)
    spread = (r == jnp.right_shift(c, 3)).astype(BF16)
    w8_s[...] = sum(jnp.dot(p, spread, preferred_element_type=F32) for p in _split3(wts_ref[...]))

    def combine(t, rows_ref):
        wm = jnp.where(mask, w8_s[pl.ds(t, 1), :], 0.0)
        hi, lo = _split2(wm)
        res = jnp.dot(jnp.concatenate([hi, lo], axis=0), rows_ref[...],
                      preferred_element_type=F32)
        o_ref[t] = x1_ref[t] + gt * (res[:SUBLANES] + res[SUBLANES:])

    def group(i, carry):
        for q, rows_ref in enumerate(rows):
            _gather_rows(idx_ref, tab_ref, len(rows) * i + q, rows_ref)
        for q, rows_ref in enumerate(rows):
            combine(len(rows) * i + q, rows_ref)
        return carry

    lax.fori_loop(0, tb // len(rows), group, 0)


def _peer_v(idx_t, wts, x13, gt3, tab3, seq):
    t = x13.shape[0]
    ne = wts.shape[1]
    tb = PEER_TILE
    tiles_per_seq = seq // tb
    return pl.pallas_call(
        _peer_v_kernel,
        grid=(t // tb,),
        in_specs=[pl.BlockSpec((ne, tb), lambda i: (0, i), memory_space=pltpu.SMEM),
                  pl.BlockSpec((tb, ne), lambda i: (i, 0)),
                  pl.BlockSpec((tb, SUBLANES, LANES), lambda i: (i, 0, 0)),
                  pl.BlockSpec((1, SUBLANES, LANES), lambda i: (i // tiles_per_seq, 0, 0)),
                  pl.BlockSpec(tab3.shape, lambda i: (0, 0, 0), pipeline_mode=pl.Buffered(1))],
        out_specs=pl.BlockSpec((tb, SUBLANES, LANES), lambda i: (i, 0, 0)),
        out_shape=jax.ShapeDtypeStruct((t, SUBLANES, LANES), F32),
        scratch_shapes=[pltpu.VMEM((ne * SUBLANES, LANES), BF16)] * PEER_GROUP
                       + [pltpu.VMEM((tb, ne * SUBLANES), F32)],
        compiler_params=pltpu.CompilerParams(dimension_semantics=("arbitrary",),
                                             vmem_limit_bytes=VMEM_LIMIT),
        name="peer_v",
    )(idx_t, wts, x13, gt3, tab3)


def _perm_in_columns(width, kv_width):
    n_shift = 3 * width + DECAY_LORA + ICLR_LORA + GATE_LORA
    aq = n_shift
    akv = aq + width
    ga = akv + 2 * kv_width
    gb = ga + width
    return n_shift, np.concatenate([np.arange(0, n_shift), np.arange(aq, aq + width),
                                    np.arange(ga, ga + width), np.arange(gb, gb + width),
                                    np.arange(akv, akv + 2 * kv_width)])


def _layer(x, ada, norm1_w, norm2_w, w_in, shift_mu, w0, w_lora_up, a0, a_lora_up, g_lora_up,
           k_k, k_a, r_k, lnx_w, lnx_b, q_norm_w, k_norm_w, sinks, w_out, peer_w_q,
           peer_keys_1, peer_keys_2, peer_u, peer_v):
    bsz, seq, d = x.shape
    t = bsz * seq
    x2 = x.reshape(t, d)
    ada3 = ada.reshape(bsz, N_ADA, d)
    n_shift, perm = _perm_in_columns(d, ATTN_KV_HEADS * ATTN_HEAD)
    w_bf = w_in[:, perm].astype(BF16)
    sh, rest = _inproj(x2, ada3, norm1_w, w_bf, shift_mu, seq, n_shift)
    ya = _rwkv(sh, bsz, seq, d, (w0, w_lora_up, a0, a_lora_up, g_lora_up, k_k, k_a,
                                 r_k.reshape(-1), lnx_w, lnx_b))
    yb = _attn(rest, bsz, seq, d, q_norm_w, k_norm_w, sinks)
    x1, h2, sct = _outproj(ya, yb, rest, x2, ada3, w_out.astype(BF16), norm2_w,
                           peer_w_q.astype(BF16), peer_keys_1, peer_keys_2, seq)
    idx_t, gate = _topk(sct)
    sub = d // LANES
    wts = _peer_u(idx_t, h2.reshape(t, sub, LANES), gate, _table_bf16(peer_u))
    gt2 = ada3[:, 5, :].reshape(bsz, sub, LANES)
    out = _peer_v(idx_t, wts, x1.reshape(t, sub, LANES), gt2, _table_bf16(peer_v), seq)
    return out.reshape(bsz, seq, d)


def kernel(x, c, ada_w, ada_b, norm1_w, norm2_w, w_in, shift_mu, w0, w_lora_up, a0, a_lora_up, g_lora_up, k_k, k_a, r_k, lnx_w, lnx_b, q_norm_w, k_norm_w, sinks, w_out, peer_w_q, peer_keys_1, peer_keys_2, peer_u, peer_v):
    depth = ada_w.shape[0]
    for l in range(depth):
        ada = _ada(c, ada_w[l], ada_b[l])
        x = _layer(x, ada, norm1_w[l], norm2_w[l], w_in[l], shift_mu[l], w0[l], w_lora_up[l],
                   a0[l], a_lora_up[l], g_lora_up[l], k_k[l], k_a[l], r_k[l], lnx_w[l], lnx_b[l],
                   q_norm_w[l], k_norm_w[l], sinks[l], w_out[l], peer_w_q[l], peer_keys_1[l],
                   peer_keys_2[l], peer_u[l], peer_v[l])
    return x
```

```python
import functools

import numpy as np
import jax
import jax.numpy as jnp
from jax import lax
from jax.experimental import pallas as pl
from jax.experimental.pallas import tpu as pltpu
from jax.experimental.pallas import tpu_sc as plsc

F32 = jnp.float32
BF16 = jnp.bfloat16
I32 = jnp.int32
HIGHEST = lax.Precision.HIGHEST

RWKV_HEAD = 64
DECAY_LORA = 64
ICLR_LORA = 64
GATE_LORA = 128
LNX_EPS = 64e-5
ATTN_HEAD = 64
ATTN_Q_HEADS = 16
ATTN_KV_HEADS = 2
ATTN_GROUP = ATTN_Q_HEADS // ATTN_KV_HEADS
WINDOW = 128
BLOCK = 128
ATTN_SCALE = ATTN_HEAD ** -0.5
NEG_INF = -1e30
N_KEYS = 128
PEER_HEADS = 8
PEER_HALF = 128
PEER_TOPK = 16
NORM_EPS = 1e-6
N_ADA = 6

LANES = 128
SUBLANES = 8
VMEM_LIMIT = 56 * 1024 * 1024

RWKV_CHUNK = 64
ROW_TILE = 256
TOPK_TILE = 128
PEER_TILE = 128
PEER_GROUP = 8
SC_BATCHES = 3
SC_CHUNK = 32
SC_LANES = 16
PACK_TILE = 512


def _nt(a, b):
    return lax.dot_general(a.astype(BF16), b.astype(BF16), (((1,), (1,)), ((), ())),
                           preferred_element_type=F32)


def _nn(a, b):
    return jnp.dot(a.astype(BF16), b.astype(BF16), preferred_element_type=F32)


def _dot_f32(a, b):
    return jnp.dot(a, b, precision=HIGHEST, preferred_element_type=F32)


def _bnt(a, b):
    return lax.dot_general(a.astype(BF16), b.astype(BF16), (((2,), (2,)), ((0,), (0,))),
                           preferred_element_type=F32)


def _bnn(a, b):
    return lax.dot_general(a.astype(BF16), b.astype(BF16), (((2,), (1,)), ((0,), (0,))),
                           preferred_element_type=F32)


def _btn(a, b):
    return lax.dot_general(a.astype(BF16), b.astype(BF16), (((1,), (1,)), ((0,), (0,))),
                           preferred_element_type=F32)


def _split2(x):
    hi = x.astype(BF16)
    lo = (x - hi.astype(F32)).astype(BF16)
    return hi, lo


def _split3(x):
    hi = x.astype(BF16)
    r1 = x - hi.astype(F32)
    mid = r1.astype(BF16)
    lo = (r1 - mid.astype(F32)).astype(BF16)
    return hi, mid, lo


def _dot_x3(a, b):
    ah, al = _split2(a)
    bh, bl = _split2(b)
    d = lambda p, q: jnp.dot(p, q, preferred_element_type=F32)
    return d(ah, bh) + (d(ah, bl) + d(al, bh))


def _dot_exact_lhs(a_bf, b):
    return sum(jnp.dot(a_bf, p, preferred_element_type=F32) for p in _split3(b))


def _ada_kernel(c_ref, w_ref, b_ref, o_ref):
    c = c_ref[...]
    cond = c * jax.nn.sigmoid(c)
    o_ref[...] = _dot_f32(cond, w_ref[...]) + b_ref[...]


def _ada(c, ada_w, ada_b):
    bsz, d = c.shape
    n = ada_w.shape[1]
    return pl.pallas_call(
        _ada_kernel,
        grid=(n // d,),
        in_specs=[pl.BlockSpec((bsz, d), lambda j: (0, 0)),
                  pl.BlockSpec((d, d), lambda j: (0, j)),
                  pl.BlockSpec((1, d), lambda j: (0, j))],
        out_specs=pl.BlockSpec((bsz, d), lambda j: (0, j)),
        out_shape=jax.ShapeDtypeStruct((bsz, n), F32),
        name="ada",
    )(c, ada_w, ada_b.reshape(1, n))


def _col_chunks(width, step):
    return [(c0, min(step, width - c0)) for c0 in range(0, width, step)]


def _inproj_kernel(x_ref, ada_ref, nw_ref, w_ref, mu_ref, sh_ref, rest_ref, carry_ref,
                   *, n_shift, n_rest, tiles_per_seq):
    i = pl.program_id(0)
    x = x_ref[...]
    tm = x.shape[0]
    ms = jnp.mean(x * x, axis=-1, keepdims=True)
    ada = ada_ref[0]
    h = x * lax.rsqrt(ms + NORM_EPS) * nw_ref[...] * (1.0 + ada[1:2]) + ada[0:1]
    hb = h.astype(BF16)
    row0 = lax.broadcasted_iota(I32, (tm, 1), 0) == 0
    seq_start = (i % tiles_per_seq) == 0
    for c0, cw in _col_chunks(n_shift, 1024):
        p = jnp.dot(hb, w_ref[:, c0:c0 + cw], preferred_element_type=F32)
        prev_last = jnp.where(seq_start, 0.0, carry_ref[:, c0:c0 + cw])
        carry_ref[:, c0:c0 + cw] = p[tm - 1:tm, :]
        prev = jnp.where(row0, prev_last, pltpu.roll(p, 1, axis=0))
        sh_ref[:, c0:c0 + cw] = p + (prev - p) * mu_ref[:, c0:c0 + cw]
    for c0, cw in _col_chunks(n_rest, 1024):
        rest_ref[:, c0:c0 + cw] = jnp.dot(hb, w_ref[:, n_shift + c0:n_shift + c0 + cw],
                                          preferred_element_type=F32)


def _inproj(x2, ada3, norm_w, w_bf, mu, seq, n_shift):
    t, d = x2.shape
    n_rest = w_bf.shape[1] - n_shift
    tm = ROW_TILE
    tiles_per_seq = seq // tm
    kern = functools.partial(_inproj_kernel, n_shift=n_shift, n_rest=n_rest,
                             tiles_per_seq=tiles_per_seq)
    return pl.pallas_call(
        kern,
        grid=(t // tm,),
        in_specs=[pl.BlockSpec((tm, d), lambda i: (i, 0)),
                  pl.BlockSpec((1, N_ADA, d), lambda i: (i // tiles_per_seq, 0, 0)),
                  pl.BlockSpec((1, d), lambda i: (0, 0)),
                  pl.BlockSpec(w_bf.shape, lambda i: (0, 0), pipeline_mode=pl.Buffered(1)),
                  pl.BlockSpec((1, n_shift), lambda i: (0, 0))],
        out_specs=[pl.BlockSpec((tm, n_shift), lambda i: (i, 0)),
                   pl.BlockSpec((tm, n_rest), lambda i: (i, 0))],
        out_shape=[jax.ShapeDtypeStruct((t, n_shift), F32),
                   jax.ShapeDtypeStruct((t, n_rest), F32)],
        scratch_shapes=[pltpu.VMEM((1, n_shift), F32)],
        compiler_params=pltpu.CompilerParams(dimension_semantics=("arbitrary",),
                                             vmem_limit_bytes=VMEM_LIMIT),
        name="inproj",
    )(x2, ada3, norm_w.reshape(1, d), w_bf, mu.reshape(1, n_shift))


def _softplus(z):
    return jnp.maximum(z, 0.0) + jnp.log1p(jnp.exp(-jnp.abs(z)))


def _rwkv_heads(r, k, v, kkraw, a, cum, lw, s0, masks):
    strict, incl, diag, levels = masks
    c = r.shape[1]
    nrm = jnp.sqrt(jnp.sum(kkraw * kkraw, axis=-1, keepdims=True))
    kk = kkraw / jnp.maximum(nrm, 1e-12)
    w_in = jnp.exp(cum)
    w_ex = jnp.exp(cum - lw)
    w_inv = jnp.exp(-cum)
    w_end = jnp.exp(cum[:, c - 1:c, :] - cum)
    bb = kk * a
    rt = r * w_in
    at = -kk * w_ex
    bt = bb * w_inv
    kt = k * w_inv
    ab = _bnt(at, bt)
    ak = _bnt(at, kt)
    rb = _bnt(rt, bt)
    rk = _bnt(rt, kt)
    lab = jnp.where(strict, ab, 0.0)
    inv = jnp.where(levels[0], lab, 0.0) + jnp.where(diag, 1.0, 0.0)
    for lvl in levels[1:]:
        inv = inv + _bnn(_bnn(inv, jnp.where(lvl, lab, 0.0)), inv)
    rhs = _bnt(at, s0) + _bnn(jnp.where(strict, ak, 0.0), v)
    u = _bnn(inv, rhs)
    y = _bnt(rt, s0) + _bnn(jnp.where(incl, rb, 0.0), u) + _bnn(jnp.where(incl, rk, 0.0), v)
    s_new = s0 * w_in[:, c - 1:c, :] + _btn(u, bb * w_end) + _btn(v, k * w_end)
    return y, s_new


def _rwkv_kernel(pr_ref, pk_ref, pv_ref, tail_ref, w0_ref, wup_ref, a0_ref, aup_ref, gup_ref,
                 kk_ref, ka_ref, rk_ref, lnw_ref, lnb_ref, o_ref, state_ref):
    j = pl.program_id(1)
    c = pr_ref.shape[0]
    n_heads = pr_ref.shape[1] // RWKV_HEAD

    @pl.when(j == 0)
    def _():
        state_ref[...] = jnp.zeros_like(state_ref)

    tail = tail_ref[...]
    pw = tail[:, :DECAY_LORA]
    pa = tail[:, DECAY_LORA:DECAY_LORA + ICLR_LORA]
    pg = tail[:, DECAY_LORA + ICLR_LORA:]
    pk = pk_ref[...]
    w_log = -_softplus(-(w0_ref[...] + _dot_x3(jnp.tanh(pw), wup_ref[...]))) - 0.5
    lw = -jnp.exp(w_log)
    a = jax.nn.sigmoid(a0_ref[...] + _dot_x3(pa, aup_ref[...]))
    row = lax.broadcasted_iota(I32, (c, c), 0)
    col = lax.broadcasted_iota(I32, (c, c), 1)
    strict = row > col
    incl = row >= col
    diag = row == col
    cum = _dot_exact_lhs(incl.astype(BF16), lw)
    k = pk * (1.0 + (a - 1.0) * ka_ref[...])
    kkraw = pk * kk_ref[...]
    g = _dot_x3(jax.nn.sigmoid(pg), gup_ref[...])

    levels = []
    shift = 0
    while (1 << shift) < c:
        levels.append((jnp.right_shift(row, shift + 1) == jnp.right_shift(col, shift + 1))
                      & ((jnp.right_shift(row, shift) & 1) == 1)
                      & ((jnp.right_shift(col, shift) & 1) == 0))
        shift += 1
    masks = (strict, incl, diag, levels)

    def heads(x):
        return jnp.stack([x[:, h * RWKV_HEAD:(h + 1) * RWKV_HEAD] for h in range(n_heads)], axis=0)

    r3, k3, v3 = heads(pr_ref[...]), heads(k), heads(pv_ref[...])
    y, s_new = _rwkv_heads(r3, k3, v3, heads(kkraw), heads(a), heads(cum), heads(lw),
                           state_ref[...], masks)
    state_ref[...] = s_new
    mu = jnp.mean(y, axis=-1, keepdims=True)
    var = jnp.mean(jnp.square(y - mu), axis=-1, keepdims=True)
    yn = (y - mu) * lax.rsqrt(var + LNX_EPS) * lnw_ref[...] + lnb_ref[...]
    bonus = jnp.sum(r3 * k3 * rk_ref[...], axis=-1, keepdims=True) * v3
    out = (yn + bonus) * heads(g)
    for h in range(n_heads):
        o_ref[:, h * RWKV_HEAD:(h + 1) * RWKV_HEAD] = out[h]


def _rwkv(sh, bsz, seq, width, params):
    (w0, w_up, a0, a_up, g_up, k_k, k_a, r_k, lnx_w, lnx_b) = params
    c = RWKV_CHUNK
    nc = seq // c
    heads = width // RWKV_HEAD
    tail_w = DECAY_LORA + ICLR_LORA + GATE_LORA
    tail_blk = (3 * width) // tail_w
    row = lambda b, j: b * nc + j
    vec = lambda a: a.reshape(1, width)
    per_head = lambda a: a.reshape(heads, 1, RWKV_HEAD)
    const = lambda shape: pl.BlockSpec(shape, lambda b, j: (0,) * len(shape))
    return pl.pallas_call(
        _rwkv_kernel,
        grid=(bsz, nc),
        in_specs=[pl.BlockSpec((c, width), lambda b, j: (row(b, j), 0)),
                  pl.BlockSpec((c, width), lambda b, j: (row(b, j), 1)),
                  pl.BlockSpec((c, width), lambda b, j: (row(b, j), 2)),
                  pl.BlockSpec((c, tail_w), lambda b, j: (row(b, j), tail_blk)),
                  const((1, width)), const((DECAY_LORA, width)),
                  const((1, width)), const((ICLR_LORA, width)), const((GATE_LORA, width)),
                  const((1, width)), const((1, width)),
                  const((heads, 1, RWKV_HEAD)), const((heads, 1, RWKV_HEAD)),
                  const((heads, 1, RWKV_HEAD))],
        out_specs=pl.BlockSpec((c, width), lambda b, j: (row(b, j), 0)),
        out_shape=jax.ShapeDtypeStruct((bsz * seq, width), F32),
        scratch_shapes=[pltpu.VMEM((heads, RWKV_HEAD, RWKV_HEAD), F32)],
        compiler_params=pltpu.CompilerParams(dimension_semantics=("arbitrary", "arbitrary"),
                                             vmem_limit_bytes=VMEM_LIMIT),
        name="rwkv",
    )(sh, sh, sh, sh, vec(w0), w_up, vec(a0), a_up, g_up, vec(k_k), vec(k_a), per_head(r_k),
      per_head(lnx_w), per_head(lnx_b))


def _head_rms(x, w):
    return x * lax.rsqrt(jnp.mean(x * x, axis=-1, keepdims=True) + NORM_EPS) * w


def _attn_kernel(sink_ref, q_ref, kvc_ref, kvp_ref, qw_ref, kw_ref, o_ref):
    blk = pl.program_id(1)
    q = q_ref[...]
    kvc = kvc_ref[...]
    kvp = kvp_ref[...]
    kv_w = ATTN_KV_HEADS * ATTN_HEAD
    qi = lax.broadcasted_iota(I32, (BLOCK, 2 * BLOCK), 0)
    kj = lax.broadcasted_iota(I32, (BLOCK, 2 * BLOCK), 1)
    rel = kj - qi
    valid = (rel >= BLOCK - WINDOW + 1) & (rel <= BLOCK) & (blk * BLOCK - BLOCK + kj >= 0)
    for g in range(ATTN_KV_HEADS):
        ks = slice(g * ATTN_HEAD, (g + 1) * ATTN_HEAD)
        vs = slice(kv_w + g * ATTN_HEAD, kv_w + (g + 1) * ATTN_HEAD)
        kband = _head_rms(jnp.concatenate([kvp[:, ks], kvc[:, ks]], axis=0), kw_ref[...])
        vband = jnp.concatenate([kvp[:, vs], kvc[:, vs]], axis=0)
        for n in range(ATTN_GROUP):
            hq = g * ATTN_GROUP + n
            qh = _head_rms(q[:, hq * ATTN_HEAD:(hq + 1) * ATTN_HEAD], qw_ref[...])
            s = jnp.where(valid, _nt(qh, kband) * ATTN_SCALE, NEG_INF)
            sink = sink_ref[hq]
            m = jnp.maximum(jnp.max(s, axis=-1, keepdims=True), sink)
            p = jnp.exp(s - m)
            denom = jnp.sum(p, axis=-1, keepdims=True) + jnp.exp(sink - m)
            o_ref[:, hq * ATTN_HEAD:(hq + 1) * ATTN_HEAD] = _nn(p, vband) / denom


def _attn(rest, bsz, seq, width, q_norm_w, k_norm_w, sinks):
    nb = seq // BLOCK
    kv_w = 2 * ATTN_KV_HEADS * ATTN_HEAD
    kv_blk = (3 * width) // kv_w
    return pl.pallas_call(
        _attn_kernel,
        grid=(bsz, nb),
        in_specs=[pl.BlockSpec(memory_space=pltpu.SMEM),
                  pl.BlockSpec((BLOCK, width), lambda b, j: (b * nb + j, 0)),
                  pl.BlockSpec((BLOCK, kv_w), lambda b, j: (b * nb + j, kv_blk)),
                  pl.BlockSpec((BLOCK, kv_w), lambda b, j: (b * nb + jnp.maximum(j - 1, 0), kv_blk)),
                  pl.BlockSpec((1, ATTN_HEAD), lambda b, j: (0, 0)),
                  pl.BlockSpec((1, ATTN_HEAD), lambda b, j: (0, 0))],
        out_specs=pl.BlockSpec((BLOCK, width), lambda b, j: (b * nb + j, 0)),
        out_shape=jax.ShapeDtypeStruct((bsz * seq, width), F32),
        compiler_params=pltpu.CompilerParams(dimension_semantics=("arbitrary", "arbitrary"),
                                             vmem_limit_bytes=VMEM_LIMIT),
        name="attn",
    )(sinks, rest, rest, rest, q_norm_w.reshape(1, ATTN_HEAD), k_norm_w.reshape(1, ATTN_HEAD))


def _outproj_kernel(ya_ref, yb_ref, ga_ref, gb_ref, x_ref, ada_ref, wo_ref, nw_ref, wq_ref,
                    k1_ref, k2_ref, x1_ref, h2_ref, sct_ref):
    ada = ada_ref[0]
    mixed = jax.nn.sigmoid(ga_ref[...]) * ya_ref[...] + jax.nn.sigmoid(gb_ref[...]) * yb_ref[...]
    x1 = x_ref[...] + ada[2:3] * _nn(mixed, wo_ref[...])
    x1_ref[...] = x1
    ms = jnp.mean(x1 * x1, axis=-1, keepdims=True)
    h2 = x1 * lax.rsqrt(ms + NORM_EPS) * nw_ref[...] * (1.0 + ada[4:5]) + ada[3:4]
    h2_ref[...] = h2
    q = _nn(h2, wq_ref[...])
    for ch in range(q.shape[1] // PEER_HALF):
        keys = k1_ref if ch % 2 == 0 else k2_ref
        cs = slice(ch * PEER_HALF, (ch + 1) * PEER_HALF)
        sct_ref[cs, :] = _nt(keys[...], q[:, cs])


def _outproj(ya, yb, rest, x2, ada3, w_out_bf, norm_w, wq_bf, keys_1, keys_2, seq):
    t, d = x2.shape
    tm = ROW_TILE
    tiles_per_seq = seq // tm
    qd = wq_bf.shape[1]
    rowblk = lambda c: pl.BlockSpec((tm, d), lambda i: (i, c))
    const = lambda shape: pl.BlockSpec(shape, lambda i: (0, 0))
    return pl.pallas_call(
        _outproj_kernel,
        grid=(t // tm,),
        in_specs=[rowblk(0), rowblk(0), rowblk(1), rowblk(2), rowblk(0),
                  pl.BlockSpec((1, N_ADA, d), lambda i: (i // tiles_per_seq, 0, 0)),
                  const((d, d)), const((1, d)), const((d, qd)),
                  const(keys_1.shape), const(keys_2.shape)],
        out_specs=[rowblk(0), rowblk(0), pl.BlockSpec((qd, tm), lambda i: (0, i))],
        out_shape=[jax.ShapeDtypeStruct((t, d), F32), jax.ShapeDtypeStruct((t, d), F32),
                   jax.ShapeDtypeStruct((qd, t), F32)],
        compiler_params=pltpu.CompilerParams(dimension_semantics=("arbitrary",),
                                             vmem_limit_bytes=VMEM_LIMIT),
        name="outproj",
    )(ya, yb, rest, rest, x2, ada3, w_out_bf, norm_w.reshape(1, d), wq_bf, keys_1, keys_2)


def _top16_rows(s, payload=None):
    n = s.shape[0]
    iota = lax.broadcasted_iota(I32, s.shape, 0).astype(F32)
    vals, picks = [], []
    for _ in range(PEER_TOPK):
        m = jnp.max(s, axis=0, keepdims=True)
        i = jnp.min(jnp.where(s == m, iota, float(n)), axis=0, keepdims=True)
        hit = iota == i
        s = jnp.where(hit, -jnp.inf, s)
        vals.append(m)
        picks.append(i if payload is None else
                     jnp.max(jnp.where(hit, payload, -1.0), axis=0, keepdims=True))
    return jnp.concatenate(vals, axis=0), jnp.concatenate(picks, axis=0)


def _topk_kernel(sct_ref, idx_ref, gate_ref):
    keep = [PEER_TOPK // (i + 1) for i in range(PEER_TOPK)]
    pad = (-sum(keep)) % SUBLANES
    idx_all, gate_all = [], []
    for h in range(PEER_HEADS):
        base = h * 2 * N_KEYS
        v1, i1 = _top16_rows(sct_ref[base:base + N_KEYS, :])
        v2, i2 = _top16_rows(sct_ref[base + N_KEYS:base + 2 * N_KEYS, :])
        cand = [v1[i:i + 1, :] + v2[0:keep[i], :] for i in range(PEER_TOPK)]
        cidx = [i1[i:i + 1, :] * float(N_KEYS) + i2[0:keep[i], :] for i in range(PEER_TOPK)]
        if pad:
            cand.append(jnp.full((pad, v1.shape[1]), -jnp.inf, F32))
            cidx.append(jnp.zeros((pad, v1.shape[1]), F32))
        sc, idx = _top16_rows(jnp.concatenate(cand, axis=0), jnp.concatenate(cidx, axis=0))
        idx_all.append(idx.astype(I32))
        ex = jnp.exp(sc - sc[0:1, :])
        gate_all.append(ex / jnp.sum(ex, axis=0, keepdims=True))
    idx_ref[...] = jnp.concatenate(idx_all, axis=0)
    gate_ref[...] = jnp.concatenate(gate_all, axis=0).T


def _topk(sct):
    qd, t = sct.shape
    tk = TOPK_TILE
    ne = PEER_HEADS * PEER_TOPK
    return pl.pallas_call(
        _topk_kernel,
        grid=(t // tk,),
        in_specs=[pl.BlockSpec((qd, tk), lambda i: (0, i))],
        out_specs=[pl.BlockSpec((ne, tk), lambda i: (0, i)),
                   pl.BlockSpec((tk, ne), lambda i: (i, 0))],
        out_shape=[jax.ShapeDtypeStruct((ne, t), I32), jax.ShapeDtypeStruct((t, ne), F32)],
        compiler_params=pltpu.CompilerParams(dimension_semantics=("arbitrary",),
                                             vmem_limit_bytes=VMEM_LIMIT),
        name="topk",
    )(sct)


def _cast_kernel(x_ref, o_ref):
    o_ref[...] = x_ref[...].astype(BF16)


def _table_bf16(tab):
    n, d = tab.shape
    out = pl.pallas_call(
        _cast_kernel,
        grid=(n // PACK_TILE,),
        in_specs=[pl.BlockSpec((PACK_TILE, d), lambda i: (i, 0))],
        out_specs=pl.BlockSpec((PACK_TILE, d), lambda i: (i, 0)),
        out_shape=jax.ShapeDtypeStruct((n, d), BF16),
        name="table_cast",
    )(tab)
    return out.reshape(n, d // LANES, LANES)


def _gelu_exact(x):
    return 0.5 * x * (1.0 + lax.erf(x * (2.0 ** -0.5)))


def _group_mask(ne):
    sub = lax.broadcasted_iota(I32, (SUBLANES, ne * SUBLANES), 0)
    col = lax.broadcasted_iota(I32, (SUBLANES, ne * SUBLANES), 1)
    return sub == (col & (SUBLANES - 1))


def _gather_rows(idx_ref, tab_ref, t, rows_ref):
    picks = idx_ref.at[:, pl.ds(t, 1)]
    for e in range(idx_ref.shape[0]):
        rows_ref[e * SUBLANES:(e + 1) * SUBLANES, :] = tab_ref[picks[e, 0]]


def _peer_u_kernel(idx_ref, x_ref, gate_ref, tab_ref, wts_ref, *scratch):
    rows, d_s = scratch[:-1], scratch[-1]
    tb = x_ref.shape[0]
    ne = gate_ref.shape[1]
    mask = _group_mask(ne)

    def reduce(t, rows_ref):
        hi, lo = _split2(x_ref[t])
        lhs = jnp.concatenate([hi, lo], axis=0)
        z = lax.dot_general(lhs, rows_ref[...], (((1,), (1,)), ((), ())),
                            preferred_element_type=F32)
        zz = z[:SUBLANES] + z[SUBLANES:]
        d_s[pl.ds(t, 1), :] = jnp.sum(jnp.where(mask, zz, 0.0), axis=0, keepdims=True)

    def group(i, carry):
        for q, rows_ref in enumerate(rows):
            _gather_rows(idx_ref, tab_ref, len(rows) * i + q, rows_ref)
        for q, rows_ref in enumerate(rows):
            reduce(len(rows) * i + q, rows_ref)
        return carry

    lax.fori_loop(0, tb // len(rows), group, 0)
    r = lax.broadcasted_iota(I32, (ne * SUBLANES, ne), 0)
    c = lax.broadcasted_iota(I32, (ne * SUBLANES, ne), 1)
    fold = (jnp.right_shift(r, 3) == c).astype(BF16)
    dots = sum(jnp.dot(p, fold, preferred_element_type=F32) for p in _split3(d_s[...]))
    wts_ref[...] = gate_ref[...] * _gelu_exact(dots)


def _peer_u(idx_t, h3, gate, tab3):
    t = h3.shape[0]
    ne = gate.shape[1]
    tb = PEER_TILE
    return pl.pallas_call(
        _peer_u_kernel,
        grid=(t // tb,),
        in_specs=[pl.BlockSpec((ne, tb), lambda i: (0, i), memory_space=pltpu.SMEM),
                  pl.BlockSpec((tb, SUBLANES, LANES), lambda i: (i, 0, 0)),
                  pl.BlockSpec((tb, ne), lambda i: (i, 0)),
                  pl.BlockSpec(tab3.shape, lambda i: (0, 0, 0), pipeline_mode=pl.Buffered(1))],
        out_specs=pl.BlockSpec((tb, ne), lambda i: (i, 0)),
        out_shape=jax.ShapeDtypeStruct((t, ne), F32),
        scratch_shapes=[pltpu.VMEM((ne * SUBLANES, LANES), BF16)] * PEER_GROUP
                       + [pltpu.VMEM((tb, ne * SUBLANES), F32)],
        compiler_params=pltpu.CompilerParams(dimension_semantics=("arbitrary",),
                                             vmem_limit_bytes=VMEM_LIMIT),
        name="peer_u",
    )(idx_t, h3, gate, tab3)


def _peer_v_kernel(idx_ref, wts_ref, x1_ref, gt_ref, tab_ref, o_ref, *scratch):
    rows, w8_s = scratch[:-1], scratch[-1]
    tb = x1_ref.shape[0]
    ne = wts_ref.shape[1]
    mask = _group_mask(ne)
    gt = gt_ref[0]
    r = lax.broadcasted_iota(I32, (ne, ne * SUBLANES), 0)
    c = lax.broadcasted_iota(I32, (ne, ne * SUBLANES), 1)
    spread = (r == jnp.right_shift(c, 3)).astype(BF16)
    w8_s[...] = sum(jnp.dot(p, spread, preferred_element_type=F32) for p in _split3(wts_ref[...]))

    def combine(t, rows_ref):
        wm = jnp.where(mask, w8_s[pl.ds(t, 1), :], 0.0)
        hi, lo = _split2(wm)
        res = jnp.dot(jnp.concatenate([hi, lo], axis=0), rows_ref[...],
                      preferred_element_type=F32)
        o_ref[t] = x1_ref[t] + gt * (res[:SUBLANES] + res[SUBLANES:])

    def group(i, carry):
        for q, rows_ref in enumerate(rows):
            _gather_rows(idx_ref, tab_ref, len(rows) * i + q, rows_ref)
        for q, rows_ref in enumerate(rows):
            combine(len(rows) * i + q, rows_ref)
        return carry

    lax.fori_loop(0, tb // len(rows), group, 0)


def _peer_v(idx_t, wts, x13, gt3, tab3, seq):
    t = x13.shape[0]
    ne = wts.shape[1]
    tb = PEER_TILE
    tiles_per_seq = seq // tb
    return pl.pallas_call(
        _peer_v_kernel,
        grid=(t // tb,),
        in_specs=[pl.BlockSpec((ne, tb), lambda i: (0, i), memory_space=pltpu.SMEM),
                  pl.BlockSpec((tb, ne), lambda i: (i, 0)),
                  pl.BlockSpec((tb, SUBLANES, LANES), lambda i: (i, 0, 0)),
                  pl.BlockSpec((1, SUBLANES, LANES), lambda i: (i // tiles_per_seq, 0, 0)),
                  pl.BlockSpec(tab3.shape, lambda i: (0, 0, 0), pipeline_mode=pl.Buffered(1))],
        out_specs=pl.BlockSpec((tb, SUBLANES, LANES), lambda i: (i, 0, 0)),
        out_shape=jax.ShapeDtypeStruct((t, SUBLANES, LANES), F32),
        scratch_shapes=[pltpu.VMEM((ne * SUBLANES, LANES), BF16)] * PEER_GROUP
                       + [pltpu.VMEM((tb, ne * SUBLANES), F32)],
        compiler_params=pltpu.CompilerParams(dimension_semantics=("arbitrary",),
                                             vmem_limit_bytes=VMEM_LIMIT),
        name="peer_v",
    )(idx_t, wts, x13, gt3, tab3)


def _w16_kernel(w_ref, o_ref):
    ne = w_ref.shape[1]
    r = lax.broadcasted_iota(I32, (ne, ne * SC_LANES), 0)
    c = lax.broadcasted_iota(I32, (ne, ne * SC_LANES), 1)
    spread = (r == jnp.right_shift(c, 4)).astype(BF16)
    o_ref[...] = sum(jnp.dot(p, spread, preferred_element_type=F32) for p in _split3(w_ref[...]))


def _w16(wts):
    t, ne = wts.shape
    tb = ROW_TILE
    return pl.pallas_call(
        _w16_kernel,
        grid=(t // tb,),
        in_specs=[pl.BlockSpec((tb, ne), lambda i: (i, 0))],
        out_specs=pl.BlockSpec((tb, ne * SC_LANES), lambda i: (i, 0)),
        out_shape=jax.ShapeDtypeStruct((t, ne * SC_LANES), F32),
        name="w16",
    )(wts)


def _peer_v_sc(idx_sc, w16, x1_sc, gt2, vtab, seq, tok_base):
    tsc, ne = idx_sc.shape
    d = vtab.shape[1]
    info = plsc.get_sparse_core_info()
    nw = info.num_cores * info.num_subcores
    tps = tsc // nw
    nblk = d // SC_LANES
    shift = seq.bit_length() - 1
    assert (1 << shift) == seq and tsc % nw == 0
    mesh = plsc.VectorSubcoreMesh(core_axis_name="c", subcore_axis_name="s")

    @functools.partial(
        pl.kernel, mesh=mesh, out_type=jax.ShapeDtypeStruct((tsc, d), F32),
        scratch_types=[pltpu.VMEM((ne,), I32), pltpu.VMEM((ne * SC_LANES,), F32),
                       pltpu.VMEM((SC_CHUNK, d), F32), pltpu.VMEM((d,), F32), pltpu.VMEM((d,), F32),
                       pltpu.VMEM(gt2.shape, F32), pltpu.SemaphoreType.DMA],
        name="peer_v_sc")
    def k(idx_hbm, w_hbm, x_hbm, gt_hbm, v_hbm, o_hbm, idx_v, w_v, rows_v, acc_v, x_v, gt_v, sem):
        wid = lax.axis_index("c") * info.num_subcores + lax.axis_index("s")
        pltpu.sync_copy(gt_hbm, gt_v)

        @pl.loop(0, tps)
        def _(t):
            tok = wid * tps + t
            pltpu.sync_copy(idx_hbm.at[tok], idx_v)
            pltpu.sync_copy(w_hbm.at[tok], w_v)
            pltpu.sync_copy(x_hbm.at[tok], x_v)
            zero = jnp.zeros((SC_LANES,), F32)
            for j in range(nblk):
                acc_v[pl.ds(j * SC_LANES, SC_LANES)] = zero

            @pl.loop(0, ne // SC_CHUNK)
            def _(ch):
                pltpu.async_copy(v_hbm.at[idx_v.at[pl.ds(ch * SC_CHUNK, SC_CHUNK)]], rows_v, sem).wait()

                @pl.loop(0, SC_CHUNK)
                def _(e):
                    wv = w_v[pl.ds((ch * SC_CHUNK + e) * SC_LANES, SC_LANES)]

                    @plsc.parallel_loop(0, d, SC_LANES, unroll=8)
                    def _(c):
                        sl = pl.ds(pl.multiple_of(c, SC_LANES), SC_LANES)
                        plsc.addupdate(acc_v.at[sl], wv * rows_v[e, sl])

            b = jnp.right_shift(tok_base + tok, shift)
            for j in range(nblk):
                sl = pl.ds(j * SC_LANES, SC_LANES)
                acc_v[sl] = x_v[sl] + gt_v[b, sl] * acc_v[sl]
            pltpu.sync_copy(acc_v, o_hbm.at[tok])

    return k(idx_sc, w16, x1_sc, gt2, vtab)


def _perm_in_columns(width, kv_width):
    n_shift = 3 * width + DECAY_LORA + ICLR_LORA + GATE_LORA
    aq = n_shift
    akv = aq + width
    ga = akv + 2 * kv_width
    gb = ga + width
    return n_shift, np.concatenate([np.arange(0, n_shift), np.arange(aq, aq + width),
                                    np.arange(ga, ga + width), np.arange(gb, gb + width),
                                    np.arange(akv, akv + 2 * kv_width)])


def _layer(x, ada, norm1_w, norm2_w, w_in, shift_mu, w0, w_lora_up, a0, a_lora_up, g_lora_up,
           k_k, k_a, r_k, lnx_w, lnx_b, q_norm_w, k_norm_w, sinks, w_out, peer_w_q,
           peer_keys_1, peer_keys_2, peer_u, peer_v):
    bsz, seq, d = x.shape
    t = bsz * seq
    x2 = x.reshape(t, d)
    ada3 = ada.reshape(bsz, N_ADA, d)
    n_shift, perm = _perm_in_columns(d, ATTN_KV_HEADS * ATTN_HEAD)
    w_bf = w_in[:, perm].astype(BF16)
    sh, rest = _inproj(x2, ada3, norm1_w, w_bf, shift_mu, seq, n_shift)
    ya = _rwkv(sh, bsz, seq, d, (w0, w_lora_up, a0, a_lora_up, g_lora_up, k_k, k_a,
                                 r_k.reshape(-1), lnx_w, lnx_b))
    yb = _attn(rest, bsz, seq, d, q_norm_w, k_norm_w, sinks)
    x1, h2, sct = _outproj(ya, yb, rest, x2, ada3, w_out.astype(BF16), norm2_w,
                           peer_w_q.astype(BF16), peer_keys_1, peer_keys_2, seq)
    idx_t, gate = _topk(sct)
    sub = d // LANES
    wts = _peer_u(idx_t, h2.reshape(t, sub, LANES), gate, _table_bf16(peer_u))
    gt2 = ada3[:, 5, :]
    t_tc = (bsz - SC_BATCHES) * seq
    out_tc = _peer_v(idx_t[:, :t_tc], wts[:t_tc], x1[:t_tc].reshape(t_tc, sub, LANES),
                     gt2.reshape(bsz, sub, LANES), _table_bf16(peer_v), seq)
    out_sc = _peer_v_sc(idx_t[:, t_tc:].T, _w16(wts[t_tc:]), x1[t_tc:], gt2, peer_v, seq, t_tc)
    return jnp.concatenate([out_tc.reshape(t_tc, d), out_sc], axis=0).reshape(bsz, seq, d)


def kernel(x, c, ada_w, ada_b, norm1_w, norm2_w, w_in, shift_mu, w0, w_lora_up, a0, a_lora_up, g_lora_up, k_k, k_a, r_k, lnx_w, lnx_b, q_norm_w, k_norm_w, sinks, w_out, peer_w_q, peer_keys_1, peer_keys_2, peer_u, peer_v):
    depth = ada_w.shape[0]
    for l in range(depth):
        ada = _ada(c, ada_w[l], ada_b[l])
        x = _layer(x, ada, norm1_w[l], norm2_w[l], w_in[l], shift_mu[l], w0[l], w_lora_up[l],
                   a0[l], a_lora_up[l], g_lora_up[l], k_k[l], k_a[l], r_k[l], lnx_w[l], lnx_b[l],
                   q_norm_w[l], k_norm_w[l], sinks[l], w_out[l], peer_w_q[l], peer_keys_1[l],
                   peer_keys_2[l], peer_u[l], peer_v[l])
    return x
```

```python
import functools

import numpy as np
import jax
import jax.numpy as jnp
from jax import lax
from jax.experimental import pallas as pl
from jax.experimental.pallas import tpu as pltpu
from jax.experimental.pallas import tpu_sc as plsc

F32 = jnp.float32
BF16 = jnp.bfloat16
I32 = jnp.int32
HIGHEST = lax.Precision.HIGHEST

RWKV_HEAD = 64
DECAY_LORA = 64
ICLR_LORA = 64
GATE_LORA = 128
LNX_EPS = 64e-5
ATTN_HEAD = 64
ATTN_Q_HEADS = 16
ATTN_KV_HEADS = 2
ATTN_GROUP = ATTN_Q_HEADS // ATTN_KV_HEADS
WINDOW = 128
BLOCK = 128
ATTN_SCALE = ATTN_HEAD ** -0.5
NEG_INF = -1e30
N_KEYS = 128
PEER_HEADS = 8
PEER_HALF = 128
PEER_TOPK = 16
NORM_EPS = 1e-6
N_ADA = 6

LANES = 128
SUBLANES = 8
VMEM_LIMIT = 56 * 1024 * 1024

RWKV_CHUNK = 64
ROW_TILE = 256
TOPK_TILE = 128
PEER_TILE = 128
PEER_GROUP = 8
SC_TOKENS = 24576
SC_CHUNK_U = 32
SC_CHUNK_V = 16
SC_LANES = 16
PACK_TILE = 512


def _nt(a, b):
    return lax.dot_general(a.astype(BF16), b.astype(BF16), (((1,), (1,)), ((), ())),
                           preferred_element_type=F32)


def _nn(a, b):
    return jnp.dot(a.astype(BF16), b.astype(BF16), preferred_element_type=F32)


def _dot_f32(a, b):
    return jnp.dot(a, b, precision=HIGHEST, preferred_element_type=F32)


def _bnt(a, b):
    return lax.dot_general(a.astype(BF16), b.astype(BF16), (((2,), (2,)), ((0,), (0,))),
                           preferred_element_type=F32)


def _bnn(a, b):
    return lax.dot_general(a.astype(BF16), b.astype(BF16), (((2,), (1,)), ((0,), (0,))),
                           preferred_element_type=F32)


def _btn(a, b):
    return lax.dot_general(a.astype(BF16), b.astype(BF16), (((1,), (1,)), ((0,), (0,))),
                           preferred_element_type=F32)


def _split2(x):
    hi = x.astype(BF16)
    lo = (x - hi.astype(F32)).astype(BF16)
    return hi, lo


def _split3(x):
    hi = x.astype(BF16)
    r1 = x - hi.astype(F32)
    mid = r1.astype(BF16)
    lo = (r1 - mid.astype(F32)).astype(BF16)
    return hi, mid, lo


def _dot_x3(a, b):
    ah, al = _split2(a)
    bh, bl = _split2(b)
    d = lambda p, q: jnp.dot(p, q, preferred_element_type=F32)
    return d(ah, bh) + (d(ah, bl) + d(al, bh))


def _dot_exact_lhs(a_bf, b):
    return sum(jnp.dot(a_bf, p, preferred_element_type=F32) for p in _split3(b))


def _ada_kernel(c_ref, w_ref, b_ref, o_ref):
    c = c_ref[...]
    cond = c * jax.nn.sigmoid(c)
    o_ref[...] = _dot_f32(cond, w_ref[...]) + b_ref[...]


def _ada(c, ada_w, ada_b):
    bsz, d = c.shape
    n = ada_w.shape[1]
    return pl.pallas_call(
        _ada_kernel,
        grid=(n // d,),
        in_specs=[pl.BlockSpec((bsz, d), lambda j: (0, 0)),
                  pl.BlockSpec((d, d), lambda j: (0, j)),
                  pl.BlockSpec((1, d), lambda j: (0, j))],
        out_specs=pl.BlockSpec((bsz, d), lambda j: (0, j)),
        out_shape=jax.ShapeDtypeStruct((bsz, n), F32),
        name="ada",
    )(c, ada_w, ada_b.reshape(1, n))


def _col_chunks(width, step):
    return [(c0, min(step, width - c0)) for c0 in range(0, width, step)]


def _inproj_kernel(x_ref, ada_ref, nw_ref, w_ref, mu_ref, sh_ref, rest_ref, carry_ref,
                   *, n_shift, n_rest, tiles_per_seq):
    i = pl.program_id(0)
    x = x_ref[...]
    tm = x.shape[0]
    ms = jnp.mean(x * x, axis=-1, keepdims=True)
    ada = ada_ref[0]
    h = x * lax.rsqrt(ms + NORM_EPS) * nw_ref[...] * (1.0 + ada[1:2]) + ada[0:1]
    hb = h.astype(BF16)
    row0 = lax.broadcasted_iota(I32, (tm, 1), 0) == 0
    seq_start = (i % tiles_per_seq) == 0
    for c0, cw in _col_chunks(n_shift, 1024):
        p = jnp.dot(hb, w_ref[:, c0:c0 + cw], preferred_element_type=F32)
        prev_last = jnp.where(seq_start, 0.0, carry_ref[:, c0:c0 + cw])
        carry_ref[:, c0:c0 + cw] = p[tm - 1:tm, :]
        prev = jnp.where(row0, prev_last, pltpu.roll(p, 1, axis=0))
        sh_ref[:, c0:c0 + cw] = p + (prev - p) * mu_ref[:, c0:c0 + cw]
    for c0, cw in _col_chunks(n_rest, 1024):
        rest_ref[:, c0:c0 + cw] = jnp.dot(hb, w_ref[:, n_shift + c0:n_shift + c0 + cw],
                                          preferred_element_type=F32)


def _inproj(x2, ada3, norm_w, w_bf, mu, seq, n_shift):
    t, d = x2.shape
    n_rest = w_bf.shape[1] - n_shift
    tm = ROW_TILE
    tiles_per_seq = seq // tm
    kern = functools.partial(_inproj_kernel, n_shift=n_shift, n_rest=n_rest,
                             tiles_per_seq=tiles_per_seq)
    return pl.pallas_call(
        kern,
        grid=(t // tm,),
        in_specs=[pl.BlockSpec((tm, d), lambda i: (i, 0)),
                  pl.BlockSpec((1, N_ADA, d), lambda i: (i // tiles_per_seq, 0, 0)),
                  pl.BlockSpec((1, d), lambda i: (0, 0)),
                  pl.BlockSpec(w_bf.shape, lambda i: (0, 0), pipeline_mode=pl.Buffered(1)),
                  pl.BlockSpec((1, n_shift), lambda i: (0, 0))],
        out_specs=[pl.BlockSpec((tm, n_shift), lambda i: (i, 0)),
                   pl.BlockSpec((tm, n_rest), lambda i: (i, 0))],
        out_shape=[jax.ShapeDtypeStruct((t, n_shift), F32),
                   jax.ShapeDtypeStruct((t, n_rest), F32)],
        scratch_shapes=[pltpu.VMEM((1, n_shift), F32)],
        compiler_params=pltpu.CompilerParams(dimension_semantics=("arbitrary",),
                                             vmem_limit_bytes=VMEM_LIMIT),
        name="inproj",
    )(x2, ada3, norm_w.reshape(1, d), w_bf, mu.reshape(1, n_shift))


def _softplus(z):
    return jnp.maximum(z, 0.0) + jnp.log1p(jnp.exp(-jnp.abs(z)))


def _rwkv_heads(r, k, v, kkraw, a, cum, lw, s0, masks):
    strict, incl, diag, levels = masks
    c = r.shape[1]
    nrm = jnp.sqrt(jnp.sum(kkraw * kkraw, axis=-1, keepdims=True))
    kk = kkraw / jnp.maximum(nrm, 1e-12)
    w_in = jnp.exp(cum)
    w_ex = jnp.exp(cum - lw)
    w_inv = jnp.exp(-cum)
    w_end = jnp.exp(cum[:, c - 1:c, :] - cum)
    bb = kk * a
    rt = r * w_in
    at = -kk * w_ex
    bt = bb * w_inv
    kt = k * w_inv
    ab = _bnt(at, bt)
    ak = _bnt(at, kt)
    rb = _bnt(rt, bt)
    rk = _bnt(rt, kt)
    lab = jnp.where(strict, ab, 0.0)
    inv = jnp.where(levels[0], lab, 0.0) + jnp.where(diag, 1.0, 0.0)
    for lvl in levels[1:]:
        inv = inv + _bnn(_bnn(inv, jnp.where(lvl, lab, 0.0)), inv)
    rhs = _bnt(at, s0) + _bnn(jnp.where(strict, ak, 0.0), v)
    u = _bnn(inv, rhs)
    y = _bnt(rt, s0) + _bnn(jnp.where(incl, rb, 0.0), u) + _bnn(jnp.where(incl, rk, 0.0), v)
    s_new = s0 * w_in[:, c - 1:c, :] + _btn(u, bb * w_end) + _btn(v, k * w_end)
    return y, s_new


def _rwkv_kernel(pr_ref, pk_ref, pv_ref, tail_ref, w0_ref, wup_ref, a0_ref, aup_ref, gup_ref,
                 kk_ref, ka_ref, rk_ref, lnw_ref, lnb_ref, o_ref, state_ref):
    j = pl.program_id(1)
    c = pr_ref.shape[0]
    n_heads = pr_ref.shape[1] // RWKV_HEAD

    @pl.when(j == 0)
    def _():
        state_ref[...] = jnp.zeros_like(state_ref)

    tail = tail_ref[...]
    pw = tail[:, :DECAY_LORA]
    pa = tail[:, DECAY_LORA:DECAY_LORA + ICLR_LORA]
    pg = tail[:, DECAY_LORA + ICLR_LORA:]
    pk = pk_ref[...]
    w_log = -_softplus(-(w0_ref[...] + _dot_x3(jnp.tanh(pw), wup_ref[...]))) - 0.5
    lw = -jnp.exp(w_log)
    a = jax.nn.sigmoid(a0_ref[...] + _dot_x3(pa, aup_ref[...]))
    row = lax.broadcasted_iota(I32, (c, c), 0)
    col = lax.broadcasted_iota(I32, (c, c), 1)
    strict = row > col
    incl = row >= col
    diag = row == col
    cum = _dot_exact_lhs(incl.astype(BF16), lw)
    k = pk * (1.0 + (a - 1.0) * ka_ref[...])
    kkraw = pk * kk_ref[...]
    g = _dot_x3(jax.nn.sigmoid(pg), gup_ref[...])

    levels = []
    shift = 0
    while (1 << shift) < c:
        levels.append((jnp.right_shift(row, shift + 1) == jnp.right_shift(col, shift + 1))
                      & ((jnp.right_shift(row, shift) & 1) == 1)
                      & ((jnp.right_shift(col, shift) & 1) == 0))
        shift += 1
    masks = (strict, incl, diag, levels)

    def heads(x):
        return jnp.stack([x[:, h * RWKV_HEAD:(h + 1) * RWKV_HEAD] for h in range(n_heads)], axis=0)

    r3, k3, v3 = heads(pr_ref[...]), heads(k), heads(pv_ref[...])
    y, s_new = _rwkv_heads(r3, k3, v3, heads(kkraw), heads(a), heads(cum), heads(lw),
                           state_ref[...], masks)
    state_ref[...] = s_new
    mu = jnp.mean(y, axis=-1, keepdims=True)
    var = jnp.mean(jnp.square(y - mu), axis=-1, keepdims=True)
    yn = (y - mu) * lax.rsqrt(var + LNX_EPS) * lnw_ref[...] + lnb_ref[...]
    bonus = jnp.sum(r3 * k3 * rk_ref[...], axis=-1, keepdims=True) * v3
    out = (yn + bonus) * heads(g)
    for h in range(n_heads):
        o_ref[:, h * RWKV_HEAD:(h + 1) * RWKV_HEAD] = out[h]


def _rwkv(sh, bsz, seq, width, params):
    (w0, w_up, a0, a_up, g_up, k_k, k_a, r_k, lnx_w, lnx_b) = params
    c = RWKV_CHUNK
    nc = seq // c
    heads = width // RWKV_HEAD
    tail_w = DECAY_LORA + ICLR_LORA + GATE_LORA
    tail_blk = (3 * width) // tail_w
    row = lambda b, j: b * nc + j
    vec = lambda a: a.reshape(1, width)
    per_head = lambda a: a.reshape(heads, 1, RWKV_HEAD)
    const = lambda shape: pl.BlockSpec(shape, lambda b, j: (0,) * len(shape))
    return pl.pallas_call(
        _rwkv_kernel,
        grid=(bsz, nc),
        in_specs=[pl.BlockSpec((c, width), lambda b, j: (row(b, j), 0)),
                  pl.BlockSpec((c, width), lambda b, j: (row(b, j), 1)),
                  pl.BlockSpec((c, width), lambda b, j: (row(b, j), 2)),
                  pl.BlockSpec((c, tail_w), lambda b, j: (row(b, j), tail_blk)),
                  const((1, width)), const((DECAY_LORA, width)),
                  const((1, width)), const((ICLR_LORA, width)), const((GATE_LORA, width)),
                  const((1, width)), const((1, width)),
                  const((heads, 1, RWKV_HEAD)), const((heads, 1, RWKV_HEAD)),
                  const((heads, 1, RWKV_HEAD))],
        out_specs=pl.BlockSpec((c, width), lambda b, j: (row(b, j), 0)),
        out_shape=jax.ShapeDtypeStruct((bsz * seq, width), F32),
        scratch_shapes=[pltpu.VMEM((heads, RWKV_HEAD, RWKV_HEAD), F32)],
        compiler_params=pltpu.CompilerParams(dimension_semantics=("arbitrary", "arbitrary"),
                                             vmem_limit_bytes=VMEM_LIMIT),
        name="rwkv",
    )(sh, sh, sh, sh, vec(w0), w_up, vec(a0), a_up, g_up, vec(k_k), vec(k_a), per_head(r_k),
      per_head(lnx_w), per_head(lnx_b))


def _head_rms(x, w):
    return x * lax.rsqrt(jnp.mean(x * x, axis=-1, keepdims=True) + NORM_EPS) * w


def _attn_kernel(sink_ref, q_ref, kvc_ref, kvp_ref, qw_ref, kw_ref, o_ref):
    blk = pl.program_id(1)
    q = q_ref[...]
    kvc = kvc_ref[...]
    kvp = kvp_ref[...]
    kv_w = ATTN_KV_HEADS * ATTN_HEAD
    qi = lax.broadcasted_iota(I32, (BLOCK, 2 * BLOCK), 0)
    kj = lax.broadcasted_iota(I32, (BLOCK, 2 * BLOCK), 1)
    rel = kj - qi
    valid = (rel >= BLOCK - WINDOW + 1) & (rel <= BLOCK) & (blk * BLOCK - BLOCK + kj >= 0)
    for g in range(ATTN_KV_HEADS):
        ks = slice(g * ATTN_HEAD, (g + 1) * ATTN_HEAD)
        vs = slice(kv_w + g * ATTN_HEAD, kv_w + (g + 1) * ATTN_HEAD)
        kband = _head_rms(jnp.concatenate([kvp[:, ks], kvc[:, ks]], axis=0), kw_ref[...])
        vband = jnp.concatenate([kvp[:, vs], kvc[:, vs]], axis=0)
        for n in range(ATTN_GROUP):
            hq = g * ATTN_GROUP + n
            qh = _head_rms(q[:, hq * ATTN_HEAD:(hq + 1) * ATTN_HEAD], qw_ref[...])
            s = jnp.where(valid, _nt(qh, kband) * ATTN_SCALE, NEG_INF)
            sink = sink_ref[hq]
            m = jnp.maximum(jnp.max(s, axis=-1, keepdims=True), sink)
            p = jnp.exp(s - m)
            denom = jnp.sum(p, axis=-1, keepdims=True) + jnp.exp(sink - m)
            o_ref[:, hq * ATTN_HEAD:(hq + 1) * ATTN_HEAD] = _nn(p, vband) / denom


def _attn(rest, bsz, seq, width, q_norm_w, k_norm_w, sinks):
    nb = seq // BLOCK
    kv_w = 2 * ATTN_KV_HEADS * ATTN_HEAD
    kv_blk = (3 * width) // kv_w
    return pl.pallas_call(
        _attn_kernel,
        grid=(bsz, nb),
        in_specs=[pl.BlockSpec(memory_space=pltpu.SMEM),
                  pl.BlockSpec((BLOCK, width), lambda b, j: (b * nb + j, 0)),
                  pl.BlockSpec((BLOCK, kv_w), lambda b, j: (b * nb + j, kv_blk)),
                  pl.BlockSpec((BLOCK, kv_w), lambda b, j: (b * nb + jnp.maximum(j - 1, 0), kv_blk)),
                  pl.BlockSpec((1, ATTN_HEAD), lambda b, j: (0, 0)),
                  pl.BlockSpec((1, ATTN_HEAD), lambda b, j: (0, 0))],
        out_specs=pl.BlockSpec((BLOCK, width), lambda b, j: (b * nb + j, 0)),
        out_shape=jax.ShapeDtypeStruct((bsz * seq, width), F32),
        compiler_params=pltpu.CompilerParams(dimension_semantics=("arbitrary", "arbitrary"),
                                             vmem_limit_bytes=VMEM_LIMIT),
        name="attn",
    )(sinks, rest, rest, rest, q_norm_w.reshape(1, ATTN_HEAD), k_norm_w.reshape(1, ATTN_HEAD))


def _outproj_kernel(ya_ref, yb_ref, ga_ref, gb_ref, x_ref, ada_ref, wo_ref, nw_ref, wq_ref,
                    k1_ref, k2_ref, x1_ref, h2_ref, sct_ref):
    ada = ada_ref[0]
    mixed = jax.nn.sigmoid(ga_ref[...]) * ya_ref[...] + jax.nn.sigmoid(gb_ref[...]) * yb_ref[...]
    x1 = x_ref[...] + ada[2:3] * _nn(mixed, wo_ref[...])
    x1_ref[...] = x1
    ms = jnp.mean(x1 * x1, axis=-1, keepdims=True)
    h2 = x1 * lax.rsqrt(ms + NORM_EPS) * nw_ref[...] * (1.0 + ada[4:5]) + ada[3:4]
    h2_ref[...] = h2
    q = _nn(h2, wq_ref[...])
    for ch in range(q.shape[1] // PEER_HALF):
        keys = k1_ref if ch % 2 == 0 else k2_ref
        cs = slice(ch * PEER_HALF, (ch + 1) * PEER_HALF)
        sct_ref[cs, :] = _nt(keys[...], q[:, cs])


def _outproj(ya, yb, rest, x2, ada3, w_out_bf, norm_w, wq_bf, keys_1, keys_2, seq):
    t, d = x2.shape
    tm = ROW_TILE
    tiles_per_seq = seq // tm
    qd = wq_bf.shape[1]
    rowblk = lambda c: pl.BlockSpec((tm, d), lambda i: (i, c))
    const = lambda shape: pl.BlockSpec(shape, lambda i: (0, 0))
    return pl.pallas_call(
        _outproj_kernel,
        grid=(t // tm,),
        in_specs=[rowblk(0), rowblk(0), rowblk(1), rowblk(2), rowblk(0),
                  pl.BlockSpec((1, N_ADA, d), lambda i: (i // tiles_per_seq, 0, 0)),
                  const((d, d)), const((1, d)), const((d, qd)),
                  const(keys_1.shape), const(keys_2.shape)],
        out_specs=[rowblk(0), rowblk(0), pl.BlockSpec((qd, tm), lambda i: (0, i))],
        out_shape=[jax.ShapeDtypeStruct((t, d), F32), jax.ShapeDtypeStruct((t, d), F32),
                   jax.ShapeDtypeStruct((qd, t), F32)],
        compiler_params=pltpu.CompilerParams(dimension_semantics=("arbitrary",),
                                             vmem_limit_bytes=VMEM_LIMIT),
        name="outproj",
    )(ya, yb, rest, rest, x2, ada3, w_out_bf, norm_w.reshape(1, d), wq_bf, keys_1, keys_2)


def _top16_rows(s, payload=None):
    n = s.shape[0]
    iota = lax.broadcasted_iota(I32, s.shape, 0).astype(F32)
    vals, picks = [], []
    for _ in range(PEER_TOPK):
        m = jnp.max(s, axis=0, keepdims=True)
        i = jnp.min(jnp.where(s == m, iota, float(n)), axis=0, keepdims=True)
        hit = iota == i
        s = jnp.where(hit, -jnp.inf, s)
        vals.append(m)
        picks.append(i if payload is None else
                     jnp.max(jnp.where(hit, payload, -1.0), axis=0, keepdims=True))
    return jnp.concatenate(vals, axis=0), jnp.concatenate(picks, axis=0)


def _topk_kernel(sct_ref, idx_ref, idx_tm_ref, gate_ref):
    keep = [PEER_TOPK // (i + 1) for i in range(PEER_TOPK)]
    pad = (-sum(keep)) % SUBLANES
    idx_all, gate_all = [], []
    for h in range(PEER_HEADS):
        base = h * 2 * N_KEYS
        v1, i1 = _top16_rows(sct_ref[base:base + N_KEYS, :])
        v2, i2 = _top16_rows(sct_ref[base + N_KEYS:base + 2 * N_KEYS, :])
        cand = [v1[i:i + 1, :] + v2[0:keep[i], :] for i in range(PEER_TOPK)]
        cidx = [i1[i:i + 1, :] * float(N_KEYS) + i2[0:keep[i], :] for i in range(PEER_TOPK)]
        if pad:
            cand.append(jnp.full((pad, v1.shape[1]), -jnp.inf, F32))
            cidx.append(jnp.zeros((pad, v1.shape[1]), F32))
        sc, idx = _top16_rows(jnp.concatenate(cand, axis=0), jnp.concatenate(cidx, axis=0))
        idx_all.append(idx.astype(I32))
        ex = jnp.exp(sc - sc[0:1, :])
        gate_all.append(ex / jnp.sum(ex, axis=0, keepdims=True))
    idx = jnp.concatenate(idx_all, axis=0)
    idx_ref[...] = idx
    idx_tm_ref[...] = idx.T
    gate_ref[...] = jnp.concatenate(gate_all, axis=0).T


def _topk(sct):
    qd, t = sct.shape
    tk = TOPK_TILE
    ne = PEER_HEADS * PEER_TOPK
    return pl.pallas_call(
        _topk_kernel,
        grid=(t // tk,),
        in_specs=[pl.BlockSpec((qd, tk), lambda i: (0, i))],
        out_specs=[pl.BlockSpec((ne, tk), lambda i: (0, i)),
                   pl.BlockSpec((tk, ne), lambda i: (i, 0)),
                   pl.BlockSpec((tk, ne), lambda i: (i, 0))],
        out_shape=[jax.ShapeDtypeStruct((ne, t), I32), jax.ShapeDtypeStruct((t, ne), I32),
                   jax.ShapeDtypeStruct((t, ne), F32)],
        compiler_params=pltpu.CompilerParams(dimension_semantics=("arbitrary",),
                                             vmem_limit_bytes=VMEM_LIMIT),
        name="topk",
    )(sct)


def _cast_kernel(x_ref, o_ref):
    o_ref[...] = x_ref[...].astype(BF16)


def _table_bf16(tab):
    n, d = tab.shape
    out = pl.pallas_call(
        _cast_kernel,
        grid=(n // PACK_TILE,),
        in_specs=[pl.BlockSpec((PACK_TILE, d), lambda i: (i, 0))],
        out_specs=pl.BlockSpec((PACK_TILE, d), lambda i: (i, 0)),
        out_shape=jax.ShapeDtypeStruct((n, d), BF16),
        name="table_cast",
    )(tab)
    return out.reshape(n, d // LANES, LANES)


def _gelu_exact(x):
    return 0.5 * x * (1.0 + lax.erf(x * (2.0 ** -0.5)))


def _group_mask(ne):
    sub = lax.broadcasted_iota(I32, (SUBLANES, ne * SUBLANES), 0)
    col = lax.broadcasted_iota(I32, (SUBLANES, ne * SUBLANES), 1)
    return sub == (col & (SUBLANES - 1))


def _gather_rows(idx_ref, tab_ref, t, rows_ref):
    picks = idx_ref.at[:, pl.ds(t, 1)]
    for e in range(idx_ref.shape[0]):
        rows_ref[e * SUBLANES:(e + 1) * SUBLANES, :] = tab_ref[picks[e, 0]]


def _peer_u_kernel(idx_ref, x_ref, gate_ref, tab_ref, wts_ref, *scratch):
    rows, d_s = scratch[:-1], scratch[-1]
    tb = x_ref.shape[0]
    ne = gate_ref.shape[1]
    mask = _group_mask(ne)

    def reduce(t, rows_ref):
        hi, lo = _split2(x_ref[t])
        lhs = jnp.concatenate([hi, lo], axis=0)
        z = lax.dot_general(lhs, rows_ref[...], (((1,), (1,)), ((), ())),
                            preferred_element_type=F32)
        zz = z[:SUBLANES] + z[SUBLANES:]
        d_s[pl.ds(t, 1), :] = jnp.sum(jnp.where(mask, zz, 0.0), axis=0, keepdims=True)

    def group(i, carry):
        for q, rows_ref in enumerate(rows):
            _gather_rows(idx_ref, tab_ref, len(rows) * i + q, rows_ref)
        for q, rows_ref in enumerate(rows):
            reduce(len(rows) * i + q, rows_ref)
        return carry

    lax.fori_loop(0, tb // len(rows), group, 0)
    r = lax.broadcasted_iota(I32, (ne * SUBLANES, ne), 0)
    c = lax.broadcasted_iota(I32, (ne * SUBLANES, ne), 1)
    fold = (jnp.right_shift(r, 3) == c).astype(BF16)
    dots = sum(jnp.dot(p, fold, preferred_element_type=F32) for p in _split3(d_s[...]))
    wts_ref[...] = gate_ref[...] * _gelu_exact(dots)


def _peer_u(idx_t, h3, gate, tab3, t):
    ne = gate.shape[1]
    tb = PEER_TILE
    return pl.pallas_call(
        _peer_u_kernel,
        grid=(t // tb,),
        in_specs=[pl.BlockSpec((ne, tb), lambda i: (0, i), memory_space=pltpu.SMEM),
                  pl.BlockSpec((tb, SUBLANES, LANES), lambda i: (i, 0, 0)),
                  pl.BlockSpec((tb, ne), lambda i: (i, 0)),
                  pl.BlockSpec(tab3.shape, lambda i: (0, 0, 0), pipeline_mode=pl.Buffered(1))],
        out_specs=pl.BlockSpec((tb, ne), lambda i: (i, 0)),
        out_shape=jax.ShapeDtypeStruct((t, ne), F32),
        scratch_shapes=[pltpu.VMEM((ne * SUBLANES, LANES), BF16)] * PEER_GROUP
                       + [pltpu.VMEM((tb, ne * SUBLANES), F32)],
        compiler_params=pltpu.CompilerParams(dimension_semantics=("arbitrary",),
                                             vmem_limit_bytes=VMEM_LIMIT),
        name="peer_u",
    )(idx_t, h3, gate, tab3)


def _peer_v_kernel(idx_ref, wts_ref, x1_ref, gt_ref, tab_ref, o_ref, *scratch):
    rows, w8_s = scratch[:-1], scratch[-1]
    tb = x1_ref.shape[0]
    ne = wts_ref.shape[1]
    mask = _group_mask(ne)
    gt = gt_ref[0]
    r = lax.broadcasted_iota(I32, (ne, ne * SUBLANES), 0)
    c = lax.broadcasted_iota(I32, (ne, ne * SUBLANES), 1)
    spread = (r == jnp.right_shift(c, 3)).astype(BF16)
    w8_s[...] = sum(jnp.dot(p, spread, preferred_element_type=F32) for p in _split3(wts_ref[...]))

    def combine(t, rows_ref):
        wm = jnp.where(mask, w8_s[pl.ds(t, 1), :], 0.0)
        hi, lo = _split2(wm)
        res = jnp.dot(jnp.concatenate([hi, lo], axis=0), rows_ref[...],
                      preferred_element_type=F32)
        o_ref[t] = x1_ref[t] + gt * (res[:SUBLANES] + res[SUBLANES:])

    def group(i, carry):
        for q, rows_ref in enumerate(rows):
            _gather_rows(idx_ref, tab_ref, len(rows) * i + q, rows_ref)
        for q, rows_ref in enumerate(rows):
            combine(len(rows) * i + q, rows_ref)
        return carry

    lax.fori_loop(0, tb // len(rows), group, 0)


def _peer_v(idx_t, wts, x13, gt3, tab3, seq):
    t, ne = wts.shape
    tb = PEER_TILE
    tiles_per_seq = seq // tb
    return pl.pallas_call(
        _peer_v_kernel,
        grid=(t // tb,),
        in_specs=[pl.BlockSpec((ne, tb), lambda i: (0, i), memory_space=pltpu.SMEM),
                  pl.BlockSpec((tb, ne), lambda i: (i, 0)),
                  pl.BlockSpec((tb, SUBLANES, LANES), lambda i: (i, 0, 0)),
                  pl.BlockSpec((1, SUBLANES, LANES), lambda i: (i // tiles_per_seq, 0, 0)),
                  pl.BlockSpec(tab3.shape, lambda i: (0, 0, 0), pipeline_mode=pl.Buffered(1))],
        out_specs=pl.BlockSpec((tb, SUBLANES, LANES), lambda i: (i, 0, 0)),
        out_shape=jax.ShapeDtypeStruct((t, SUBLANES, LANES), F32),
        scratch_shapes=[pltpu.VMEM((ne * SUBLANES, LANES), BF16)] * PEER_GROUP
                       + [pltpu.VMEM((tb, ne * SUBLANES), F32)],
        compiler_params=pltpu.CompilerParams(dimension_semantics=("arbitrary",),
                                             vmem_limit_bytes=VMEM_LIMIT),
        name="peer_v",
    )(idx_t, wts, x13, gt3, tab3)


def _sc_mesh():
    info = plsc.get_sparse_core_info()
    mesh = plsc.VectorSubcoreMesh(core_axis_name="c", subcore_axis_name="s")
    return mesh, info.num_cores * info.num_subcores, info.num_subcores


def _sc_gather_chunks(tab_hbm, idx_v, bufs, sems, n_chunks, consume):
    rows = bufs[0].shape[0]

    def start(ch):
        return pltpu.async_copy(tab_hbm.at[idx_v.at[pl.ds(ch * rows, rows)]],
                                bufs[ch % 2], sems[ch % 2])

    copy = start(0)
    for ch in range(n_chunks):
        nxt = start(ch + 1) if ch + 1 < n_chunks else None
        copy.wait()
        consume(ch, bufs[ch % 2])
        copy = nxt


def _peer_u_sc(idx_tm, h2, utab, tok_base, tsc):
    ne = idx_tm.shape[1]
    d = utab.shape[1]
    mesh, nw, nsub = _sc_mesh()
    tps = tsc // nw
    rows = SC_CHUNK_U
    assert tsc % nw == 0 and ne % rows == 0

    @functools.partial(
        pl.kernel, mesh=mesh, out_type=jax.ShapeDtypeStruct((tsc, ne * SC_LANES), F32),
        scratch_types=[pltpu.VMEM((ne,), I32), pltpu.VMEM((d,), F32),
                       pltpu.VMEM((rows, d), F32), pltpu.VMEM((rows, d), F32),
                       pltpu.VMEM((ne * SC_LANES,), F32),
                       pltpu.SemaphoreType.DMA, pltpu.SemaphoreType.DMA],
        name="peer_u_sc")
    def k(idx_hbm, x_hbm, u_hbm, o_hbm, idx_v, x_v, rows_a, rows_b, dp_v, sem_a, sem_b):
        wid = lax.axis_index("c") * nsub + lax.axis_index("s")
        zero = jnp.zeros((SC_LANES,), F32)

        @pl.loop(0, tps)
        def _(t):
            loc = wid * tps + t
            pltpu.sync_copy(idx_hbm.at[tok_base + loc], idx_v)
            pltpu.sync_copy(x_hbm.at[tok_base + loc], x_v)
            for e in range(ne):
                dp_v[pl.ds(e * SC_LANES, SC_LANES)] = zero

            def consume(ch, rows_v):
                @pl.loop(0, d, step=SC_LANES)
                def _(c):
                    sl = pl.ds(pl.multiple_of(c, SC_LANES), SC_LANES)
                    xv = x_v[sl]

                    @plsc.parallel_loop(0, rows, 1, unroll=8)
                    def _(r):
                        dst = pl.ds(pl.multiple_of((ch * rows + r) * SC_LANES, SC_LANES), SC_LANES)
                        plsc.addupdate(dp_v.at[dst], rows_v[r, sl] * xv)

            _sc_gather_chunks(u_hbm, idx_v, (rows_a, rows_b), (sem_a, sem_b), ne // rows, consume)
            pltpu.sync_copy(dp_v, o_hbm.at[loc])

    return k(idx_tm, h2, utab)


def _sc_weights_kernel(dp_ref, gate_ref, o_ref):
    ne = gate_ref.shape[1]
    r = lax.broadcasted_iota(I32, (ne * SC_LANES, ne), 0)
    c = lax.broadcasted_iota(I32, (ne * SC_LANES, ne), 1)
    fold = (jnp.right_shift(r, 4) == c).astype(BF16)
    dots = sum(jnp.dot(p, fold, preferred_element_type=F32) for p in _split3(dp_ref[...]))
    wts = gate_ref[...] * _gelu_exact(dots)
    r = lax.broadcasted_iota(I32, (ne, ne * SC_LANES), 0)
    c = lax.broadcasted_iota(I32, (ne, ne * SC_LANES), 1)
    spread = (r == jnp.right_shift(c, 4)).astype(BF16)
    o_ref[...] = sum(jnp.dot(p, spread, preferred_element_type=F32) for p in _split3(wts))


def _sc_weights(dparts, gate, tok_base):
    tsc, w = dparts.shape
    ne = gate.shape[1]
    tb = ROW_TILE
    blk0 = tok_base // tb
    return pl.pallas_call(
        _sc_weights_kernel,
        grid=(tsc // tb,),
        in_specs=[pl.BlockSpec((tb, w), lambda i: (i, 0)),
                  pl.BlockSpec((tb, ne), lambda i: (i + blk0, 0))],
        out_specs=pl.BlockSpec((tb, w), lambda i: (i, 0)),
        out_shape=jax.ShapeDtypeStruct((tsc, w), F32),
        name="sc_weights",
    )(dparts, gate)


def _peer_v_sc(idx_tm, w16, x1, gt2, vtab, seq, tok_base):
    tsc = w16.shape[0]
    ne = idx_tm.shape[1]
    d = vtab.shape[1]
    mesh, nw, nsub = _sc_mesh()
    tps = tsc // nw
    nblk = d // SC_LANES
    shift = seq.bit_length() - 1
    rows = SC_CHUNK_V
    assert (1 << shift) == seq and tsc % nw == 0 and ne % rows == 0

    @functools.partial(
        pl.kernel, mesh=mesh, out_type=jax.ShapeDtypeStruct((tsc, d), F32),
        scratch_types=[pltpu.VMEM((ne,), I32), pltpu.VMEM((ne * SC_LANES,), F32),
                       pltpu.VMEM((rows, d), F32), pltpu.VMEM((rows, d), F32),
                       pltpu.VMEM((d,), F32), pltpu.VMEM((d,), F32),
                       pltpu.VMEM(gt2.shape, F32), pltpu.SemaphoreType.DMA, pltpu.SemaphoreType.DMA],
        name="peer_v_sc")
    def k(idx_hbm, w_hbm, x_hbm, gt_hbm, v_hbm, o_hbm, idx_v, w_v, rows_a, rows_b, acc_v, x_v,
          gt_v, sem_a, sem_b):
        wid = lax.axis_index("c") * nsub + lax.axis_index("s")
        pltpu.sync_copy(gt_hbm, gt_v)
        zero = jnp.zeros((SC_LANES,), F32)

        @pl.loop(0, tps)
        def _(t):
            loc = wid * tps + t
            tok = tok_base + loc
            pltpu.sync_copy(idx_hbm.at[tok], idx_v)
            pltpu.sync_copy(w_hbm.at[loc], w_v)
            pltpu.sync_copy(x_hbm.at[tok], x_v)
            for j in range(nblk):
                acc_v[pl.ds(j * SC_LANES, SC_LANES)] = zero

            def consume(ch, rows_v):
                @pl.loop(0, rows)
                def _(e):
                    wv = w_v[pl.ds((ch * rows + e) * SC_LANES, SC_LANES)]

                    @plsc.parallel_loop(0, d, SC_LANES, unroll=8)
                    def _(c):
                        sl = pl.ds(pl.multiple_of(c, SC_LANES), SC_LANES)
                        plsc.addupdate(acc_v.at[sl], wv * rows_v[e, sl])

            _sc_gather_chunks(v_hbm, idx_v, (rows_a, rows_b), (sem_a, sem_b), ne // rows, consume)
            b = jnp.right_shift(tok, shift)
            for j in range(nblk):
                sl = pl.ds(j * SC_LANES, SC_LANES)
                acc_v[sl] = x_v[sl] + gt_v[b, sl] * acc_v[sl]
            pltpu.sync_copy(acc_v, o_hbm.at[loc])

    return k(idx_tm, w16, x1, gt2, vtab)


def _perm_in_columns(width, kv_width):
    n_shift = 3 * width + DECAY_LORA + ICLR_LORA + GATE_LORA
    aq = n_shift
    akv = aq + width
    ga = akv + 2 * kv_width
    gb = ga + width
    return n_shift, np.concatenate([np.arange(0, n_shift), np.arange(aq, aq + width),
                                    np.arange(ga, ga + width), np.arange(gb, gb + width),
                                    np.arange(akv, akv + 2 * kv_width)])


def _layer(x, ada, norm1_w, norm2_w, w_in, shift_mu, w0, w_lora_up, a0, a_lora_up, g_lora_up,
           k_k, k_a, r_k, lnx_w, lnx_b, q_norm_w, k_norm_w, sinks, w_out, peer_w_q,
           peer_keys_1, peer_keys_2, peer_u, peer_v):
    bsz, seq, d = x.shape
    t = bsz * seq
    x2 = x.reshape(t, d)
    ada3 = ada.reshape(bsz, N_ADA, d)
    n_shift, perm = _perm_in_columns(d, ATTN_KV_HEADS * ATTN_HEAD)
    w_bf = w_in[:, perm].astype(BF16)
    sh, rest = _inproj(x2, ada3, norm1_w, w_bf, shift_mu, seq, n_shift)
    ya = _rwkv(sh, bsz, seq, d, (w0, w_lora_up, a0, a_lora_up, g_lora_up, k_k, k_a,
                                 r_k.reshape(-1), lnx_w, lnx_b))
    yb = _attn(rest, bsz, seq, d, q_norm_w, k_norm_w, sinks)
    x1, h2, sct = _outproj(ya, yb, rest, x2, ada3, w_out.astype(BF16), norm2_w,
                           peer_w_q.astype(BF16), peer_keys_1, peer_keys_2, seq)
    idx_t, idx_tm, gate = _topk(sct)
    sub = d // LANES
    gt2 = ada3[:, 5, :]
    t_tc = t - SC_TOKENS
    wts = _peer_u(idx_t, h2.reshape(t, sub, LANES), gate, _table_bf16(peer_u), t_tc)
    out_tc = _peer_v(idx_t, wts, x1.reshape(t, sub, LANES), gt2.reshape(bsz, sub, LANES),
                     _table_bf16(peer_v), seq)
    w16 = _sc_weights(_peer_u_sc(idx_tm, h2, peer_u, t_tc, SC_TOKENS), gate, t_tc)
    out_sc = _peer_v_sc(idx_tm, w16, x1, gt2, peer_v, seq, t_tc)
    return jnp.concatenate([out_tc.reshape(t_tc, d), out_sc], axis=0).reshape(bsz, seq, d)


def kernel(x, c, ada_w, ada_b, norm1_w, norm2_w, w_in, shift_mu, w0, w_lora_up, a0, a_lora_up, g_lora_up, k_k, k_a, r_k, lnx_w, lnx_b, q_norm_w, k_norm_w, sinks, w_out, peer_w_q, peer_keys_1, peer_keys_2, peer_u, peer_v):
    depth = ada_w.shape[0]
    for l in range(depth):
        ada = _ada(c, ada_w[l], ada_b[l])
        x = _layer(x, ada, norm1_w[l], norm2_w[l], w_in[l], shift_mu[l], w0[l], w_lora_up[l],
                   a0[l], a_lora_up[l], g_lora_up[l], k_k[l], k_a[l], r_k[l], lnx_w[l], lnx_b[l],
                   q_norm_w[l], k_norm_w[l], sinks[l], w_out[l], peer_w_q[l], peer_keys_1[l],
                   peer_keys_2[l], peer_u[l], peer_v[l])
    return x
```

```python
import functools

import numpy as np
import jax
import jax.numpy as jnp
from jax import lax
from jax.experimental import pallas as pl
from jax.experimental.pallas import tpu as pltpu
from jax.experimental.pallas import tpu_sc as plsc

F32 = jnp.float32
BF16 = jnp.bfloat16
I32 = jnp.int32
HIGHEST = lax.Precision.HIGHEST

RWKV_HEAD = 64
DECAY_LORA = 64
ICLR_LORA = 64
GATE_LORA = 128
LNX_EPS = 64e-5
ATTN_HEAD = 64
ATTN_Q_HEADS = 16
ATTN_KV_HEADS = 2
ATTN_GROUP = ATTN_Q_HEADS // ATTN_KV_HEADS
WINDOW = 128
BLOCK = 128
ATTN_SCALE = ATTN_HEAD ** -0.5
NEG_INF = -1e30
N_KEYS = 128
PEER_HEADS = 8
PEER_HALF = 128
PEER_TOPK = 16
NORM_EPS = 1e-6
N_ADA = 6

LANES = 128
SUBLANES = 8
VMEM_LIMIT = 56 * 1024 * 1024

RWKV_CHUNK = 64
ROW_TILE = 256
TOPK_TILE = 128
PEER_TILE = 128
PEER_GROUP = 8
SC_TOKENS = 24576
SC_CHUNK_U = 32
SC_CHUNK_V = 16
SC_LANES = 16
SC_PARTS = 8
PACK_TILE = 512


def _nt(a, b):
    return lax.dot_general(a.astype(BF16), b.astype(BF16), (((1,), (1,)), ((), ())),
                           preferred_element_type=F32)


def _nn(a, b):
    return jnp.dot(a.astype(BF16), b.astype(BF16), preferred_element_type=F32)


def _dot_f32(a, b):
    return jnp.dot(a, b, precision=HIGHEST, preferred_element_type=F32)


def _bnt(a, b):
    return lax.dot_general(a.astype(BF16), b.astype(BF16), (((2,), (2,)), ((0,), (0,))),
                           preferred_element_type=F32)


def _bnn(a, b):
    return lax.dot_general(a.astype(BF16), b.astype(BF16), (((2,), (1,)), ((0,), (0,))),
                           preferred_element_type=F32)


def _btn(a, b):
    return lax.dot_general(a.astype(BF16), b.astype(BF16), (((1,), (1,)), ((0,), (0,))),
                           preferred_element_type=F32)


def _split2(x):
    hi = x.astype(BF16)
    lo = (x - hi.astype(F32)).astype(BF16)
    return hi, lo


def _split3(x):
    hi = x.astype(BF16)
    r1 = x - hi.astype(F32)
    mid = r1.astype(BF16)
    lo = (r1 - mid.astype(F32)).astype(BF16)
    return hi, mid, lo


def _dot_x3(a, b):
    ah, al = _split2(a)
    bh, bl = _split2(b)
    d = lambda p, q: jnp.dot(p, q, preferred_element_type=F32)
    return d(ah, bh) + (d(ah, bl) + d(al, bh))


def _dot_exact_lhs(a_bf, b):
    return sum(jnp.dot(a_bf, p, preferred_element_type=F32) for p in _split3(b))


def _ada_kernel(c_ref, w_ref, b_ref, o_ref):
    c = c_ref[...]
    cond = c * jax.nn.sigmoid(c)
    o_ref[...] = _dot_f32(cond, w_ref[...]) + b_ref[...]


def _ada(c, ada_w, ada_b):
    bsz, d = c.shape
    n = ada_w.shape[1]
    return pl.pallas_call(
        _ada_kernel,
        grid=(n // d,),
        in_specs=[pl.BlockSpec((bsz, d), lambda j: (0, 0)),
                  pl.BlockSpec((d, d), lambda j: (0, j)),
                  pl.BlockSpec((1, d), lambda j: (0, j))],
        out_specs=pl.BlockSpec((bsz, d), lambda j: (0, j)),
        out_shape=jax.ShapeDtypeStruct((bsz, n), F32),
        name="ada",
    )(c, ada_w, ada_b.reshape(1, n))


def _col_chunks(width, step):
    return [(c0, min(step, width - c0)) for c0 in range(0, width, step)]


def _inproj_kernel(x_ref, ada_ref, nw_ref, w_ref, mu_ref, sh_ref, rest_ref, carry_ref,
                   *, n_shift, n_rest, tiles_per_seq):
    i = pl.program_id(0)
    x = x_ref[...]
    tm = x.shape[0]
    ms = jnp.mean(x * x, axis=-1, keepdims=True)
    ada = ada_ref[0]
    h = x * lax.rsqrt(ms + NORM_EPS) * nw_ref[...] * (1.0 + ada[1:2]) + ada[0:1]
    hb = h.astype(BF16)
    row0 = lax.broadcasted_iota(I32, (tm, 1), 0) == 0
    seq_start = (i % tiles_per_seq) == 0
    for c0, cw in _col_chunks(n_shift, 1024):
        p = jnp.dot(hb, w_ref[:, c0:c0 + cw], preferred_element_type=F32)
        prev_last = jnp.where(seq_start, 0.0, carry_ref[:, c0:c0 + cw])
        carry_ref[:, c0:c0 + cw] = p[tm - 1:tm, :]
        prev = jnp.where(row0, prev_last, pltpu.roll(p, 1, axis=0))
        sh_ref[:, c0:c0 + cw] = p + (prev - p) * mu_ref[:, c0:c0 + cw]
    for c0, cw in _col_chunks(n_rest, 1024):
        rest_ref[:, c0:c0 + cw] = jnp.dot(hb, w_ref[:, n_shift + c0:n_shift + c0 + cw],
                                          preferred_element_type=F32)


def _inproj(x2, ada3, norm_w, w_bf, mu, seq, n_shift):
    t, d = x2.shape
    n_rest = w_bf.shape[1] - n_shift
    tm = ROW_TILE
    tiles_per_seq = seq // tm
    kern = functools.partial(_inproj_kernel, n_shift=n_shift, n_rest=n_rest,
                             tiles_per_seq=tiles_per_seq)
    return pl.pallas_call(
        kern,
        grid=(t // tm,),
        in_specs=[pl.BlockSpec((tm, d), lambda i: (i, 0)),
                  pl.BlockSpec((1, N_ADA, d), lambda i: (i // tiles_per_seq, 0, 0)),
                  pl.BlockSpec((1, d), lambda i: (0, 0)),
                  pl.BlockSpec(w_bf.shape, lambda i: (0, 0), pipeline_mode=pl.Buffered(1)),
                  pl.BlockSpec((1, n_shift), lambda i: (0, 0))],
        out_specs=[pl.BlockSpec((tm, n_shift), lambda i: (i, 0)),
                   pl.BlockSpec((tm, n_rest), lambda i: (i, 0))],
        out_shape=[jax.ShapeDtypeStruct((t, n_shift), F32),
                   jax.ShapeDtypeStruct((t, n_rest), F32)],
        scratch_shapes=[pltpu.VMEM((1, n_shift), F32)],
        compiler_params=pltpu.CompilerParams(dimension_semantics=("arbitrary",),
                                             vmem_limit_bytes=VMEM_LIMIT),
        name="inproj",
    )(x2, ada3, norm_w.reshape(1, d), w_bf, mu.reshape(1, n_shift))


def _softplus(z):
    return jnp.maximum(z, 0.0) + jnp.log1p(jnp.exp(-jnp.abs(z)))


def _rwkv_heads(r, k, v, kkraw, a, cum, lw, s0, masks):
    strict, incl, diag, levels = masks
    c = r.shape[1]
    nrm = jnp.sqrt(jnp.sum(kkraw * kkraw, axis=-1, keepdims=True))
    kk = kkraw / jnp.maximum(nrm, 1e-12)
    w_in = jnp.exp(cum)
    w_ex = jnp.exp(cum - lw)
    w_inv = jnp.exp(-cum)
    w_end = jnp.exp(cum[:, c - 1:c, :] - cum)
    bb = kk * a
    rt = r * w_in
    at = -kk * w_ex
    bt = bb * w_inv
    kt = k * w_inv
    ab = _bnt(at, bt)
    ak = _bnt(at, kt)
    rb = _bnt(rt, bt)
    rk = _bnt(rt, kt)
    lab = jnp.where(strict, ab, 0.0)
    inv = jnp.where(levels[0], lab, 0.0) + jnp.where(diag, 1.0, 0.0)
    for lvl in levels[1:]:
        inv = inv + _bnn(_bnn(inv, jnp.where(lvl, lab, 0.0)), inv)
    rhs = _bnt(at, s0) + _bnn(jnp.where(strict, ak, 0.0), v)
    u = _bnn(inv, rhs)
    y = _bnt(rt, s0) + _bnn(jnp.where(incl, rb, 0.0), u) + _bnn(jnp.where(incl, rk, 0.0), v)
    s_new = s0 * w_in[:, c - 1:c, :] + _btn(u, bb * w_end) + _btn(v, k * w_end)
    return y, s_new


def _rwkv_kernel(pr_ref, pk_ref, pv_ref, tail_ref, w0_ref, wup_ref, a0_ref, aup_ref, gup_ref,
                 kk_ref, ka_ref, rk_ref, lnw_ref, lnb_ref, o_ref, state_ref):
    j = pl.program_id(1)
    c = pr_ref.shape[0]
    n_heads = pr_ref.shape[1] // RWKV_HEAD

    @pl.when(j == 0)
    def _():
        state_ref[...] = jnp.zeros_like(state_ref)

    tail = tail_ref[...]
    pw = tail[:, :DECAY_LORA]
    pa = tail[:, DECAY_LORA:DECAY_LORA + ICLR_LORA]
    pg = tail[:, DECAY_LORA + ICLR_LORA:]
    pk = pk_ref[...]
    w_log = -_softplus(-(w0_ref[...] + _dot_x3(jnp.tanh(pw), wup_ref[...]))) - 0.5
    lw = -jnp.exp(w_log)
    a = jax.nn.sigmoid(a0_ref[...] + _dot_x3(pa, aup_ref[...]))
    row = lax.broadcasted_iota(I32, (c, c), 0)
    col = lax.broadcasted_iota(I32, (c, c), 1)
    strict = row > col
    incl = row >= col
    diag = row == col
    cum = _dot_exact_lhs(incl.astype(BF16), lw)
    k = pk * (1.0 + (a - 1.0) * ka_ref[...])
    kkraw = pk * kk_ref[...]
    g = _dot_x3(jax.nn.sigmoid(pg), gup_ref[...])

    levels = []
    shift = 0
    while (1 << shift) < c:
        levels.append((jnp.right_shift(row, shift + 1) == jnp.right_shift(col, shift + 1))
                      & ((jnp.right_shift(row, shift) & 1) == 1)
                      & ((jnp.right_shift(col, shift) & 1) == 0))
        shift += 1
    masks = (strict, incl, diag, levels)

    def heads(x):
        return jnp.stack([x[:, h * RWKV_HEAD:(h + 1) * RWKV_HEAD] for h in range(n_heads)], axis=0)

    r3, k3, v3 = heads(pr_ref[...]), heads(k), heads(pv_ref[...])
    y, s_new = _rwkv_heads(r3, k3, v3, heads(kkraw), heads(a), heads(cum), heads(lw),
                           state_ref[...], masks)
    state_ref[...] = s_new
    mu = jnp.mean(y, axis=-1, keepdims=True)
    var = jnp.mean(jnp.square(y - mu), axis=-1, keepdims=True)
    yn = (y - mu) * lax.rsqrt(var + LNX_EPS) * lnw_ref[...] + lnb_ref[...]
    bonus = jnp.sum(r3 * k3 * rk_ref[...], axis=-1, keepdims=True) * v3
    out = (yn + bonus) * heads(g)
    for h in range(n_heads):
        o_ref[:, h * RWKV_HEAD:(h + 1) * RWKV_HEAD] = out[h]


def _rwkv(sh, bsz, seq, width, params):
    (w0, w_up, a0, a_up, g_up, k_k, k_a, r_k, lnx_w, lnx_b) = params
    c = RWKV_CHUNK
    nc = seq // c
    heads = width // RWKV_HEAD
    tail_w = DECAY_LORA + ICLR_LORA + GATE_LORA
    tail_blk = (3 * width) // tail_w
    row = lambda b, j: b * nc + j
    vec = lambda a: a.reshape(1, width)
    per_head = lambda a: a.reshape(heads, 1, RWKV_HEAD)
    const = lambda shape: pl.BlockSpec(shape, lambda b, j: (0,) * len(shape))
    return pl.pallas_call(
        _rwkv_kernel,
        grid=(bsz, nc),
        in_specs=[pl.BlockSpec((c, width), lambda b, j: (row(b, j), 0)),
                  pl.BlockSpec((c, width), lambda b, j: (row(b, j), 1)),
                  pl.BlockSpec((c, width), lambda b, j: (row(b, j), 2)),
                  pl.BlockSpec((c, tail_w), lambda b, j: (row(b, j), tail_blk)),
                  const((1, width)), const((DECAY_LORA, width)),
                  const((1, width)), const((ICLR_LORA, width)), const((GATE_LORA, width)),
                  const((1, width)), const((1, width)),
                  const((heads, 1, RWKV_HEAD)), const((heads, 1, RWKV_HEAD)),
                  const((heads, 1, RWKV_HEAD))],
        out_specs=pl.BlockSpec((c, width), lambda b, j: (row(b, j), 0)),
        out_shape=jax.ShapeDtypeStruct((bsz * seq, width), F32),
        scratch_shapes=[pltpu.VMEM((heads, RWKV_HEAD, RWKV_HEAD), F32)],
        compiler_params=pltpu.CompilerParams(dimension_semantics=("arbitrary", "arbitrary"),
                                             vmem_limit_bytes=VMEM_LIMIT),
        name="rwkv",
    )(sh, sh, sh, sh, vec(w0), w_up, vec(a0), a_up, g_up, vec(k_k), vec(k_a), per_head(r_k),
      per_head(lnx_w), per_head(lnx_b))


def _head_rms(x, w):
    return x * lax.rsqrt(jnp.mean(x * x, axis=-1, keepdims=True) + NORM_EPS) * w


def _attn_kernel(sink_ref, q_ref, kvc_ref, kvp_ref, qw_ref, kw_ref, o_ref):
    blk = pl.program_id(1)
    q = q_ref[...]
    kvc = kvc_ref[...]
    kvp = kvp_ref[...]
    kv_w = ATTN_KV_HEADS * ATTN_HEAD
    qi = lax.broadcasted_iota(I32, (BLOCK, 2 * BLOCK), 0)
    kj = lax.broadcasted_iota(I32, (BLOCK, 2 * BLOCK), 1)
    rel = kj - qi
    valid = (rel >= BLOCK - WINDOW + 1) & (rel <= BLOCK) & (blk * BLOCK - BLOCK + kj >= 0)
    for g in range(ATTN_KV_HEADS):
        ks = slice(g * ATTN_HEAD, (g + 1) * ATTN_HEAD)
        vs = slice(kv_w + g * ATTN_HEAD, kv_w + (g + 1) * ATTN_HEAD)
        kband = _head_rms(jnp.concatenate([kvp[:, ks], kvc[:, ks]], axis=0), kw_ref[...])
        vband = jnp.concatenate([kvp[:, vs], kvc[:, vs]], axis=0)
        for n in range(ATTN_GROUP):
            hq = g * ATTN_GROUP + n
            qh = _head_rms(q[:, hq * ATTN_HEAD:(hq + 1) * ATTN_HEAD], qw_ref[...])
            s = jnp.where(valid, _nt(qh, kband) * ATTN_SCALE, NEG_INF)
            sink = sink_ref[hq]
            m = jnp.maximum(jnp.max(s, axis=-1, keepdims=True), sink)
            p = jnp.exp(s - m)
            denom = jnp.sum(p, axis=-1, keepdims=True) + jnp.exp(sink - m)
            o_ref[:, hq * ATTN_HEAD:(hq + 1) * ATTN_HEAD] = _nn(p, vband) / denom


def _attn(rest, bsz, seq, width, q_norm_w, k_norm_w, sinks):
    nb = seq // BLOCK
    kv_w = 2 * ATTN_KV_HEADS * ATTN_HEAD
    kv_blk = (3 * width) // kv_w
    return pl.pallas_call(
        _attn_kernel,
        grid=(bsz, nb),
        in_specs=[pl.BlockSpec(memory_space=pltpu.SMEM),
                  pl.BlockSpec((BLOCK, width), lambda b, j: (b * nb + j, 0)),
                  pl.BlockSpec((BLOCK, kv_w), lambda b, j: (b * nb + j, kv_blk)),
                  pl.BlockSpec((BLOCK, kv_w), lambda b, j: (b * nb + jnp.maximum(j - 1, 0), kv_blk)),
                  pl.BlockSpec((1, ATTN_HEAD), lambda b, j: (0, 0)),
                  pl.BlockSpec((1, ATTN_HEAD), lambda b, j: (0, 0))],
        out_specs=pl.BlockSpec((BLOCK, width), lambda b, j: (b * nb + j, 0)),
        out_shape=jax.ShapeDtypeStruct((bsz * seq, width), F32),
        compiler_params=pltpu.CompilerParams(dimension_semantics=("arbitrary", "arbitrary"),
                                             vmem_limit_bytes=VMEM_LIMIT),
        name="attn",
    )(sinks, rest, rest, rest, q_norm_w.reshape(1, ATTN_HEAD), k_norm_w.reshape(1, ATTN_HEAD))


def _outproj_kernel(ya_ref, yb_ref, ga_ref, gb_ref, x_ref, ada_ref, wo_ref, nw_ref, wq_ref,
                    k1_ref, k2_ref, x1_ref, h2_ref, sct_ref):
    ada = ada_ref[0]
    mixed = jax.nn.sigmoid(ga_ref[...]) * ya_ref[...] + jax.nn.sigmoid(gb_ref[...]) * yb_ref[...]
    x1 = x_ref[...] + ada[2:3] * _nn(mixed, wo_ref[...])
    x1_ref[...] = x1
    ms = jnp.mean(x1 * x1, axis=-1, keepdims=True)
    h2 = x1 * lax.rsqrt(ms + NORM_EPS) * nw_ref[...] * (1.0 + ada[4:5]) + ada[3:4]
    h2_ref[...] = h2
    q = _nn(h2, wq_ref[...])
    for ch in range(q.shape[1] // PEER_HALF):
        keys = k1_ref if ch % 2 == 0 else k2_ref
        cs = slice(ch * PEER_HALF, (ch + 1) * PEER_HALF)
        sct_ref[cs, :] = _nt(keys[...], q[:, cs])


def _outproj(ya, yb, rest, x2, ada3, w_out_bf, norm_w, wq_bf, keys_1, keys_2, seq):
    t, d = x2.shape
    tm = ROW_TILE
    tiles_per_seq = seq // tm
    qd = wq_bf.shape[1]
    rowblk = lambda c: pl.BlockSpec((tm, d), lambda i: (i, c))
    const = lambda shape: pl.BlockSpec(shape, lambda i: (0, 0))
    return pl.pallas_call(
        _outproj_kernel,
        grid=(t // tm,),
        in_specs=[rowblk(0), rowblk(0), rowblk(1), rowblk(2), rowblk(0),
                  pl.BlockSpec((1, N_ADA, d), lambda i: (i // tiles_per_seq, 0, 0)),
                  const((d, d)), const((1, d)), const((d, qd)),
                  const(keys_1.shape), const(keys_2.shape)],
        out_specs=[rowblk(0), rowblk(0), pl.BlockSpec((qd, tm), lambda i: (0, i))],
        out_shape=[jax.ShapeDtypeStruct((t, d), F32), jax.ShapeDtypeStruct((t, d), F32),
                   jax.ShapeDtypeStruct((qd, t), F32)],
        compiler_params=pltpu.CompilerParams(dimension_semantics=("arbitrary",),
                                             vmem_limit_bytes=VMEM_LIMIT),
        name="outproj",
    )(ya, yb, rest, rest, x2, ada3, w_out_bf, norm_w.reshape(1, d), wq_bf, keys_1, keys_2)


def _top16_rows(s, payload=None):
    n = s.shape[0]
    iota = lax.broadcasted_iota(I32, s.shape, 0).astype(F32)
    vals, picks = [], []
    for _ in range(PEER_TOPK):
        m = jnp.max(s, axis=0, keepdims=True)
        i = jnp.min(jnp.where(s == m, iota, float(n)), axis=0, keepdims=True)
        hit = iota == i
        s = jnp.where(hit, -jnp.inf, s)
        vals.append(m)
        picks.append(i if payload is None else
                     jnp.max(jnp.where(hit, payload, -1.0), axis=0, keepdims=True))
    return jnp.concatenate(vals, axis=0), jnp.concatenate(picks, axis=0)


def _topk_kernel(sct_ref, idx_ref, idx_tm_ref, gate_ref):
    keep = [PEER_TOPK // (i + 1) for i in range(PEER_TOPK)]
    pad = (-sum(keep)) % SUBLANES
    idx_all, gate_all = [], []
    for h in range(PEER_HEADS):
        base = h * 2 * N_KEYS
        v1, i1 = _top16_rows(sct_ref[base:base + N_KEYS, :])
        v2, i2 = _top16_rows(sct_ref[base + N_KEYS:base + 2 * N_KEYS, :])
        cand = [v1[i:i + 1, :] + v2[0:keep[i], :] for i in range(PEER_TOPK)]
        cidx = [i1[i:i + 1, :] * float(N_KEYS) + i2[0:keep[i], :] for i in range(PEER_TOPK)]
        if pad:
            cand.append(jnp.full((pad, v1.shape[1]), -jnp.inf, F32))
            cidx.append(jnp.zeros((pad, v1.shape[1]), F32))
        sc, idx = _top16_rows(jnp.concatenate(cand, axis=0), jnp.concatenate(cidx, axis=0))
        idx_all.append(idx.astype(I32))
        ex = jnp.exp(sc - sc[0:1, :])
        gate_all.append(ex / jnp.sum(ex, axis=0, keepdims=True))
    idx = jnp.concatenate(idx_all, axis=0)
    idx_ref[...] = idx
    idx_tm_ref[...] = idx.T
    gate_ref[...] = jnp.concatenate(gate_all, axis=0).T


def _topk(sct):
    qd, t = sct.shape
    tk = TOPK_TILE
    ne = PEER_HEADS * PEER_TOPK
    return pl.pallas_call(
        _topk_kernel,
        grid=(t // tk,),
        in_specs=[pl.BlockSpec((qd, tk), lambda i: (0, i))],
        out_specs=[pl.BlockSpec((ne, tk), lambda i: (0, i)),
                   pl.BlockSpec((tk, ne), lambda i: (i, 0)),
                   pl.BlockSpec((tk, ne), lambda i: (i, 0))],
        out_shape=[jax.ShapeDtypeStruct((ne, t), I32), jax.ShapeDtypeStruct((t, ne), I32),
                   jax.ShapeDtypeStruct((t, ne), F32)],
        compiler_params=pltpu.CompilerParams(dimension_semantics=("arbitrary",),
                                             vmem_limit_bytes=VMEM_LIMIT),
        name="topk",
    )(sct)


def _cast_kernel(x_ref, o_ref):
    o_ref[...] = x_ref[...].astype(BF16)


def _table_bf16(tab):
    n, d = tab.shape
    out = pl.pallas_call(
        _cast_kernel,
        grid=(n // PACK_TILE,),
        in_specs=[pl.BlockSpec((PACK_TILE, d), lambda i: (i, 0))],
        out_specs=pl.BlockSpec((PACK_TILE, d), lambda i: (i, 0)),
        out_shape=jax.ShapeDtypeStruct((n, d), BF16),
        name="table_cast",
    )(tab)
    return out.reshape(n, d // LANES, LANES)


def _gelu_exact(x):
    return 0.5 * x * (1.0 + lax.erf(x * (2.0 ** -0.5)))


def _group_mask(ne):
    sub = lax.broadcasted_iota(I32, (SUBLANES, ne * SUBLANES), 0)
    col = lax.broadcasted_iota(I32, (SUBLANES, ne * SUBLANES), 1)
    return sub == (col & (SUBLANES - 1))


def _gather_rows(idx_ref, tab_ref, t, rows_ref):
    picks = idx_ref.at[:, pl.ds(t, 1)]
    for e in range(idx_ref.shape[0]):
        rows_ref[e * SUBLANES:(e + 1) * SUBLANES, :] = tab_ref[picks[e, 0]]


def _peer_u_kernel(idx_ref, x_ref, gate_ref, tab_ref, wts_ref, *scratch):
    rows, d_s = scratch[:-1], scratch[-1]
    tb = x_ref.shape[0]
    ne = gate_ref.shape[1]
    mask = _group_mask(ne)

    def reduce(t, rows_ref):
        hi, lo = _split2(x_ref[t])
        lhs = jnp.concatenate([hi, lo], axis=0)
        z = lax.dot_general(lhs, rows_ref[...], (((1,), (1,)), ((), ())),
                            preferred_element_type=F32)
        zz = z[:SUBLANES] + z[SUBLANES:]
        d_s[pl.ds(t, 1), :] = jnp.sum(jnp.where(mask, zz, 0.0), axis=0, keepdims=True)

    def group(i, carry):
        for q, rows_ref in enumerate(rows):
            _gather_rows(idx_ref, tab_ref, len(rows) * i + q, rows_ref)
        for q, rows_ref in enumerate(rows):
            reduce(len(rows) * i + q, rows_ref)
        return carry

    lax.fori_loop(0, tb // len(rows), group, 0)
    r = lax.broadcasted_iota(I32, (ne * SUBLANES, ne), 0)
    c = lax.broadcasted_iota(I32, (ne * SUBLANES, ne), 1)
    fold = (jnp.right_shift(r, 3) == c).astype(BF16)
    dots = sum(jnp.dot(p, fold, preferred_element_type=F32) for p in _split3(d_s[...]))
    wts_ref[...] = gate_ref[...] * _gelu_exact(dots)


def _peer_u(idx_t, h3, gate, tab3, t):
    ne = gate.shape[1]
    tb = PEER_TILE
    return pl.pallas_call(
        _peer_u_kernel,
        grid=(t // tb,),
        in_specs=[pl.BlockSpec((ne, tb), lambda i: (0, i), memory_space=pltpu.SMEM),
                  pl.BlockSpec((tb, SUBLANES, LANES), lambda i: (i, 0, 0)),
                  pl.BlockSpec((tb, ne), lambda i: (i, 0)),
                  pl.BlockSpec(tab3.shape, lambda i: (0, 0, 0), pipeline_mode=pl.Buffered(1))],
        out_specs=pl.BlockSpec((tb, ne), lambda i: (i, 0)),
        out_shape=jax.ShapeDtypeStruct((t, ne), F32),
        scratch_shapes=[pltpu.VMEM((ne * SUBLANES, LANES), BF16)] * PEER_GROUP
                       + [pltpu.VMEM((tb, ne * SUBLANES), F32)],
        compiler_params=pltpu.CompilerParams(dimension_semantics=("arbitrary",),
                                             vmem_limit_bytes=VMEM_LIMIT),
        name="peer_u",
    )(idx_t, h3, gate, tab3)


def _peer_v_kernel(idx_ref, wts_ref, x1_ref, gt_ref, tab_ref, o_ref, *scratch):
    rows, w8_s = scratch[:-1], scratch[-1]
    tb = x1_ref.shape[0]
    ne = wts_ref.shape[1]
    mask = _group_mask(ne)
    gt = gt_ref[0]
    r = lax.broadcasted_iota(I32, (ne, ne * SUBLANES), 0)
    c = lax.broadcasted_iota(I32, (ne, ne * SUBLANES), 1)
    spread = (r == jnp.right_shift(c, 3)).astype(BF16)
    w8_s[...] = sum(jnp.dot(p, spread, preferred_element_type=F32) for p in _split3(wts_ref[...]))

    def combine(t, rows_ref):
        wm = jnp.where(mask, w8_s[pl.ds(t, 1), :], 0.0)
        hi, lo = _split2(wm)
        res = jnp.dot(jnp.concatenate([hi, lo], axis=0), rows_ref[...],
                      preferred_element_type=F32)
        o_ref[t] = x1_ref[t] + gt * (res[:SUBLANES] + res[SUBLANES:])

    def group(i, carry):
        for q, rows_ref in enumerate(rows):
            _gather_rows(idx_ref, tab_ref, len(rows) * i + q, rows_ref)
        for q, rows_ref in enumerate(rows):
            combine(len(rows) * i + q, rows_ref)
        return carry

    lax.fori_loop(0, tb // len(rows), group, 0)


def _peer_v(idx_t, wts, x13, gt3, tab3, seq):
    t, ne = wts.shape
    tb = PEER_TILE
    tiles_per_seq = seq // tb
    return pl.pallas_call(
        _peer_v_kernel,
        grid=(t // tb,),
        in_specs=[pl.BlockSpec((ne, tb), lambda i: (0, i), memory_space=pltpu.SMEM),
                  pl.BlockSpec((tb, ne), lambda i: (i, 0)),
                  pl.BlockSpec((tb, SUBLANES, LANES), lambda i: (i, 0, 0)),
                  pl.BlockSpec((1, SUBLANES, LANES), lambda i: (i // tiles_per_seq, 0, 0)),
                  pl.BlockSpec(tab3.shape, lambda i: (0, 0, 0), pipeline_mode=pl.Buffered(1))],
        out_specs=pl.BlockSpec((tb, SUBLANES, LANES), lambda i: (i, 0, 0)),
        out_shape=jax.ShapeDtypeStruct((t, SUBLANES, LANES), F32),
        scratch_shapes=[pltpu.VMEM((ne * SUBLANES, LANES), BF16)] * PEER_GROUP
                       + [pltpu.VMEM((tb, ne * SUBLANES), F32)],
        compiler_params=pltpu.CompilerParams(dimension_semantics=("arbitrary",),
                                             vmem_limit_bytes=VMEM_LIMIT),
        name="peer_v",
    )(idx_t, wts, x13, gt3, tab3)


def _sc_mesh():
    info = plsc.get_sparse_core_info()
    mesh = plsc.VectorSubcoreMesh(core_axis_name="c", subcore_axis_name="s")
    return mesh, info.num_cores * info.num_subcores, info.num_subcores


def _sc_gather_chunks(tab_hbm, idx_v, bufs, sems, n_chunks, consume):
    rows = bufs[0].shape[0]

    def start(ch):
        return pltpu.async_copy(tab_hbm.at[idx_v.at[pl.ds(ch * rows, rows)]],
                                bufs[ch % 2], sems[ch % 2])

    copy = start(0)
    for ch in range(n_chunks):
        nxt = start(ch + 1) if ch + 1 < n_chunks else None
        copy.wait()
        consume(ch, bufs[ch % 2])
        copy = nxt


def _peer_u_sc(idx_tm, h2, utab, tok_base, tsc):
    ne = idx_tm.shape[1]
    d = utab.shape[1]
    mesh, nw, nsub = _sc_mesh()
    tps = tsc // nw
    rows = SC_CHUNK_U
    part_w = SC_PARTS * SC_LANES
    assert tsc % nw == 0 and ne % rows == 0 and rows % 2 == 0 and d % part_w == 0

    @functools.partial(
        pl.kernel, mesh=mesh, out_type=jax.ShapeDtypeStruct((tsc, ne * SC_LANES), F32),
        scratch_types=[pltpu.VMEM((ne,), I32), pltpu.VMEM((d,), F32),
                       pltpu.VMEM((rows, d), F32), pltpu.VMEM((rows, d), F32),
                       pltpu.VMEM((rows * part_w,), F32), pltpu.VMEM((ne * SC_LANES,), F32),
                       pltpu.SemaphoreType.DMA, pltpu.SemaphoreType.DMA],
        name="peer_u_sc")
    def k(idx_hbm, x_hbm, u_hbm, o_hbm, idx_v, x_v, rows_a, rows_b, part_v, dp_v, sem_a, sem_b):
        wid = lax.axis_index("c") * nsub + lax.axis_index("s")
        zero = jnp.zeros((SC_LANES,), F32)

        @pl.loop(0, tps)
        def _(t):
            loc = wid * tps + t
            pltpu.sync_copy(idx_hbm.at[tok_base + loc], idx_v)
            pltpu.sync_copy(x_hbm.at[tok_base + loc], x_v)
            def consume(ch, rows_v):
                @plsc.parallel_loop(0, rows * part_w, SC_LANES, unroll=8)
                def _(i):
                    part_v[pl.ds(pl.multiple_of(i, SC_LANES), SC_LANES)] = zero

                @pl.loop(0, rows, step=2)
                def _(r):
                    @plsc.parallel_loop(0, d, SC_LANES, unroll=SC_PARTS)
                    def _(c):
                        sl = pl.ds(pl.multiple_of(c, SC_LANES), SC_LANES)
                        xv = x_v[sl]
                        k = jnp.bitwise_and(c, part_w - 1)
                        for q in range(2):
                            dst = pl.ds(pl.multiple_of((r + q) * part_w + k, SC_LANES), SC_LANES)
                            plsc.addupdate(part_v.at[dst], rows_v[r + q, sl] * xv)

                @plsc.parallel_loop(0, rows, 1, unroll=2)
                def _(r):
                    acc = part_v[pl.ds(pl.multiple_of(r * part_w, SC_LANES), SC_LANES)]
                    for p in range(1, SC_PARTS):
                        acc = acc + part_v[pl.ds(pl.multiple_of(r * part_w + p * SC_LANES, SC_LANES), SC_LANES)]
                    dp_v[pl.ds(pl.multiple_of((ch * rows + r) * SC_LANES, SC_LANES), SC_LANES)] = acc

            _sc_gather_chunks(u_hbm, idx_v, (rows_a, rows_b), (sem_a, sem_b), ne // rows, consume)
            pltpu.sync_copy(dp_v, o_hbm.at[loc])

    return k(idx_tm, h2, utab)


def _sc_weights_kernel(dp_ref, gate_ref, o_ref):
    ne = gate_ref.shape[1]
    r = lax.broadcasted_iota(I32, (ne * SC_LANES, ne), 0)
    c = lax.broadcasted_iota(I32, (ne * SC_LANES, ne), 1)
    fold = (jnp.right_shift(r, 4) == c).astype(BF16)
    dots = sum(jnp.dot(p, fold, preferred_element_type=F32) for p in _split3(dp_ref[...]))
    wts = gate_ref[...] * _gelu_exact(dots)
    r = lax.broadcasted_iota(I32, (ne, ne * SC_LANES), 0)
    c = lax.broadcasted_iota(I32, (ne, ne * SC_LANES), 1)
    spread = (r == jnp.right_shift(c, 4)).astype(BF16)
    o_ref[...] = sum(jnp.dot(p, spread, preferred_element_type=F32) for p in _split3(wts))


def _sc_weights(dparts, gate, tok_base):
    tsc, w = dparts.shape
    ne = gate.shape[1]
    tb = ROW_TILE
    blk0 = tok_base // tb
    return pl.pallas_call(
        _sc_weights_kernel,
        grid=(tsc // tb,),
        in_specs=[pl.BlockSpec((tb, w), lambda i: (i, 0)),
                  pl.BlockSpec((tb, ne), lambda i: (i + blk0, 0))],
        out_specs=pl.BlockSpec((tb, w), lambda i: (i, 0)),
        out_shape=jax.ShapeDtypeStruct((tsc, w), F32),
        name="sc_weights",
    )(dparts, gate)


def _peer_v_sc(idx_tm, w16, x1, gt2, vtab, seq, tok_base):
    tsc = w16.shape[0]
    ne = idx_tm.shape[1]
    d = vtab.shape[1]
    mesh, nw, nsub = _sc_mesh()
    tps = tsc // nw
    nblk = d // SC_LANES
    shift = seq.bit_length() - 1
    rows = SC_CHUNK_V
    assert (1 << shift) == seq and tsc % nw == 0 and ne % rows == 0

    @functools.partial(
        pl.kernel, mesh=mesh, out_type=jax.ShapeDtypeStruct((tsc, d), F32),
        scratch_types=[pltpu.VMEM((ne,), I32), pltpu.VMEM((ne * SC_LANES,), F32),
                       pltpu.VMEM((rows, d), F32), pltpu.VMEM((rows, d), F32),
                       pltpu.VMEM((d,), F32), pltpu.VMEM((d,), F32),
                       pltpu.VMEM(gt2.shape, F32), pltpu.SemaphoreType.DMA, pltpu.SemaphoreType.DMA],
        name="peer_v_sc")
    def k(idx_hbm, w_hbm, x_hbm, gt_hbm, v_hbm, o_hbm, idx_v, w_v, rows_a, rows_b, acc_v, x_v,
          gt_v, sem_a, sem_b):
        wid = lax.axis_index("c") * nsub + lax.axis_index("s")
        pltpu.sync_copy(gt_hbm, gt_v)
        zero = jnp.zeros((SC_LANES,), F32)

        @pl.loop(0, tps)
        def _(t):
            loc = wid * tps + t
            tok = tok_base + loc
            pltpu.sync_copy(idx_hbm.at[tok], idx_v)
            pltpu.sync_copy(w_hbm.at[loc], w_v)
            pltpu.sync_copy(x_hbm.at[tok], x_v)
            for j in range(nblk):
                acc_v[pl.ds(j * SC_LANES, SC_LANES)] = zero

            def consume(ch, rows_v):
                @pl.loop(0, rows)
                def _(e):
                    wv = w_v[pl.ds((ch * rows + e) * SC_LANES, SC_LANES)]

                    @plsc.parallel_loop(0, d, SC_LANES, unroll=8)
                    def _(c):
                        sl = pl.ds(pl.multiple_of(c, SC_LANES), SC_LANES)
                        plsc.addupdate(acc_v.at[sl], wv * rows_v[e, sl])

            _sc_gather_chunks(v_hbm, idx_v, (rows_a, rows_b), (sem_a, sem_b), ne // rows, consume)
            b = jnp.right_shift(tok, shift)
            for j in range(nblk):
                sl = pl.ds(j * SC_LANES, SC_LANES)
                acc_v[sl] = x_v[sl] + gt_v[b, sl] * acc_v[sl]
            pltpu.sync_copy(acc_v, o_hbm.at[loc])

    return k(idx_tm, w16, x1, gt2, vtab)


def _perm_in_columns(width, kv_width):
    n_shift = 3 * width + DECAY_LORA + ICLR_LORA + GATE_LORA
    aq = n_shift
    akv = aq + width
    ga = akv + 2 * kv_width
    gb = ga + width
    return n_shift, np.concatenate([np.arange(0, n_shift), np.arange(aq, aq + width),
                                    np.arange(ga, ga + width), np.arange(gb, gb + width),
                                    np.arange(akv, akv + 2 * kv_width)])


def _layer(x, ada, norm1_w, norm2_w, w_in, shift_mu, w0, w_lora_up, a0, a_lora_up, g_lora_up,
           k_k, k_a, r_k, lnx_w, lnx_b, q_norm_w, k_norm_w, sinks, w_out, peer_w_q,
           peer_keys_1, peer_keys_2, peer_u, peer_v):
    bsz, seq, d = x.shape
    t = bsz * seq
    x2 = x.reshape(t, d)
    ada3 = ada.reshape(bsz, N_ADA, d)
    n_shift, perm = _perm_in_columns(d, ATTN_KV_HEADS * ATTN_HEAD)
    w_bf = w_in[:, perm].astype(BF16)
    sh, rest = _inproj(x2, ada3, norm1_w, w_bf, shift_mu, seq, n_shift)
    ya = _rwkv(sh, bsz, seq, d, (w0, w_lora_up, a0, a_lora_up, g_lora_up, k_k, k_a,
                                 r_k.reshape(-1), lnx_w, lnx_b))
    yb = _attn(rest, bsz, seq, d, q_norm_w, k_norm_w, sinks)
    x1, h2, sct = _outproj(ya, yb, rest, x2, ada3, w_out.astype(BF16), norm2_w,
                           peer_w_q.astype(BF16), peer_keys_1, peer_keys_2, seq)
    idx_t, idx_tm, gate = _topk(sct)
    sub = d // LANES
    gt2 = ada3[:, 5, :]
    t_tc = t - SC_TOKENS
    wts = _peer_u(idx_t, h2.reshape(t, sub, LANES), gate, _table_bf16(peer_u), t_tc)
    out_tc = _peer_v(idx_t, wts, x1.reshape(t, sub, LANES), gt2.reshape(bsz, sub, LANES),
                     _table_bf16(peer_v), seq)
    w16 = _sc_weights(_peer_u_sc(idx_tm, h2, peer_u, t_tc, SC_TOKENS), gate, t_tc)
    out_sc = _peer_v_sc(idx_tm, w16, x1, gt2, peer_v, seq, t_tc)
    return jnp.concatenate([out_tc.reshape(t_tc, d), out_sc], axis=0).reshape(bsz, seq, d)


def kernel(x, c, ada_w, ada_b, norm1_w, norm2_w, w_in, shift_mu, w0, w_lora_up, a0, a_lora_up, g_lora_up, k_k, k_a, r_k, lnx_w, lnx_b, q_norm_w, k_norm_w, sinks, w_out, peer_w_q, peer_keys_1, peer_keys_2, peer_u, peer_v):
    depth = ada_w.shape[0]
    for l in range(depth):
        ada = _ada(c, ada_w[l], ada_b[l])
        x = _layer(x, ada, norm1_w[l], norm2_w[l], w_in[l], shift_mu[l], w0[l], w_lora_up[l],
                   a0[l], a_lora_up[l], g_lora_up[l], k_k[l], k_a[l], r_k[l], lnx_w[l], lnx_b[l],
                   q_norm_w[l], k_norm_w[l], sinks[l], w_out[l], peer_w_q[l], peer_keys_1[l],
                   peer_keys_2[l], peer_u[l], peer_v[l])
    return x
```

```python
import functools

import numpy as np
import jax
import jax.numpy as jnp
from jax import lax
from jax.experimental import pallas as pl
from jax.experimental.pallas import tpu as pltpu
from jax.experimental.pallas import tpu_sc as plsc

F32 = jnp.float32
BF16 = jnp.bfloat16
I32 = jnp.int32
HIGHEST = lax.Precision.HIGHEST

RWKV_HEAD = 64
DECAY_LORA = 64
ICLR_LORA = 64
GATE_LORA = 128
LNX_EPS = 64e-5
ATTN_HEAD = 64
ATTN_Q_HEADS = 16
ATTN_KV_HEADS = 2
ATTN_GROUP = ATTN_Q_HEADS // ATTN_KV_HEADS
WINDOW = 128
BLOCK = 128
ATTN_SCALE = ATTN_HEAD ** -0.5
NEG_INF = -1e30
N_KEYS = 128
PEER_HEADS = 8
PEER_HALF = 128
PEER_TOPK = 16
NORM_EPS = 1e-6
N_ADA = 6

LANES = 128
SUBLANES = 8
VMEM_LIMIT = 56 * 1024 * 1024

RWKV_CHUNK = 64
ROW_TILE = 256
TOPK_TILE = 128
PEER_TILE = 128
PEER_GROUP = 8
SC_TOKENS = 18432
SC_CHUNK_U = 32
SC_CHUNK_V = 16
SC_LANES = 16
PACK_TILE = 512


def _nt(a, b):
    return lax.dot_general(a.astype(BF16), b.astype(BF16), (((1,), (1,)), ((), ())),
                           preferred_element_type=F32)


def _nn(a, b):
    return jnp.dot(a.astype(BF16), b.astype(BF16), preferred_element_type=F32)


def _dot_f32(a, b):
    return jnp.dot(a, b, precision=HIGHEST, preferred_element_type=F32)


def _bnt(a, b):
    return lax.dot_general(a.astype(BF16), b.astype(BF16), (((2,), (2,)), ((0,), (0,))),
                           preferred_element_type=F32)


def _bnn(a, b):
    return lax.dot_general(a.astype(BF16), b.astype(BF16), (((2,), (1,)), ((0,), (0,))),
                           preferred_element_type=F32)


def _btn(a, b):
    return lax.dot_general(a.astype(BF16), b.astype(BF16), (((1,), (1,)), ((0,), (0,))),
                           preferred_element_type=F32)


def _split2(x):
    hi = x.astype(BF16)
    lo = (x - hi.astype(F32)).astype(BF16)
    return hi, lo


def _split3(x):
    hi = x.astype(BF16)
    r1 = x - hi.astype(F32)
    mid = r1.astype(BF16)
    lo = (r1 - mid.astype(F32)).astype(BF16)
    return hi, mid, lo


def _dot_x3(a, b):
    ah, al = _split2(a)
    bh, bl = _split2(b)
    d = lambda p, q: jnp.dot(p, q, preferred_element_type=F32)
    return d(ah, bh) + (d(ah, bl) + d(al, bh))


def _dot_exact_lhs(a_bf, b):
    return sum(jnp.dot(a_bf, p, preferred_element_type=F32) for p in _split3(b))


def _ada_kernel(c_ref, w_ref, b_ref, o_ref):
    c = c_ref[...]
    cond = c * jax.nn.sigmoid(c)
    o_ref[...] = _dot_f32(cond, w_ref[...]) + b_ref[...]


def _ada(c, ada_w, ada_b):
    bsz, d = c.shape
    n = ada_w.shape[1]
    return pl.pallas_call(
        _ada_kernel,
        grid=(n // d,),
        in_specs=[pl.BlockSpec((bsz, d), lambda j: (0, 0)),
                  pl.BlockSpec((d, d), lambda j: (0, j)),
                  pl.BlockSpec((1, d), lambda j: (0, j))],
        out_specs=pl.BlockSpec((bsz, d), lambda j: (0, j)),
        out_shape=jax.ShapeDtypeStruct((bsz, n), F32),
        name="ada",
    )(c, ada_w, ada_b.reshape(1, n))


def _col_chunks(width, step):
    return [(c0, min(step, width - c0)) for c0 in range(0, width, step)]


def _inproj_kernel(x_ref, ada_ref, nw_ref, w_ref, mu_ref, sh_ref, rest_ref, carry_ref,
                   *, n_shift, n_rest, tiles_per_seq):
    i = pl.program_id(0)
    x = x_ref[...]
    tm = x.shape[0]
    ms = jnp.mean(x * x, axis=-1, keepdims=True)
    ada = ada_ref[0]
    h = x * lax.rsqrt(ms + NORM_EPS) * nw_ref[...] * (1.0 + ada[1:2]) + ada[0:1]
    hb = h.astype(BF16)
    row0 = lax.broadcasted_iota(I32, (tm, 1), 0) == 0
    seq_start = (i % tiles_per_seq) == 0
    for c0, cw in _col_chunks(n_shift, 1024):
        p = jnp.dot(hb, w_ref[:, c0:c0 + cw], preferred_element_type=F32)
        prev_last = jnp.where(seq_start, 0.0, carry_ref[:, c0:c0 + cw])
        carry_ref[:, c0:c0 + cw] = p[tm - 1:tm, :]
        prev = jnp.where(row0, prev_last, pltpu.roll(p, 1, axis=0))
        sh_ref[:, c0:c0 + cw] = p + (prev - p) * mu_ref[:, c0:c0 + cw]
    for c0, cw in _col_chunks(n_rest, 1024):
        rest_ref[:, c0:c0 + cw] = jnp.dot(hb, w_ref[:, n_shift + c0:n_shift + c0 + cw],
                                          preferred_element_type=F32)


def _inproj(x2, ada3, norm_w, w_bf, mu, seq, n_shift):
    t, d = x2.shape
    n_rest = w_bf.shape[1] - n_shift
    tm = ROW_TILE
    tiles_per_seq = seq // tm
    kern = functools.partial(_inproj_kernel, n_shift=n_shift, n_rest=n_rest,
                             tiles_per_seq=tiles_per_seq)
    return pl.pallas_call(
        kern,
        grid=(t // tm,),
        in_specs=[pl.BlockSpec((tm, d), lambda i: (i, 0)),
                  pl.BlockSpec((1, N_ADA, d), lambda i: (i // tiles_per_seq, 0, 0)),
                  pl.BlockSpec((1, d), lambda i: (0, 0)),
                  pl.BlockSpec(w_bf.shape, lambda i: (0, 0), pipeline_mode=pl.Buffered(1)),
                  pl.BlockSpec((1, n_shift), lambda i: (0, 0))],
        out_specs=[pl.BlockSpec((tm, n_shift), lambda i: (i, 0)),
                   pl.BlockSpec((tm, n_rest), lambda i: (i, 0))],
        out_shape=[jax.ShapeDtypeStruct((t, n_shift), F32),
                   jax.ShapeDtypeStruct((t, n_rest), F32)],
        scratch_shapes=[pltpu.VMEM((1, n_shift), F32)],
        compiler_params=pltpu.CompilerParams(dimension_semantics=("arbitrary",),
                                             vmem_limit_bytes=VMEM_LIMIT),
        name="inproj",
    )(x2, ada3, norm_w.reshape(1, d), w_bf, mu.reshape(1, n_shift))


def _softplus(z):
    return jnp.maximum(z, 0.0) + jnp.log1p(jnp.exp(-jnp.abs(z)))


def _rwkv_heads(r, k, v, kkraw, a, cum, lw, s0, masks):
    strict, incl, diag, levels = masks
    c = r.shape[1]
    nrm = jnp.sqrt(jnp.sum(kkraw * kkraw, axis=-1, keepdims=True))
    kk = kkraw / jnp.maximum(nrm, 1e-12)
    w_in = jnp.exp(cum)
    w_ex = jnp.exp(cum - lw)
    w_inv = jnp.exp(-cum)
    w_end = jnp.exp(cum[:, c - 1:c, :] - cum)
    bb = kk * a
    rt = r * w_in
    at = -kk * w_ex
    bt = bb * w_inv
    kt = k * w_inv
    ab = _bnt(at, bt)
    ak = _bnt(at, kt)
    rb = _bnt(rt, bt)
    rk = _bnt(rt, kt)
    lab = jnp.where(strict, ab, 0.0)
    inv = jnp.where(levels[0], lab, 0.0) + jnp.where(diag, 1.0, 0.0)
    for lvl in levels[1:]:
        inv = inv + _bnn(_bnn(inv, jnp.where(lvl, lab, 0.0)), inv)
    rhs = _bnt(at, s0) + _bnn(jnp.where(strict, ak, 0.0), v)
    u = _bnn(inv, rhs)
    y = _bnt(rt, s0) + _bnn(jnp.where(incl, rb, 0.0), u) + _bnn(jnp.where(incl, rk, 0.0), v)
    s_new = s0 * w_in[:, c - 1:c, :] + _btn(u, bb * w_end) + _btn(v, k * w_end)
    return y, s_new


def _rwkv_kernel(pr_ref, pk_ref, pv_ref, tail_ref, w0_ref, wup_ref, a0_ref, aup_ref, gup_ref,
                 kk_ref, ka_ref, rk_ref, lnw_ref, lnb_ref, o_ref, state_ref):
    j = pl.program_id(1)
    c = pr_ref.shape[0]
    n_heads = pr_ref.shape[1] // RWKV_HEAD

    @pl.when(j == 0)
    def _():
        state_ref[...] = jnp.zeros_like(state_ref)

    tail = tail_ref[...]
    pw = tail[:, :DECAY_LORA]
    pa = tail[:, DECAY_LORA:DECAY_LORA + ICLR_LORA]
    pg = tail[:, DECAY_LORA + ICLR_LORA:]
    pk = pk_ref[...]
    w_log = -_softplus(-(w0_ref[...] + _dot_x3(jnp.tanh(pw), wup_ref[...]))) - 0.5
    lw = -jnp.exp(w_log)
    a = jax.nn.sigmoid(a0_ref[...] + _dot_x3(pa, aup_ref[...]))
    row = lax.broadcasted_iota(I32, (c, c), 0)
    col = lax.broadcasted_iota(I32, (c, c), 1)
    strict = row > col
    incl = row >= col
    diag = row == col
    cum = _dot_exact_lhs(incl.astype(BF16), lw)
    k = pk * (1.0 + (a - 1.0) * ka_ref[...])
    kkraw = pk * kk_ref[...]
    g = _dot_x3(jax.nn.sigmoid(pg), gup_ref[...])

    levels = []
    shift = 0
    while (1 << shift) < c:
        levels.append((jnp.right_shift(row, shift + 1) == jnp.right_shift(col, shift + 1))
                      & ((jnp.right_shift(row, shift) & 1) == 1)
                      & ((jnp.right_shift(col, shift) & 1) == 0))
        shift += 1
    masks = (strict, incl, diag, levels)

    def heads(x):
        return jnp.stack([x[:, h * RWKV_HEAD:(h + 1) * RWKV_HEAD] for h in range(n_heads)], axis=0)

    r3, k3, v3 = heads(pr_ref[...]), heads(k), heads(pv_ref[...])
    y, s_new = _rwkv_heads(r3, k3, v3, heads(kkraw), heads(a), heads(cum), heads(lw),
                           state_ref[...], masks)
    state_ref[...] = s_new
    mu = jnp.mean(y, axis=-1, keepdims=True)
    var = jnp.mean(jnp.square(y - mu), axis=-1, keepdims=True)
    yn = (y - mu) * lax.rsqrt(var + LNX_EPS) * lnw_ref[...] + lnb_ref[...]
    bonus = jnp.sum(r3 * k3 * rk_ref[...], axis=-1, keepdims=True) * v3
    out = (yn + bonus) * heads(g)
    for h in range(n_heads):
        o_ref[:, h * RWKV_HEAD:(h + 1) * RWKV_HEAD] = out[h]


def _rwkv(sh, bsz, seq, width, params):
    (w0, w_up, a0, a_up, g_up, k_k, k_a, r_k, lnx_w, lnx_b) = params
    c = RWKV_CHUNK
    nc = seq // c
    heads = width // RWKV_HEAD
    tail_w = DECAY_LORA + ICLR_LORA + GATE_LORA
    tail_blk = (3 * width) // tail_w
    row = lambda b, j: b * nc + j
    vec = lambda a: a.reshape(1, width)
    per_head = lambda a: a.reshape(heads, 1, RWKV_HEAD)
    const = lambda shape: pl.BlockSpec(shape, lambda b, j: (0,) * len(shape))
    return pl.pallas_call(
        _rwkv_kernel,
        grid=(bsz, nc),
        in_specs=[pl.BlockSpec((c, width), lambda b, j: (row(b, j), 0)),
                  pl.BlockSpec((c, width), lambda b, j: (row(b, j), 1)),
                  pl.BlockSpec((c, width), lambda b, j: (row(b, j), 2)),
                  pl.BlockSpec((c, tail_w), lambda b, j: (row(b, j), tail_blk)),
                  const((1, width)), const((DECAY_LORA, width)),
                  const((1, width)), const((ICLR_LORA, width)), const((GATE_LORA, width)),
                  const((1, width)), const((1, width)),
                  const((heads, 1, RWKV_HEAD)), const((heads, 1, RWKV_HEAD)),
                  const((heads, 1, RWKV_HEAD))],
        out_specs=pl.BlockSpec((c, width), lambda b, j: (row(b, j), 0)),
        out_shape=jax.ShapeDtypeStruct((bsz * seq, width), F32),
        scratch_shapes=[pltpu.VMEM((heads, RWKV_HEAD, RWKV_HEAD), F32)],
        compiler_params=pltpu.CompilerParams(dimension_semantics=("arbitrary", "arbitrary"),
                                             vmem_limit_bytes=VMEM_LIMIT),
        name="rwkv",
    )(sh, sh, sh, sh, vec(w0), w_up, vec(a0), a_up, g_up, vec(k_k), vec(k_a), per_head(r_k),
      per_head(lnx_w), per_head(lnx_b))


def _head_rms(x, w):
    return x * lax.rsqrt(jnp.mean(x * x, axis=-1, keepdims=True) + NORM_EPS) * w


def _attn_kernel(sink_ref, q_ref, kvc_ref, kvp_ref, qw_ref, kw_ref, o_ref):
    blk = pl.program_id(1)
    q = q_ref[...]
    kvc = kvc_ref[...]
    kvp = kvp_ref[...]
    kv_w = ATTN_KV_HEADS * ATTN_HEAD
    qi = lax.broadcasted_iota(I32, (BLOCK, 2 * BLOCK), 0)
    kj = lax.broadcasted_iota(I32, (BLOCK, 2 * BLOCK), 1)
    rel = kj - qi
    valid = (rel >= BLOCK - WINDOW + 1) & (rel <= BLOCK) & (blk * BLOCK - BLOCK + kj >= 0)
    for g in range(ATTN_KV_HEADS):
        ks = slice(g * ATTN_HEAD, (g + 1) * ATTN_HEAD)
        vs = slice(kv_w + g * ATTN_HEAD, kv_w + (g + 1) * ATTN_HEAD)
        kband = _head_rms(jnp.concatenate([kvp[:, ks], kvc[:, ks]], axis=0), kw_ref[...])
        vband = jnp.concatenate([kvp[:, vs], kvc[:, vs]], axis=0)
        for n in range(ATTN_GROUP):
            hq = g * ATTN_GROUP + n
            qh = _head_rms(q[:, hq * ATTN_HEAD:(hq + 1) * ATTN_HEAD], qw_ref[...])
            s = jnp.where(valid, _nt(qh, kband) * ATTN_SCALE, NEG_INF)
            sink = sink_ref[hq]
            m = jnp.maximum(jnp.max(s, axis=-1, keepdims=True), sink)
            p = jnp.exp(s - m)
            denom = jnp.sum(p, axis=-1, keepdims=True) + jnp.exp(sink - m)
            o_ref[:, hq * ATTN_HEAD:(hq + 1) * ATTN_HEAD] = _nn(p, vband) / denom


def _attn(rest, bsz, seq, width, q_norm_w, k_norm_w, sinks):
    nb = seq // BLOCK
    kv_w = 2 * ATTN_KV_HEADS * ATTN_HEAD
    kv_blk = (3 * width) // kv_w
    return pl.pallas_call(
        _attn_kernel,
        grid=(bsz, nb),
        in_specs=[pl.BlockSpec(memory_space=pltpu.SMEM),
                  pl.BlockSpec((BLOCK, width), lambda b, j: (b * nb + j, 0)),
                  pl.BlockSpec((BLOCK, kv_w), lambda b, j: (b * nb + j, kv_blk)),
                  pl.BlockSpec((BLOCK, kv_w), lambda b, j: (b * nb + jnp.maximum(j - 1, 0), kv_blk)),
                  pl.BlockSpec((1, ATTN_HEAD), lambda b, j: (0, 0)),
                  pl.BlockSpec((1, ATTN_HEAD), lambda b, j: (0, 0))],
        out_specs=pl.BlockSpec((BLOCK, width), lambda b, j: (b * nb + j, 0)),
        out_shape=jax.ShapeDtypeStruct((bsz * seq, width), F32),
        compiler_params=pltpu.CompilerParams(dimension_semantics=("arbitrary", "arbitrary"),
                                             vmem_limit_bytes=VMEM_LIMIT),
        name="attn",
    )(sinks, rest, rest, rest, q_norm_w.reshape(1, ATTN_HEAD), k_norm_w.reshape(1, ATTN_HEAD))


def _outproj_kernel(ya_ref, yb_ref, ga_ref, gb_ref, x_ref, ada_ref, wo_ref, nw_ref, wq_ref,
                    k1_ref, k2_ref, x1_ref, h2_ref, sct_ref):
    ada = ada_ref[0]
    mixed = jax.nn.sigmoid(ga_ref[...]) * ya_ref[...] + jax.nn.sigmoid(gb_ref[...]) * yb_ref[...]
    x1 = x_ref[...] + ada[2:3] * _nn(mixed, wo_ref[...])
    x1_ref[...] = x1
    ms = jnp.mean(x1 * x1, axis=-1, keepdims=True)
    h2 = x1 * lax.rsqrt(ms + NORM_EPS) * nw_ref[...] * (1.0 + ada[4:5]) + ada[3:4]
    h2_ref[...] = h2
    q = _nn(h2, wq_ref[...])
    for ch in range(q.shape[1] // PEER_HALF):
        keys = k1_ref if ch % 2 == 0 else k2_ref
        cs = slice(ch * PEER_HALF, (ch + 1) * PEER_HALF)
        sct_ref[cs, :] = _nt(keys[...], q[:, cs])


def _outproj(ya, yb, rest, x2, ada3, w_out_bf, norm_w, wq_bf, keys_1, keys_2, seq):
    t, d = x2.shape
    tm = ROW_TILE
    tiles_per_seq = seq // tm
    qd = wq_bf.shape[1]
    rowblk = lambda c: pl.BlockSpec((tm, d), lambda i: (i, c))
    const = lambda shape: pl.BlockSpec(shape, lambda i: (0, 0))
    return pl.pallas_call(
        _outproj_kernel,
        grid=(t // tm,),
        in_specs=[rowblk(0), rowblk(0), rowblk(1), rowblk(2), rowblk(0),
                  pl.BlockSpec((1, N_ADA, d), lambda i: (i // tiles_per_seq, 0, 0)),
                  const((d, d)), const((1, d)), const((d, qd)),
                  const(keys_1.shape), const(keys_2.shape)],
        out_specs=[rowblk(0), rowblk(0), pl.BlockSpec((qd, tm), lambda i: (0, i))],
        out_shape=[jax.ShapeDtypeStruct((t, d), F32), jax.ShapeDtypeStruct((t, d), F32),
                   jax.ShapeDtypeStruct((qd, t), F32)],
        compiler_params=pltpu.CompilerParams(dimension_semantics=("arbitrary",),
                                             vmem_limit_bytes=VMEM_LIMIT),
        name="outproj",
    )(ya, yb, rest, rest, x2, ada3, w_out_bf, norm_w.reshape(1, d), wq_bf, keys_1, keys_2)


def _top16_rows(s, payload=None):
    n = s.shape[0]
    iota = lax.broadcasted_iota(I32, s.shape, 0).astype(F32)
    vals, picks = [], []
    for _ in range(PEER_TOPK):
        m = jnp.max(s, axis=0, keepdims=True)
        i = jnp.min(jnp.where(s == m, iota, float(n)), axis=0, keepdims=True)
        hit = iota == i
        s = jnp.where(hit, -jnp.inf, s)
        vals.append(m)
        picks.append(i if payload is None else
                     jnp.max(jnp.where(hit, payload, -1.0), axis=0, keepdims=True))
    return jnp.concatenate(vals, axis=0), jnp.concatenate(picks, axis=0)


def _topk_kernel(sct_ref, idx_ref, idx_tm_ref, gate_ref):
    keep = [PEER_TOPK // (i + 1) for i in range(PEER_TOPK)]
    pad = (-sum(keep)) % SUBLANES
    idx_all, gate_all = [], []
    for h in range(PEER_HEADS):
        base = h * 2 * N_KEYS
        v1, i1 = _top16_rows(sct_ref[base:base + N_KEYS, :])
        v2, i2 = _top16_rows(sct_ref[base + N_KEYS:base + 2 * N_KEYS, :])
        cand = [v1[i:i + 1, :] + v2[0:keep[i], :] for i in range(PEER_TOPK)]
        cidx = [i1[i:i + 1, :] * float(N_KEYS) + i2[0:keep[i], :] for i in range(PEER_TOPK)]
        if pad:
            cand.append(jnp.full((pad, v1.shape[1]), -jnp.inf, F32))
            cidx.append(jnp.zeros((pad, v1.shape[1]), F32))
        sc, idx = _top16_rows(jnp.concatenate(cand, axis=0), jnp.concatenate(cidx, axis=0))
        idx_all.append(idx.astype(I32))
        ex = jnp.exp(sc - sc[0:1, :])
        gate_all.append(ex / jnp.sum(ex, axis=0, keepdims=True))
    idx = jnp.concatenate(idx_all, axis=0)
    idx_ref[...] = idx
    idx_tm_ref[...] = idx.T
    gate_ref[...] = jnp.concatenate(gate_all, axis=0).T


def _topk(sct):
    qd, t = sct.shape
    tk = TOPK_TILE
    ne = PEER_HEADS * PEER_TOPK
    return pl.pallas_call(
        _topk_kernel,
        grid=(t // tk,),
        in_specs=[pl.BlockSpec((qd, tk), lambda i: (0, i))],
        out_specs=[pl.BlockSpec((ne, tk), lambda i: (0, i)),
                   pl.BlockSpec((tk, ne), lambda i: (i, 0)),
                   pl.BlockSpec((tk, ne), lambda i: (i, 0))],
        out_shape=[jax.ShapeDtypeStruct((ne, t), I32), jax.ShapeDtypeStruct((t, ne), I32),
                   jax.ShapeDtypeStruct((t, ne), F32)],
        compiler_params=pltpu.CompilerParams(dimension_semantics=("arbitrary",),
                                             vmem_limit_bytes=VMEM_LIMIT),
        name="topk",
    )(sct)


def _cast_kernel(x_ref, o_ref):
    o_ref[...] = x_ref[...].astype(BF16)


def _table_bf16(tab):
    n, d = tab.shape
    out = pl.pallas_call(
        _cast_kernel,
        grid=(n // PACK_TILE,),
        in_specs=[pl.BlockSpec((PACK_TILE, d), lambda i: (i, 0))],
        out_specs=pl.BlockSpec((PACK_TILE, d), lambda i: (i, 0)),
        out_shape=jax.ShapeDtypeStruct((n, d), BF16),
        name="table_cast",
    )(tab)
    return out.reshape(n, d // LANES, LANES)


def _gelu_exact(x):
    return 0.5 * x * (1.0 + lax.erf(x * (2.0 ** -0.5)))


def _group_mask(ne):
    sub = lax.broadcasted_iota(I32, (SUBLANES, ne * SUBLANES), 0)
    col = lax.broadcasted_iota(I32, (SUBLANES, ne * SUBLANES), 1)
    return sub == (col & (SUBLANES - 1))


def _gather_rows(idx_ref, tab_ref, t, rows_ref):
    picks = idx_ref.at[:, pl.ds(t, 1)]
    for e in range(idx_ref.shape[0]):
        rows_ref[e * SUBLANES:(e + 1) * SUBLANES, :] = tab_ref[picks[e, 0]]


def _peer_u_kernel(idx_ref, x_ref, gate_ref, tab_ref, wts_ref, *scratch):
    rows, d_s = scratch[:-1], scratch[-1]
    tb = x_ref.shape[0]
    ne = gate_ref.shape[1]
    mask = _group_mask(ne)

    def reduce(t, rows_ref):
        hi, lo = _split2(x_ref[t])
        lhs = jnp.concatenate([hi, lo], axis=0)
        z = lax.dot_general(lhs, rows_ref[...], (((1,), (1,)), ((), ())),
                            preferred_element_type=F32)
        zz = z[:SUBLANES] + z[SUBLANES:]
        d_s[pl.ds(t, 1), :] = jnp.sum(jnp.where(mask, zz, 0.0), axis=0, keepdims=True)

    def group(i, carry):
        for q, rows_ref in enumerate(rows):
            _gather_rows(idx_ref, tab_ref, len(rows) * i + q, rows_ref)
        for q, rows_ref in enumerate(rows):
            reduce(len(rows) * i + q, rows_ref)
        return carry

    lax.fori_loop(0, tb // len(rows), group, 0)
    r = lax.broadcasted_iota(I32, (ne * SUBLANES, ne), 0)
    c = lax.broadcasted_iota(I32, (ne * SUBLANES, ne), 1)
    fold = (jnp.right_shift(r, 3) == c).astype(BF16)
    dots = sum(jnp.dot(p, fold, preferred_element_type=F32) for p in _split3(d_s[...]))
    wts_ref[...] = gate_ref[...] * _gelu_exact(dots)


def _peer_u(idx_t, h3, gate, tab3, t):
    ne = gate.shape[1]
    tb = PEER_TILE
    return pl.pallas_call(
        _peer_u_kernel,
        grid=(t // tb,),
        in_specs=[pl.BlockSpec((ne, tb), lambda i: (0, i), memory_space=pltpu.SMEM),
                  pl.BlockSpec((tb, SUBLANES, LANES), lambda i: (i, 0, 0)),
                  pl.BlockSpec((tb, ne), lambda i: (i, 0)),
                  pl.BlockSpec(tab3.shape, lambda i: (0, 0, 0), pipeline_mode=pl.Buffered(1))],
        out_specs=pl.BlockSpec((tb, ne), lambda i: (i, 0)),
        out_shape=jax.ShapeDtypeStruct((t, ne), F32),
        scratch_shapes=[pltpu.VMEM((ne * SUBLANES, LANES), BF16)] * PEER_GROUP
                       + [pltpu.VMEM((tb, ne * SUBLANES), F32)],
        compiler_params=pltpu.CompilerParams(dimension_semantics=("arbitrary",),
                                             vmem_limit_bytes=VMEM_LIMIT),
        name="peer_u",
    )(idx_t, h3, gate, tab3)


def _peer_v_kernel(idx_ref, wts_ref, x1_ref, gt_ref, tab_ref, o_ref, *scratch):
    rows, w8_s = scratch[:-1], scratch[-1]
    tb = x1_ref.shape[0]
    ne = wts_ref.shape[1]
    mask = _group_mask(ne)
    gt = gt_ref[0]
    r = lax.broadcasted_iota(I32, (ne, ne * SUBLANES), 0)
    c = lax.broadcasted_iota(I32, (ne, ne * SUBLANES), 1)
    spread = (r == jnp.right_shift(c, 3)).astype(BF16)
    w8_s[...] = sum(jnp.dot(p, spread, preferred_element_type=F32) for p in _split3(wts_ref[...]))

    def combine(t, rows_ref):
        wm = jnp.where(mask, w8_s[pl.ds(t, 1), :], 0.0)
        hi, lo = _split2(wm)
        res = jnp.dot(jnp.concatenate([hi, lo], axis=0), rows_ref[...],
                      preferred_element_type=F32)
        o_ref[t] = x1_ref[t] + gt * (res[:SUBLANES] + res[SUBLANES:])

    def group(i, carry):
        for q, rows_ref in enumerate(rows):
            _gather_rows(idx_ref, tab_ref, len(rows) * i + q, rows_ref)
        for q, rows_ref in enumerate(rows):
            combine(len(rows) * i + q, rows_ref)
        return carry

    lax.fori_loop(0, tb // len(rows), group, 0)


def _peer_v(idx_t, wts, x13, gt3, tab3, seq):
    t, ne = wts.shape
    tb = PEER_TILE
    tiles_per_seq = seq // tb
    return pl.pallas_call(
        _peer_v_kernel,
        grid=(t // tb,),
        in_specs=[pl.BlockSpec((ne, tb), lambda i: (0, i), memory_space=pltpu.SMEM),
                  pl.BlockSpec((tb, ne), lambda i: (i, 0)),
                  pl.BlockSpec((tb, SUBLANES, LANES), lambda i: (i, 0, 0)),
                  pl.BlockSpec((1, SUBLANES, LANES), lambda i: (i // tiles_per_seq, 0, 0)),
                  pl.BlockSpec(tab3.shape, lambda i: (0, 0, 0), pipeline_mode=pl.Buffered(1))],
        out_specs=pl.BlockSpec((tb, SUBLANES, LANES), lambda i: (i, 0, 0)),
        out_shape=jax.ShapeDtypeStruct((t, SUBLANES, LANES), F32),
        scratch_shapes=[pltpu.VMEM((ne * SUBLANES, LANES), BF16)] * PEER_GROUP
                       + [pltpu.VMEM((tb, ne * SUBLANES), F32)],
        compiler_params=pltpu.CompilerParams(dimension_semantics=("arbitrary",),
                                             vmem_limit_bytes=VMEM_LIMIT),
        name="peer_v",
    )(idx_t, wts, x13, gt3, tab3)


def _sc_mesh():
    info = plsc.get_sparse_core_info()
    mesh = plsc.VectorSubcoreMesh(core_axis_name="c", subcore_axis_name="s")
    return mesh, info.num_cores * info.num_subcores, info.num_subcores


def _sc_gather_chunks(tab_hbm, idx_v, bufs, sems, n_chunks, consume):
    rows = bufs[0].shape[0]

    def start(ch):
        return pltpu.async_copy(tab_hbm.at[idx_v.at[pl.ds(ch * rows, rows)]],
                                bufs[ch % 2], sems[ch % 2])

    copy = start(0)
    for ch in range(n_chunks):
        nxt = start(ch + 1) if ch + 1 < n_chunks else None
        copy.wait()
        consume(ch, bufs[ch % 2])
        copy = nxt


def _peer_u_sc(idx_tm, h2, utab, tok_base, tsc):
    ne = idx_tm.shape[1]
    d = utab.shape[1]
    mesh, nw, nsub = _sc_mesh()
    tps = tsc // nw
    rows = SC_CHUNK_U
    assert tsc % nw == 0 and ne % rows == 0

    @functools.partial(
        pl.kernel, mesh=mesh, out_type=jax.ShapeDtypeStruct((tsc, ne * SC_LANES), F32),
        scratch_types=[pltpu.VMEM((ne,), I32), pltpu.VMEM((d,), F32),
                       pltpu.VMEM((rows, d), F32), pltpu.VMEM((rows, d), F32),
                       pltpu.VMEM((ne * SC_LANES,), F32),
                       pltpu.SemaphoreType.DMA, pltpu.SemaphoreType.DMA],
        name="peer_u_sc")
    def k(idx_hbm, x_hbm, u_hbm, o_hbm, idx_v, x_v, rows_a, rows_b, dp_v, sem_a, sem_b):
        wid = lax.axis_index("c") * nsub + lax.axis_index("s")
        zero = jnp.zeros((SC_LANES,), F32)

        @pl.loop(0, tps)
        def _(t):
            loc = wid * tps + t
            pltpu.sync_copy(idx_hbm.at[tok_base + loc], idx_v)
            pltpu.sync_copy(x_hbm.at[tok_base + loc], x_v)
            for e in range(ne):
                dp_v[pl.ds(e * SC_LANES, SC_LANES)] = zero

            def consume(ch, rows_v):
                @pl.loop(0, d, step=SC_LANES)
                def _(c):
                    sl = pl.ds(pl.multiple_of(c, SC_LANES), SC_LANES)
                    xv = x_v[sl]

                    @plsc.parallel_loop(0, rows, 1, unroll=8)
                    def _(r):
                        dst = pl.ds(pl.multiple_of((ch * rows + r) * SC_LANES, SC_LANES), SC_LANES)
                        plsc.addupdate(dp_v.at[dst], rows_v[r, sl] * xv)

            _sc_gather_chunks(u_hbm, idx_v, (rows_a, rows_b), (sem_a, sem_b), ne // rows, consume)
            pltpu.sync_copy(dp_v, o_hbm.at[loc])

    return k(idx_tm, h2, utab)


def _sc_weights_kernel(dp_ref, gate_ref, wts_tc_ref, o_ref, wts_tc_out_ref):
    del wts_tc_ref, wts_tc_out_ref
    ne = gate_ref.shape[1]
    r = lax.broadcasted_iota(I32, (ne * SC_LANES, ne), 0)
    c = lax.broadcasted_iota(I32, (ne * SC_LANES, ne), 1)
    fold = (jnp.right_shift(r, 4) == c).astype(BF16)
    dots = sum(jnp.dot(p, fold, preferred_element_type=F32) for p in _split3(dp_ref[...]))
    wts = gate_ref[...] * _gelu_exact(dots)
    r = lax.broadcasted_iota(I32, (ne, ne * SC_LANES), 0)
    c = lax.broadcasted_iota(I32, (ne, ne * SC_LANES), 1)
    spread = (r == jnp.right_shift(c, 4)).astype(BF16)
    o_ref[...] = sum(jnp.dot(p, spread, preferred_element_type=F32) for p in _split3(wts))


def _sc_weights(dparts, gate, wts_tc, tok_base):
    tsc, w = dparts.shape
    ne = gate.shape[1]
    tb = ROW_TILE
    blk0 = tok_base // tb
    return pl.pallas_call(
        _sc_weights_kernel,
        grid=(tsc // tb,),
        in_specs=[pl.BlockSpec((tb, w), lambda i: (i, 0)),
                  pl.BlockSpec((tb, ne), lambda i: (i + blk0, 0)),
                  pl.BlockSpec(memory_space=pl.ANY)],
        out_specs=[pl.BlockSpec((tb, w), lambda i: (i, 0)),
                   pl.BlockSpec(memory_space=pl.ANY)],
        out_shape=[jax.ShapeDtypeStruct((tsc, w), F32),
                   jax.ShapeDtypeStruct(wts_tc.shape, wts_tc.dtype)],
        input_output_aliases={2: 1},
        name="sc_weights",
    )(dparts, gate, wts_tc)


def _peer_v_sc(idx_tm, w16, x1, gt2, vtab, seq, tok_base):
    tsc = w16.shape[0]
    ne = idx_tm.shape[1]
    d = vtab.shape[1]
    mesh, nw, nsub = _sc_mesh()
    tps = tsc // nw
    nblk = d // SC_LANES
    shift = seq.bit_length() - 1
    rows = SC_CHUNK_V
    assert (1 << shift) == seq and tsc % nw == 0 and ne % rows == 0

    @functools.partial(
        pl.kernel, mesh=mesh, out_type=jax.ShapeDtypeStruct((tsc, d), F32),
        scratch_types=[pltpu.VMEM((ne,), I32), pltpu.VMEM((ne * SC_LANES,), F32),
                       pltpu.VMEM((rows, d), F32), pltpu.VMEM((rows, d), F32),
                       pltpu.VMEM((d,), F32), pltpu.VMEM((d,), F32),
                       pltpu.VMEM(gt2.shape, F32), pltpu.SemaphoreType.DMA, pltpu.SemaphoreType.DMA],
        name="peer_v_sc")
    def k(idx_hbm, w_hbm, x_hbm, gt_hbm, v_hbm, o_hbm, idx_v, w_v, rows_a, rows_b, acc_v, x_v,
          gt_v, sem_a, sem_b):
        wid = lax.axis_index("c") * nsub + lax.axis_index("s")
        pltpu.sync_copy(gt_hbm, gt_v)
        zero = jnp.zeros((SC_LANES,), F32)

        @pl.loop(0, tps)
        def _(t):
            loc = wid * tps + t
            tok = tok_base + loc
            pltpu.sync_copy(idx_hbm.at[tok], idx_v)
            pltpu.sync_copy(w_hbm.at[loc], w_v)
            pltpu.sync_copy(x_hbm.at[tok], x_v)
            for j in range(nblk):
                acc_v[pl.ds(j * SC_LANES, SC_LANES)] = zero

            def consume(ch, rows_v):
                @pl.loop(0, rows)
                def _(e):
                    wv = w_v[pl.ds((ch * rows + e) * SC_LANES, SC_LANES)]

                    @plsc.parallel_loop(0, d, SC_LANES, unroll=8)
                    def _(c):
                        sl = pl.ds(pl.multiple_of(c, SC_LANES), SC_LANES)
                        plsc.addupdate(acc_v.at[sl], wv * rows_v[e, sl])

            _sc_gather_chunks(v_hbm, idx_v, (rows_a, rows_b), (sem_a, sem_b), ne // rows, consume)
            b = jnp.right_shift(tok, shift)
            for j in range(nblk):
                sl = pl.ds(j * SC_LANES, SC_LANES)
                acc_v[sl] = x_v[sl] + gt_v[b, sl] * acc_v[sl]
            pltpu.sync_copy(acc_v, o_hbm.at[loc])

    return k(idx_tm, w16, x1, gt2, vtab)


def _perm_in_columns(width, kv_width):
    n_shift = 3 * width + DECAY_LORA + ICLR_LORA + GATE_LORA
    aq = n_shift
    akv = aq + width
    ga = akv + 2 * kv_width
    gb = ga + width
    return n_shift, np.concatenate([np.arange(0, n_shift), np.arange(aq, aq + width),
                                    np.arange(ga, ga + width), np.arange(gb, gb + width),
                                    np.arange(akv, akv + 2 * kv_width)])


def _layer(x, ada, norm1_w, norm2_w, w_in, shift_mu, w0, w_lora_up, a0, a_lora_up, g_lora_up,
           k_k, k_a, r_k, lnx_w, lnx_b, q_norm_w, k_norm_w, sinks, w_out, peer_w_q,
           peer_keys_1, peer_keys_2, peer_u, peer_v):
    bsz, seq, d = x.shape
    t = bsz * seq
    x2 = x.reshape(t, d)
    ada3 = ada.reshape(bsz, N_ADA, d)
    n_shift, perm = _perm_in_columns(d, ATTN_KV_HEADS * ATTN_HEAD)
    w_bf = w_in[:, perm].astype(BF16)
    sh, rest = _inproj(x2, ada3, norm1_w, w_bf, shift_mu, seq, n_shift)
    ya = _rwkv(sh, bsz, seq, d, (w0, w_lora_up, a0, a_lora_up, g_lora_up, k_k, k_a,
                                 r_k.reshape(-1), lnx_w, lnx_b))
    yb = _attn(rest, bsz, seq, d, q_norm_w, k_norm_w, sinks)
    x1, h2, sct = _outproj(ya, yb, rest, x2, ada3, w_out.astype(BF16), norm2_w,
                           peer_w_q.astype(BF16), peer_keys_1, peer_keys_2, seq)
    idx_t, idx_tm, gate = _topk(sct)
    sub = d // LANES
    gt2 = ada3[:, 5, :]
    t_tc = t - SC_TOKENS
    wts = _peer_u(idx_t, h2.reshape(t, sub, LANES), gate, _table_bf16(peer_u), t_tc)
    dparts = _peer_u_sc(idx_tm, h2, peer_u, t_tc, SC_TOKENS)
    w16, wts = _sc_weights(dparts, gate, wts, t_tc)
    out_tc = _peer_v(idx_t, wts, x1.reshape(t, sub, LANES), gt2.reshape(bsz, sub, LANES),
                     _table_bf16(peer_v), seq)
    out_sc = _peer_v_sc(idx_tm, w16, x1, gt2, peer_v, seq, t_tc)
    return jnp.concatenate([out_tc.reshape(t_tc, d), out_sc], axis=0).reshape(bsz, seq, d)


def kernel(x, c, ada_w, ada_b, norm1_w, norm2_w, w_in, shift_mu, w0, w_lora_up, a0, a_lora_up, g_lora_up, k_k, k_a, r_k, lnx_w, lnx_b, q_norm_w, k_norm_w, sinks, w_out, peer_w_q, peer_keys_1, peer_keys_2, peer_u, peer_v):
    depth = ada_w.shape[0]
    for l in range(depth):
        ada = _ada(c, ada_w[l], ada_b[l])
        x = _layer(x, ada, norm1_w[l], norm2_w[l], w_in[l], shift_mu[l], w0[l], w_lora_up[l],
                   a0[l], a_lora_up[l], g_lora_up[l], k_k[l], k_a[l], r_k[l], lnx_w[l], lnx_b[l],
                   q_norm_w[l], k_norm_w[l], sinks[l], w_out[l], peer_w_q[l], peer_keys_1[l],
                   peer_keys_2[l], peer_u[l], peer_v[l])
    return x
```

```python
import functools

import numpy as np
import jax
import jax.numpy as jnp
from jax import lax
from jax.experimental import pallas as pl
from jax.experimental.pallas import tpu as pltpu
from jax.experimental.pallas import tpu_sc as plsc

F32 = jnp.float32
BF16 = jnp.bfloat16
I32 = jnp.int32
HIGHEST = lax.Precision.HIGHEST

RWKV_HEAD = 64
DECAY_LORA = 64
ICLR_LORA = 64
GATE_LORA = 128
LNX_EPS = 64e-5
ATTN_HEAD = 64
ATTN_Q_HEADS = 16
ATTN_KV_HEADS = 2
ATTN_GROUP = ATTN_Q_HEADS // ATTN_KV_HEADS
WINDOW = 128
BLOCK = 128
ATTN_SCALE = ATTN_HEAD ** -0.5
NEG_INF = -1e30
N_KEYS = 128
PEER_HEADS = 8
PEER_HALF = 128
PEER_TOPK = 16
NORM_EPS = 1e-6
N_ADA = 6

LANES = 128
SUBLANES = 8
VMEM_LIMIT = 56 * 1024 * 1024

RWKV_CHUNK = 64
ROW_TILE = 256
TOPK_TILE = 128
PEER_TILE = 128
PEER_GROUP = 8
SC_TOKENS = 26624
SC_CHUNK_U = 64
SC_CHUNK_V = 32
SC_LANES = 16
PACK_TILE = 512


def _nt(a, b):
    return lax.dot_general(a.astype(BF16), b.astype(BF16), (((1,), (1,)), ((), ())),
                           preferred_element_type=F32)


def _nn(a, b):
    return jnp.dot(a.astype(BF16), b.astype(BF16), preferred_element_type=F32)


def _dot_f32(a, b):
    return jnp.dot(a, b, precision=HIGHEST, preferred_element_type=F32)


def _bnt(a, b):
    return lax.dot_general(a.astype(BF16), b.astype(BF16), (((2,), (2,)), ((0,), (0,))),
                           preferred_element_type=F32)


def _bnn(a, b):
    return lax.dot_general(a.astype(BF16), b.astype(BF16), (((2,), (1,)), ((0,), (0,))),
                           preferred_element_type=F32)


def _btn(a, b):
    return lax.dot_general(a.astype(BF16), b.astype(BF16), (((1,), (1,)), ((0,), (0,))),
                           preferred_element_type=F32)


def _split2(x):
    hi = x.astype(BF16)
    lo = (x - hi.astype(F32)).astype(BF16)
    return hi, lo


def _split3(x):
    hi = x.astype(BF16)
    r1 = x - hi.astype(F32)
    mid = r1.astype(BF16)
    lo = (r1 - mid.astype(F32)).astype(BF16)
    return hi, mid, lo


def _dot_x3(a, b):
    ah, al = _split2(a)
    bh, bl = _split2(b)
    d = lambda p, q: jnp.dot(p, q, preferred_element_type=F32)
    return d(ah, bh) + (d(ah, bl) + d(al, bh))


def _dot_exact_lhs(a_bf, b):
    return sum(jnp.dot(a_bf, p, preferred_element_type=F32) for p in _split3(b))


def _ada_kernel(c_ref, w_ref, b_ref, o_ref):
    c = c_ref[...]
    cond = c * jax.nn.sigmoid(c)
    o_ref[...] = _dot_f32(cond, w_ref[...]) + b_ref[...]


def _ada(c, ada_w, ada_b):
    bsz, d = c.shape
    n = ada_w.shape[1]
    return pl.pallas_call(
        _ada_kernel,
        grid=(n // d,),
        in_specs=[pl.BlockSpec((bsz, d), lambda j: (0, 0)),
                  pl.BlockSpec((d, d), lambda j: (0, j)),
                  pl.BlockSpec((1, d), lambda j: (0, j))],
        out_specs=pl.BlockSpec((bsz, d), lambda j: (0, j)),
        out_shape=jax.ShapeDtypeStruct((bsz, n), F32),
        name="ada",
    )(c, ada_w, ada_b.reshape(1, n))


def _col_chunks(width, step):
    return [(c0, min(step, width - c0)) for c0 in range(0, width, step)]


def _inproj_kernel(x_ref, ada_ref, nw_ref, w_ref, mu_ref, sh_ref, rest_ref, carry_ref,
                   *, n_shift, n_rest, tiles_per_seq):
    i = pl.program_id(0)
    x = x_ref[...]
    tm = x.shape[0]
    ms = jnp.mean(x * x, axis=-1, keepdims=True)
    ada = ada_ref[0]
    h = x * lax.rsqrt(ms + NORM_EPS) * nw_ref[...] * (1.0 + ada[1:2]) + ada[0:1]
    hb = h.astype(BF16)
    row0 = lax.broadcasted_iota(I32, (tm, 1), 0) == 0
    seq_start = (i % tiles_per_seq) == 0
    for c0, cw in _col_chunks(n_shift, 1024):
        p = jnp.dot(hb, w_ref[:, c0:c0 + cw], preferred_element_type=F32)
        prev_last = jnp.where(seq_start, 0.0, carry_ref[:, c0:c0 + cw])
        carry_ref[:, c0:c0 + cw] = p[tm - 1:tm, :]
        prev = jnp.where(row0, prev_last, pltpu.roll(p, 1, axis=0))
        sh_ref[:, c0:c0 + cw] = p + (prev - p) * mu_ref[:, c0:c0 + cw]
    for c0, cw in _col_chunks(n_rest, 1024):
        rest_ref[:, c0:c0 + cw] = jnp.dot(hb, w_ref[:, n_shift + c0:n_shift + c0 + cw],
                                          preferred_element_type=F32)


def _inproj(x2, ada3, norm_w, w_bf, mu, seq, n_shift):
    t, d = x2.shape
    n_rest = w_bf.shape[1] - n_shift
    tm = ROW_TILE
    tiles_per_seq = seq // tm
    kern = functools.partial(_inproj_kernel, n_shift=n_shift, n_rest=n_rest,
                             tiles_per_seq=tiles_per_seq)
    return pl.pallas_call(
        kern,
        grid=(t // tm,),
        in_specs=[pl.BlockSpec((tm, d), lambda i: (i, 0)),
                  pl.BlockSpec((1, N_ADA, d), lambda i: (i // tiles_per_seq, 0, 0)),
                  pl.BlockSpec((1, d), lambda i: (0, 0)),
                  pl.BlockSpec(w_bf.shape, lambda i: (0, 0), pipeline_mode=pl.Buffered(1)),
                  pl.BlockSpec((1, n_shift), lambda i: (0, 0))],
        out_specs=[pl.BlockSpec((tm, n_shift), lambda i: (i, 0)),
                   pl.BlockSpec((tm, n_rest), lambda i: (i, 0))],
        out_shape=[jax.ShapeDtypeStruct((t, n_shift), F32),
                   jax.ShapeDtypeStruct((t, n_rest), F32)],
        scratch_shapes=[pltpu.VMEM((1, n_shift), F32)],
        compiler_params=pltpu.CompilerParams(dimension_semantics=("arbitrary",),
                                             vmem_limit_bytes=VMEM_LIMIT),
        name="inproj",
    )(x2, ada3, norm_w.reshape(1, d), w_bf, mu.reshape(1, n_shift))


def _softplus(z):
    return jnp.maximum(z, 0.0) + jnp.log1p(jnp.exp(-jnp.abs(z)))


def _rwkv_heads(r, k, v, kkraw, a, cum, lw, s0, masks):
    strict, incl, diag, levels = masks
    c = r.shape[1]
    nrm = jnp.sqrt(jnp.sum(kkraw * kkraw, axis=-1, keepdims=True))
    kk = kkraw / jnp.maximum(nrm, 1e-12)
    w_in = jnp.exp(cum)
    w_ex = jnp.exp(cum - lw)
    w_inv = jnp.exp(-cum)
    w_end = jnp.exp(cum[:, c - 1:c, :] - cum)
    bb = kk * a
    rt = r * w_in
    at = -kk * w_ex
    bt = bb * w_inv
    kt = k * w_inv
    ab = _bnt(at, bt)
    ak = _bnt(at, kt)
    rb = _bnt(rt, bt)
    rk = _bnt(rt, kt)
    lab = jnp.where(strict, ab, 0.0)
    inv = jnp.where(levels[0], lab, 0.0) + jnp.where(diag, 1.0, 0.0)
    for lvl in levels[1:]:
        inv = inv + _bnn(_bnn(inv, jnp.where(lvl, lab, 0.0)), inv)
    rhs = _bnt(at, s0) + _bnn(jnp.where(strict, ak, 0.0), v)
    u = _bnn(inv, rhs)
    y = _bnt(rt, s0) + _bnn(jnp.where(incl, rb, 0.0), u) + _bnn(jnp.where(incl, rk, 0.0), v)
    s_new = s0 * w_in[:, c - 1:c, :] + _btn(u, bb * w_end) + _btn(v, k * w_end)
    return y, s_new


def _rwkv_kernel(pr_ref, pk_ref, pv_ref, tail_ref, w0_ref, wup_ref, a0_ref, aup_ref, gup_ref,
                 kk_ref, ka_ref, rk_ref, lnw_ref, lnb_ref, o_ref, state_ref):
    j = pl.program_id(1)
    c = pr_ref.shape[0]
    n_heads = pr_ref.shape[1] // RWKV_HEAD

    @pl.when(j == 0)
    def _():
        state_ref[...] = jnp.zeros_like(state_ref)

    tail = tail_ref[...]
    pw = tail[:, :DECAY_LORA]
    pa = tail[:, DECAY_LORA:DECAY_LORA + ICLR_LORA]
    pg = tail[:, DECAY_LORA + ICLR_LORA:]
    pk = pk_ref[...]
    w_log = -_softplus(-(w0_ref[...] + _dot_x3(jnp.tanh(pw), wup_ref[...]))) - 0.5
    lw = -jnp.exp(w_log)
    a = jax.nn.sigmoid(a0_ref[...] + _dot_x3(pa, aup_ref[...]))
    row = lax.broadcasted_iota(I32, (c, c), 0)
    col = lax.broadcasted_iota(I32, (c, c), 1)
    strict = row > col
    incl = row >= col
    diag = row == col
    cum = _dot_exact_lhs(incl.astype(BF16), lw)
    k = pk * (1.0 + (a - 1.0) * ka_ref[...])
    kkraw = pk * kk_ref[...]
    g = _dot_x3(jax.nn.sigmoid(pg), gup_ref[...])

    levels = []
    shift = 0
    while (1 << shift) < c:
        levels.append((jnp.right_shift(row, shift + 1) == jnp.right_shift(col, shift + 1))
                      & ((jnp.right_shift(row, shift) & 1) == 1)
                      & ((jnp.right_shift(col, shift) & 1) == 0))
        shift += 1
    masks = (strict, incl, diag, levels)

    def heads(x):
        return jnp.stack([x[:, h * RWKV_HEAD:(h + 1) * RWKV_HEAD] for h in range(n_heads)], axis=0)

    r3, k3, v3 = heads(pr_ref[...]), heads(k), heads(pv_ref[...])
    y, s_new = _rwkv_heads(r3, k3, v3, heads(kkraw), heads(a), heads(cum), heads(lw),
                           state_ref[...], masks)
    state_ref[...] = s_new
    mu = jnp.mean(y, axis=-1, keepdims=True)
    var = jnp.mean(jnp.square(y - mu), axis=-1, keepdims=True)
    yn = (y - mu) * lax.rsqrt(var + LNX_EPS) * lnw_ref[...] + lnb_ref[...]
    bonus = jnp.sum(r3 * k3 * rk_ref[...], axis=-1, keepdims=True) * v3
    out = (yn + bonus) * heads(g)
    for h in range(n_heads):
        o_ref[:, h * RWKV_HEAD:(h + 1) * RWKV_HEAD] = out[h]


def _rwkv(sh, bsz, seq, width, params):
    (w0, w_up, a0, a_up, g_up, k_k, k_a, r_k, lnx_w, lnx_b) = params
    c = RWKV_CHUNK
    nc = seq // c
    heads = width // RWKV_HEAD
    tail_w = DECAY_LORA + ICLR_LORA + GATE_LORA
    tail_blk = (3 * width) // tail_w
    row = lambda b, j: b * nc + j
    vec = lambda a: a.reshape(1, width)
    per_head = lambda a: a.reshape(heads, 1, RWKV_HEAD)
    const = lambda shape: pl.BlockSpec(shape, lambda b, j: (0,) * len(shape))
    return pl.pallas_call(
        _rwkv_kernel,
        grid=(bsz, nc),
        in_specs=[pl.BlockSpec((c, width), lambda b, j: (row(b, j), 0)),
                  pl.BlockSpec((c, width), lambda b, j: (row(b, j), 1)),
                  pl.BlockSpec((c, width), lambda b, j: (row(b, j), 2)),
                  pl.BlockSpec((c, tail_w), lambda b, j: (row(b, j), tail_blk)),
                  const((1, width)), const((DECAY_LORA, width)),
                  const((1, width)), const((ICLR_LORA, width)), const((GATE_LORA, width)),
                  const((1, width)), const((1, width)),
                  const((heads, 1, RWKV_HEAD)), const((heads, 1, RWKV_HEAD)),
                  const((heads, 1, RWKV_HEAD))],
        out_specs=pl.BlockSpec((c, width), lambda b, j: (row(b, j), 0)),
        out_shape=jax.ShapeDtypeStruct((bsz * seq, width), F32),
        scratch_shapes=[pltpu.VMEM((heads, RWKV_HEAD, RWKV_HEAD), F32)],
        compiler_params=pltpu.CompilerParams(dimension_semantics=("arbitrary", "arbitrary"),
                                             vmem_limit_bytes=VMEM_LIMIT),
        name="rwkv",
    )(sh, sh, sh, sh, vec(w0), w_up, vec(a0), a_up, g_up, vec(k_k), vec(k_a), per_head(r_k),
      per_head(lnx_w), per_head(lnx_b))


def _head_rms(x, w):
    return x * lax.rsqrt(jnp.mean(x * x, axis=-1, keepdims=True) + NORM_EPS) * w


def _attn_kernel(sink_ref, q_ref, kvc_ref, kvp_ref, qw_ref, kw_ref, o_ref):
    blk = pl.program_id(1)
    q = q_ref[...]
    kvc = kvc_ref[...]
    kvp = kvp_ref[...]
    kv_w = ATTN_KV_HEADS * ATTN_HEAD
    qi = lax.broadcasted_iota(I32, (BLOCK, 2 * BLOCK), 0)
    kj = lax.broadcasted_iota(I32, (BLOCK, 2 * BLOCK), 1)
    rel = kj - qi
    valid = (rel >= BLOCK - WINDOW + 1) & (rel <= BLOCK) & (blk * BLOCK - BLOCK + kj >= 0)
    for g in range(ATTN_KV_HEADS):
        ks = slice(g * ATTN_HEAD, (g + 1) * ATTN_HEAD)
        vs = slice(kv_w + g * ATTN_HEAD, kv_w + (g + 1) * ATTN_HEAD)
        kband = _head_rms(jnp.concatenate([kvp[:, ks], kvc[:, ks]], axis=0), kw_ref[...])
        vband = jnp.concatenate([kvp[:, vs], kvc[:, vs]], axis=0)
        for n in range(ATTN_GROUP):
            hq = g * ATTN_GROUP + n
            qh = _head_rms(q[:, hq * ATTN_HEAD:(hq + 1) * ATTN_HEAD], qw_ref[...])
            s = jnp.where(valid, _nt(qh, kband) * ATTN_SCALE, NEG_INF)
            sink = sink_ref[hq]
            m = jnp.maximum(jnp.max(s, axis=-1, keepdims=True), sink)
            p = jnp.exp(s - m)
            denom = jnp.sum(p, axis=-1, keepdims=True) + jnp.exp(sink - m)
            o_ref[:, hq * ATTN_HEAD:(hq + 1) * ATTN_HEAD] = _nn(p, vband) / denom


def _attn(rest, bsz, seq, width, q_norm_w, k_norm_w, sinks):
    nb = seq // BLOCK
    kv_w = 2 * ATTN_KV_HEADS * ATTN_HEAD
    kv_blk = (3 * width) // kv_w
    return pl.pallas_call(
        _attn_kernel,
        grid=(bsz, nb),
        in_specs=[pl.BlockSpec(memory_space=pltpu.SMEM),
                  pl.BlockSpec((BLOCK, width), lambda b, j: (b * nb + j, 0)),
                  pl.BlockSpec((BLOCK, kv_w), lambda b, j: (b * nb + j, kv_blk)),
                  pl.BlockSpec((BLOCK, kv_w), lambda b, j: (b * nb + jnp.maximum(j - 1, 0), kv_blk)),
                  pl.BlockSpec((1, ATTN_HEAD), lambda b, j: (0, 0)),
                  pl.BlockSpec((1, ATTN_HEAD), lambda b, j: (0, 0))],
        out_specs=pl.BlockSpec((BLOCK, width), lambda b, j: (b * nb + j, 0)),
        out_shape=jax.ShapeDtypeStruct((bsz * seq, width), F32),
        compiler_params=pltpu.CompilerParams(dimension_semantics=("arbitrary", "arbitrary"),
                                             vmem_limit_bytes=VMEM_LIMIT),
        name="attn",
    )(sinks, rest, rest, rest, q_norm_w.reshape(1, ATTN_HEAD), k_norm_w.reshape(1, ATTN_HEAD))


def _outproj_kernel(ya_ref, yb_ref, ga_ref, gb_ref, x_ref, ada_ref, wo_ref, nw_ref, wq_ref,
                    k1_ref, k2_ref, x1_ref, h2_ref, sct_ref):
    ada = ada_ref[0]
    mixed = jax.nn.sigmoid(ga_ref[...]) * ya_ref[...] + jax.nn.sigmoid(gb_ref[...]) * yb_ref[...]
    x1 = x_ref[...] + ada[2:3] * _nn(mixed, wo_ref[...])
    x1_ref[...] = x1
    ms = jnp.mean(x1 * x1, axis=-1, keepdims=True)
    h2 = x1 * lax.rsqrt(ms + NORM_EPS) * nw_ref[...] * (1.0 + ada[4:5]) + ada[3:4]
    h2_ref[...] = h2
    q = _nn(h2, wq_ref[...])
    for ch in range(q.shape[1] // PEER_HALF):
        keys = k1_ref if ch % 2 == 0 else k2_ref
        cs = slice(ch * PEER_HALF, (ch + 1) * PEER_HALF)
        sct_ref[cs, :] = _nt(keys[...], q[:, cs])


def _outproj(ya, yb, rest, x2, ada3, w_out_bf, norm_w, wq_bf, keys_1, keys_2, seq):
    t, d = x2.shape
    tm = ROW_TILE
    tiles_per_seq = seq // tm
    qd = wq_bf.shape[1]
    rowblk = lambda c: pl.BlockSpec((tm, d), lambda i: (i, c))
    const = lambda shape: pl.BlockSpec(shape, lambda i: (0, 0))
    return pl.pallas_call(
        _outproj_kernel,
        grid=(t // tm,),
        in_specs=[rowblk(0), rowblk(0), rowblk(1), rowblk(2), rowblk(0),
                  pl.BlockSpec((1, N_ADA, d), lambda i: (i // tiles_per_seq, 0, 0)),
                  const((d, d)), const((1, d)), const((d, qd)),
                  const(keys_1.shape), const(keys_2.shape)],
        out_specs=[rowblk(0), rowblk(0), pl.BlockSpec((qd, tm), lambda i: (0, i))],
        out_shape=[jax.ShapeDtypeStruct((t, d), F32), jax.ShapeDtypeStruct((t, d), F32),
                   jax.ShapeDtypeStruct((qd, t), F32)],
        compiler_params=pltpu.CompilerParams(dimension_semantics=("arbitrary",),
                                             vmem_limit_bytes=VMEM_LIMIT),
        name="outproj",
    )(ya, yb, rest, rest, x2, ada3, w_out_bf, norm_w.reshape(1, d), wq_bf, keys_1, keys_2)


def _top16_rows(s, payload=None):
    n = s.shape[0]
    iota = lax.broadcasted_iota(I32, s.shape, 0).astype(F32)
    vals, picks = [], []
    for _ in range(PEER_TOPK):
        m = jnp.max(s, axis=0, keepdims=True)
        i = jnp.min(jnp.where(s == m, iota, float(n)), axis=0, keepdims=True)
        hit = iota == i
        s = jnp.where(hit, -jnp.inf, s)
        vals.append(m)
        picks.append(i if payload is None else
                     jnp.max(jnp.where(hit, payload, -1.0), axis=0, keepdims=True))
    return jnp.concatenate(vals, axis=0), jnp.concatenate(picks, axis=0)


def _topk_kernel(sct_ref, idx_ref, idx_tm_ref, gate_ref):
    keep = [PEER_TOPK // (i + 1) for i in range(PEER_TOPK)]
    pad = (-sum(keep)) % SUBLANES
    idx_all, gate_all = [], []
    for h in range(PEER_HEADS):
        base = h * 2 * N_KEYS
        v1, i1 = _top16_rows(sct_ref[base:base + N_KEYS, :])
        v2, i2 = _top16_rows(sct_ref[base + N_KEYS:base + 2 * N_KEYS, :])
        cand = [v1[i:i + 1, :] + v2[0:keep[i], :] for i in range(PEER_TOPK)]
        cidx = [i1[i:i + 1, :] * float(N_KEYS) + i2[0:keep[i], :] for i in range(PEER_TOPK)]
        if pad:
            cand.append(jnp.full((pad, v1.shape[1]), -jnp.inf, F32))
            cidx.append(jnp.zeros((pad, v1.shape[1]), F32))
        sc, idx = _top16_rows(jnp.concatenate(cand, axis=0), jnp.concatenate(cidx, axis=0))
        idx_all.append(idx.astype(I32))
        ex = jnp.exp(sc - sc[0:1, :])
        gate_all.append(ex / jnp.sum(ex, axis=0, keepdims=True))
    idx = jnp.concatenate(idx_all, axis=0)
    idx_ref[...] = idx
    idx_tm_ref[...] = idx.T
    gate_ref[...] = jnp.concatenate(gate_all, axis=0).T


def _topk(sct):
    qd, t = sct.shape
    tk = TOPK_TILE
    ne = PEER_HEADS * PEER_TOPK
    return pl.pallas_call(
        _topk_kernel,
        grid=(t // tk,),
        in_specs=[pl.BlockSpec((qd, tk), lambda i: (0, i))],
        out_specs=[pl.BlockSpec((ne, tk), lambda i: (0, i)),
                   pl.BlockSpec((tk, ne), lambda i: (i, 0)),
                   pl.BlockSpec((tk, ne), lambda i: (i, 0))],
        out_shape=[jax.ShapeDtypeStruct((ne, t), I32), jax.ShapeDtypeStruct((t, ne), I32),
                   jax.ShapeDtypeStruct((t, ne), F32)],
        compiler_params=pltpu.CompilerParams(dimension_semantics=("arbitrary",),
                                             vmem_limit_bytes=VMEM_LIMIT),
        name="topk",
    )(sct)


def _cast_kernel(x_ref, o_ref):
    o_ref[...] = x_ref[...].astype(BF16)


def _table_bf16(tab):
    n, d = tab.shape
    out = pl.pallas_call(
        _cast_kernel,
        grid=(n // PACK_TILE,),
        in_specs=[pl.BlockSpec((PACK_TILE, d), lambda i: (i, 0))],
        out_specs=pl.BlockSpec((PACK_TILE, d), lambda i: (i, 0)),
        out_shape=jax.ShapeDtypeStruct((n, d), BF16),
        name="table_cast",
    )(tab)
    return out.reshape(n, d // LANES, LANES), lax.bitcast_convert_type(out.reshape(n, d // 2, 2), I32)


def _gelu_exact(x):
    return 0.5 * x * (1.0 + lax.erf(x * (2.0 ** -0.5)))


def _group_mask(ne):
    sub = lax.broadcasted_iota(I32, (SUBLANES, ne * SUBLANES), 0)
    col = lax.broadcasted_iota(I32, (SUBLANES, ne * SUBLANES), 1)
    return sub == (col & (SUBLANES - 1))


def _gather_rows(idx_ref, tab_ref, t, rows_ref):
    picks = idx_ref.at[:, pl.ds(t, 1)]
    for e in range(idx_ref.shape[0]):
        rows_ref[e * SUBLANES:(e + 1) * SUBLANES, :] = tab_ref[picks[e, 0]]


def _peer_u_kernel(idx_ref, x_ref, gate_ref, tab_ref, wts_ref, *scratch):
    rows, d_s = scratch[:-1], scratch[-1]
    tb = x_ref.shape[0]
    ne = gate_ref.shape[1]
    mask = _group_mask(ne)

    def reduce(t, rows_ref):
        hi, lo = _split2(x_ref[t])
        lhs = jnp.concatenate([hi, lo], axis=0)
        z = lax.dot_general(lhs, rows_ref[...], (((1,), (1,)), ((), ())),
                            preferred_element_type=F32)
        zz = z[:SUBLANES] + z[SUBLANES:]
        d_s[pl.ds(t, 1), :] = jnp.sum(jnp.where(mask, zz, 0.0), axis=0, keepdims=True)

    def group(i, carry):
        for q, rows_ref in enumerate(rows):
            _gather_rows(idx_ref, tab_ref, len(rows) * i + q, rows_ref)
        for q, rows_ref in enumerate(rows):
            reduce(len(rows) * i + q, rows_ref)
        return carry

    lax.fori_loop(0, tb // len(rows), group, 0)
    r = lax.broadcasted_iota(I32, (ne * SUBLANES, ne), 0)
    c = lax.broadcasted_iota(I32, (ne * SUBLANES, ne), 1)
    fold = (jnp.right_shift(r, 3) == c).astype(BF16)
    dots = sum(jnp.dot(p, fold, preferred_element_type=F32) for p in _split3(d_s[...]))
    wts_ref[...] = gate_ref[...] * _gelu_exact(dots)


def _peer_u(idx_t, h3, gate, tab3, t):
    ne = gate.shape[1]
    tb = PEER_TILE
    return pl.pallas_call(
        _peer_u_kernel,
        grid=(t // tb,),
        in_specs=[pl.BlockSpec((ne, tb), lambda i: (0, i), memory_space=pltpu.SMEM),
                  pl.BlockSpec((tb, SUBLANES, LANES), lambda i: (i, 0, 0)),
                  pl.BlockSpec((tb, ne), lambda i: (i, 0)),
                  pl.BlockSpec(tab3.shape, lambda i: (0, 0, 0), pipeline_mode=pl.Buffered(1))],
        out_specs=pl.BlockSpec((tb, ne), lambda i: (i, 0)),
        out_shape=jax.ShapeDtypeStruct((t, ne), F32),
        scratch_shapes=[pltpu.VMEM((ne * SUBLANES, LANES), BF16)] * PEER_GROUP
                       + [pltpu.VMEM((tb, ne * SUBLANES), F32)],
        compiler_params=pltpu.CompilerParams(dimension_semantics=("arbitrary",),
                                             vmem_limit_bytes=VMEM_LIMIT),
        name="peer_u",
    )(idx_t, h3, gate, tab3)


def _peer_v_kernel(idx_ref, wts_ref, x1_ref, gt_ref, tab_ref, o_ref, *scratch):
    rows, w8_s = scratch[:-1], scratch[-1]
    tb = x1_ref.shape[0]
    ne = wts_ref.shape[1]
    mask = _group_mask(ne)
    gt = gt_ref[0]
    r = lax.broadcasted_iota(I32, (ne, ne * SUBLANES), 0)
    c = lax.broadcasted_iota(I32, (ne, ne * SUBLANES), 1)
    spread = (r == jnp.right_shift(c, 3)).astype(BF16)
    w8_s[...] = sum(jnp.dot(p, spread, preferred_element_type=F32) for p in _split3(wts_ref[...]))

    def combine(t, rows_ref):
        wm = jnp.where(mask, w8_s[pl.ds(t, 1), :], 0.0)
        hi, lo = _split2(wm)
        res = jnp.dot(jnp.concatenate([hi, lo], axis=0), rows_ref[...],
                      preferred_element_type=F32)
        o_ref[t] = x1_ref[t] + gt * (res[:SUBLANES] + res[SUBLANES:])

    def group(i, carry):
        for q, rows_ref in enumerate(rows):
            _gather_rows(idx_ref, tab_ref, len(rows) * i + q, rows_ref)
        for q, rows_ref in enumerate(rows):
            combine(len(rows) * i + q, rows_ref)
        return carry

    lax.fori_loop(0, tb // len(rows), group, 0)


def _peer_v(idx_t, wts, x13, gt3, tab3, seq):
    t, ne = wts.shape
    tb = PEER_TILE
    tiles_per_seq = seq // tb
    return pl.pallas_call(
        _peer_v_kernel,
        grid=(t // tb,),
        in_specs=[pl.BlockSpec((ne, tb), lambda i: (0, i), memory_space=pltpu.SMEM),
                  pl.BlockSpec((tb, ne), lambda i: (i, 0)),
                  pl.BlockSpec((tb, SUBLANES, LANES), lambda i: (i, 0, 0)),
                  pl.BlockSpec((1, SUBLANES, LANES), lambda i: (i // tiles_per_seq, 0, 0)),
                  pl.BlockSpec(tab3.shape, lambda i: (0, 0, 0), pipeline_mode=pl.Buffered(1))],
        out_specs=pl.BlockSpec((tb, SUBLANES, LANES), lambda i: (i, 0, 0)),
        out_shape=jax.ShapeDtypeStruct((t, SUBLANES, LANES), F32),
        scratch_shapes=[pltpu.VMEM((ne * SUBLANES, LANES), BF16)] * PEER_GROUP
                       + [pltpu.VMEM((tb, ne * SUBLANES), F32)],
        compiler_params=pltpu.CompilerParams(dimension_semantics=("arbitrary",),
                                             vmem_limit_bytes=VMEM_LIMIT),
        name="peer_v",
    )(idx_t, wts, x13, gt3, tab3)


def _sc_mesh():
    info = plsc.get_sparse_core_info()
    mesh = plsc.VectorSubcoreMesh(core_axis_name="c", subcore_axis_name="s")
    return mesh, info.num_cores * info.num_subcores, info.num_subcores


def _sc_gather_chunks(tab_hbm, idx_v, bufs, sems, n_chunks, consume):
    rows = bufs[0].shape[0]

    def start(ch):
        return pltpu.async_copy(tab_hbm.at[idx_v.at[pl.ds(ch * rows, rows)]],
                                bufs[ch % 2], sems[ch % 2])

    copy = start(0)
    for ch in range(n_chunks):
        nxt = start(ch + 1) if ch + 1 < n_chunks else None
        copy.wait()
        consume(ch, bufs[ch % 2])
        copy = nxt


def _sc_unpack(words):
    even = lax.bitcast_convert_type(jnp.left_shift(words, 16), F32)
    odd = lax.bitcast_convert_type(jnp.bitwise_and(words, -65536), F32)
    return even, odd


def _peer_u_sc(idx_tm, h2, utab_w, tok_base, tsc):
    ne = idx_tm.shape[1]
    dw = utab_w.shape[1]
    d = 2 * dw
    mesh, nw, nsub = _sc_mesh()
    tps = tsc // nw
    rows = SC_CHUNK_U
    assert tsc % nw == 0 and ne % rows == 0 and dw % SC_LANES == 0

    @functools.partial(
        pl.kernel, mesh=mesh, out_type=jax.ShapeDtypeStruct((tsc, ne * SC_LANES), F32),
        scratch_types=[pltpu.VMEM((ne,), I32), pltpu.VMEM((d,), F32),
                       pltpu.VMEM((dw,), F32), pltpu.VMEM((dw,), F32),
                       pltpu.VMEM((rows, dw), I32), pltpu.VMEM((rows, dw), I32),
                       pltpu.VMEM((ne * SC_LANES,), F32),
                       pltpu.SemaphoreType.DMA, pltpu.SemaphoreType.DMA],
        compiler_params=pltpu.CompilerParams(needs_layout_passes=False), name="peer_u_sc")
    def k(idx_hbm, x_hbm, u_hbm, o_hbm, idx_v, x_v, xe_v, xo_v, rows_a, rows_b, dp_v, sem_a, sem_b):
        wid = lax.axis_index("c") * nsub + lax.axis_index("s")
        zero = jnp.zeros((SC_LANES,), F32)
        lane2 = 2 * lax.iota(I32, SC_LANES)

        @pl.loop(0, tps)
        def _(t):
            loc = wid * tps + t
            pltpu.sync_copy(idx_hbm.at[tok_base + loc], idx_v)
            pltpu.sync_copy(x_hbm.at[tok_base + loc], x_v)
            for j in range(dw // SC_LANES):
                sl = pl.ds(j * SC_LANES, SC_LANES)
                xe_v[sl] = plsc.load_gather(x_v, [lane2 + 2 * SC_LANES * j])
                xo_v[sl] = plsc.load_gather(x_v, [lane2 + (2 * SC_LANES * j + 1)])
            for e in range(ne):
                dp_v[pl.ds(e * SC_LANES, SC_LANES)] = zero

            def consume(ch, rows_v):
                @pl.loop(0, dw, step=SC_LANES)
                def _(c):
                    sl = pl.ds(pl.multiple_of(c, SC_LANES), SC_LANES)
                    xe = xe_v[sl]
                    xo = xo_v[sl]

                    @plsc.parallel_loop(0, rows, 1, unroll=8)
                    def _(r):
                        even, odd = _sc_unpack(rows_v[r, sl])
                        dst = pl.ds(pl.multiple_of((ch * rows + r) * SC_LANES, SC_LANES), SC_LANES)
                        plsc.addupdate(dp_v.at[dst], even * xe + odd * xo)

            _sc_gather_chunks(u_hbm, idx_v, (rows_a, rows_b), (sem_a, sem_b), ne // rows, consume)
            pltpu.sync_copy(dp_v, o_hbm.at[loc])

    return k(idx_tm, h2, utab_w)


def _sc_weights_kernel(dp_ref, gate_ref, wts_tc_ref, o_ref, wts_tc_out_ref):
    del wts_tc_ref, wts_tc_out_ref
    ne = gate_ref.shape[1]
    r = lax.broadcasted_iota(I32, (ne * SC_LANES, ne), 0)
    c = lax.broadcasted_iota(I32, (ne * SC_LANES, ne), 1)
    fold = (jnp.right_shift(r, 4) == c).astype(BF16)
    dots = sum(jnp.dot(p, fold, preferred_element_type=F32) for p in _split3(dp_ref[...]))
    wts = gate_ref[...] * _gelu_exact(dots)
    r = lax.broadcasted_iota(I32, (ne, ne * SC_LANES), 0)
    c = lax.broadcasted_iota(I32, (ne, ne * SC_LANES), 1)
    spread = (r == jnp.right_shift(c, 4)).astype(BF16)
    o_ref[...] = sum(jnp.dot(p, spread, preferred_element_type=F32) for p in _split3(wts))


def _sc_weights(dparts, gate, wts_tc, tok_base):
    tsc, w = dparts.shape
    ne = gate.shape[1]
    tb = ROW_TILE
    blk0 = tok_base // tb
    return pl.pallas_call(
        _sc_weights_kernel,
        grid=(tsc // tb,),
        in_specs=[pl.BlockSpec((tb, w), lambda i: (i, 0)),
                  pl.BlockSpec((tb, ne), lambda i: (i + blk0, 0)),
                  pl.BlockSpec(memory_space=pl.ANY)],
        out_specs=[pl.BlockSpec((tb, w), lambda i: (i, 0)),
                   pl.BlockSpec(memory_space=pl.ANY)],
        out_shape=[jax.ShapeDtypeStruct((tsc, w), F32),
                   jax.ShapeDtypeStruct(wts_tc.shape, wts_tc.dtype)],
        input_output_aliases={2: 1},
        name="sc_weights",
    )(dparts, gate, wts_tc)


def _peer_v_sc(idx_tm, w16, x1, gt2, vtab_w, seq, tok_base):
    tsc = w16.shape[0]
    ne = idx_tm.shape[1]
    dw = vtab_w.shape[1]
    d = 2 * dw
    mesh, nw, nsub = _sc_mesh()
    tps = tsc // nw
    shift = seq.bit_length() - 1
    rows = SC_CHUNK_V
    assert (1 << shift) == seq and tsc % nw == 0 and ne % rows == 0 and dw % SC_LANES == 0

    @functools.partial(
        pl.kernel, mesh=mesh, out_type=jax.ShapeDtypeStruct((tsc, d), F32),
        scratch_types=[pltpu.VMEM((ne,), I32), pltpu.VMEM((ne * SC_LANES,), F32),
                       pltpu.VMEM((rows, dw), I32), pltpu.VMEM((rows, dw), I32),
                       pltpu.VMEM((dw,), F32), pltpu.VMEM((dw,), F32),
                       pltpu.VMEM((d,), F32), pltpu.VMEM((d,), F32), pltpu.VMEM((d,), F32),
                       pltpu.SemaphoreType.DMA, pltpu.SemaphoreType.DMA],
        compiler_params=pltpu.CompilerParams(needs_layout_passes=False), name="peer_v_sc")
    def k(idx_hbm, w_hbm, x_hbm, gt_hbm, v_hbm, o_hbm, idx_v, w_v, rows_a, rows_b, acce_v, acco_v,
          x_v, gt_v, out_v, sem_a, sem_b):
        wid = lax.axis_index("c") * nsub + lax.axis_index("s")
        zero = jnp.zeros((SC_LANES,), F32)
        lane2 = 2 * lax.iota(I32, SC_LANES)

        @pl.loop(0, tps)
        def _(t):
            loc = wid * tps + t
            tok = tok_base + loc
            pltpu.sync_copy(idx_hbm.at[tok], idx_v)
            pltpu.sync_copy(w_hbm.at[loc], w_v)
            pltpu.sync_copy(x_hbm.at[tok], x_v)
            pltpu.sync_copy(gt_hbm.at[jnp.right_shift(tok, shift)], gt_v)
            for j in range(dw // SC_LANES):
                acce_v[pl.ds(j * SC_LANES, SC_LANES)] = zero
                acco_v[pl.ds(j * SC_LANES, SC_LANES)] = zero

            def consume(ch, rows_v):
                @pl.loop(0, rows)
                def _(e):
                    wv = w_v[pl.ds((ch * rows + e) * SC_LANES, SC_LANES)]

                    @plsc.parallel_loop(0, dw, SC_LANES, unroll=8)
                    def _(c):
                        sl = pl.ds(pl.multiple_of(c, SC_LANES), SC_LANES)
                        even, odd = _sc_unpack(rows_v[e, sl])
                        plsc.addupdate(acce_v.at[sl], wv * even)
                        plsc.addupdate(acco_v.at[sl], wv * odd)

            _sc_gather_chunks(v_hbm, idx_v, (rows_a, rows_b), (sem_a, sem_b), ne // rows, consume)
            for j in range(dw // SC_LANES):
                sl = pl.ds(j * SC_LANES, SC_LANES)
                ie = lane2 + 2 * SC_LANES * j
                io = ie + 1
                plsc.store_scatter(out_v, [ie], plsc.load_gather(x_v, [ie])
                                   + plsc.load_gather(gt_v, [ie]) * acce_v[sl])
                plsc.store_scatter(out_v, [io], plsc.load_gather(x_v, [io])
                                   + plsc.load_gather(gt_v, [io]) * acco_v[sl])
            pltpu.sync_copy(out_v, o_hbm.at[loc])

    return k(idx_tm, w16, x1, gt2, vtab_w)


def _perm_in_columns(width, kv_width):
    n_shift = 3 * width + DECAY_LORA + ICLR_LORA + GATE_LORA
    aq = n_shift
    akv = aq + width
    ga = akv + 2 * kv_width
    gb = ga + width
    return n_shift, np.concatenate([np.arange(0, n_shift), np.arange(aq, aq + width),
                                    np.arange(ga, ga + width), np.arange(gb, gb + width),
                                    np.arange(akv, akv + 2 * kv_width)])


def _layer(x, ada, norm1_w, norm2_w, w_in, shift_mu, w0, w_lora_up, a0, a_lora_up, g_lora_up,
           k_k, k_a, r_k, lnx_w, lnx_b, q_norm_w, k_norm_w, sinks, w_out, peer_w_q,
           peer_keys_1, peer_keys_2, peer_u, peer_v):
    bsz, seq, d = x.shape
    t = bsz * seq
    x2 = x.reshape(t, d)
    ada3 = ada.reshape(bsz, N_ADA, d)
    n_shift, perm = _perm_in_columns(d, ATTN_KV_HEADS * ATTN_HEAD)
    w_bf = w_in[:, perm].astype(BF16)
    sh, rest = _inproj(x2, ada3, norm1_w, w_bf, shift_mu, seq, n_shift)
    ya = _rwkv(sh, bsz, seq, d, (w0, w_lora_up, a0, a_lora_up, g_lora_up, k_k, k_a,
                                 r_k.reshape(-1), lnx_w, lnx_b))
    yb = _attn(rest, bsz, seq, d, q_norm_w, k_norm_w, sinks)
    x1, h2, sct = _outproj(ya, yb, rest, x2, ada3, w_out.astype(BF16), norm2_w,
                           peer_w_q.astype(BF16), peer_keys_1, peer_keys_2, seq)
    idx_t, idx_tm, gate = _topk(sct)
    sub = d // LANES
    gt2 = ada3[:, 5, :]
    t_tc = t - SC_TOKENS
    u3, uw = _table_bf16(peer_u)
    v3, vw = _table_bf16(peer_v)
    wts = _peer_u(idx_t, h2.reshape(t, sub, LANES), gate, u3, t_tc)
    dparts = _peer_u_sc(idx_tm, h2, uw, t_tc, SC_TOKENS)
    w16, wts = _sc_weights(dparts, gate, wts, t_tc)
    out_tc = _peer_v(idx_t, wts, x1.reshape(t, sub, LANES), gt2.reshape(bsz, sub, LANES), v3, seq)
    out_sc = _peer_v_sc(idx_tm, w16, x1, gt2, vw, seq, t_tc)
    return jnp.concatenate([out_tc.reshape(t_tc, d), out_sc], axis=0).reshape(bsz, seq, d)


def kernel(x, c, ada_w, ada_b, norm1_w, norm2_w, w_in, shift_mu, w0, w_lora_up, a0, a_lora_up, g_lora_up, k_k, k_a, r_k, lnx_w, lnx_b, q_norm_w, k_norm_w, sinks, w_out, peer_w_q, peer_keys_1, peer_keys_2, peer_u, peer_v):
    depth = ada_w.shape[0]
    for l in range(depth):
        ada = _ada(c, ada_w[l], ada_b[l])
        x = _layer(x, ada, norm1_w[l], norm2_w[l], w_in[l], shift_mu[l], w0[l], w_lora_up[l],
                   a0[l], a_lora_up[l], g_lora_up[l], k_k[l], k_a[l], r_k[l], lnx_w[l], lnx_b[l],
                   q_norm_w[l], k_norm_w[l], sinks[l], w_out[l], peer_w_q[l], peer_keys_1[l],
                   peer_keys_2[l], peer_u[l], peer_v[l])
    return x
```

```python
import functools

import numpy as np
import jax
import jax.numpy as jnp
from jax import lax
from jax.experimental import pallas as pl
from jax.experimental.pallas import tpu as pltpu
from jax.experimental.pallas import tpu_sc as plsc

F32 = jnp.float32
BF16 = jnp.bfloat16
I32 = jnp.int32
HIGHEST = lax.Precision.HIGHEST

RWKV_HEAD = 64
DECAY_LORA = 64
ICLR_LORA = 64
GATE_LORA = 128
LNX_EPS = 64e-5
ATTN_HEAD = 64
ATTN_Q_HEADS = 16
ATTN_KV_HEADS = 2
ATTN_GROUP = ATTN_Q_HEADS // ATTN_KV_HEADS
WINDOW = 128
BLOCK = 128
ATTN_SCALE = ATTN_HEAD ** -0.5
NEG_INF = -1e30
N_KEYS = 128
PEER_HEADS = 8
PEER_HALF = 128
PEER_TOPK = 16
NORM_EPS = 1e-6
N_ADA = 6

LANES = 128
SUBLANES = 8
VMEM_LIMIT = 56 * 1024 * 1024

RWKV_CHUNK = 64
ROW_TILE = 256
TOPK_TILE = 128
PEER_TILE = 128
PEER_GROUP = 8
SC_TOKENS_U = 25600
SC_TOKENS_V = 21504
SC_CHUNK_U = 64
SC_CHUNK_V = 32
SC_LANES = 16
PACK_TILE = 512


def _nt(a, b):
    return lax.dot_general(a.astype(BF16), b.astype(BF16), (((1,), (1,)), ((), ())),
                           preferred_element_type=F32)


def _nn(a, b):
    return jnp.dot(a.astype(BF16), b.astype(BF16), preferred_element_type=F32)


def _dot_f32(a, b):
    return jnp.dot(a, b, precision=HIGHEST, preferred_element_type=F32)


def _bnt(a, b):
    return lax.dot_general(a.astype(BF16), b.astype(BF16), (((2,), (2,)), ((0,), (0,))),
                           preferred_element_type=F32)


def _bnn(a, b):
    return lax.dot_general(a.astype(BF16), b.astype(BF16), (((2,), (1,)), ((0,), (0,))),
                           preferred_element_type=F32)


def _btn(a, b):
    return lax.dot_general(a.astype(BF16), b.astype(BF16), (((1,), (1,)), ((0,), (0,))),
                           preferred_element_type=F32)


def _split2(x):
    hi = x.astype(BF16)
    lo = (x - hi.astype(F32)).astype(BF16)
    return hi, lo


def _split3(x):
    hi = x.astype(BF16)
    r1 = x - hi.astype(F32)
    mid = r1.astype(BF16)
    lo = (r1 - mid.astype(F32)).astype(BF16)
    return hi, mid, lo


def _dot_x3(a, b):
    ah, al = _split2(a)
    bh, bl = _split2(b)
    d = lambda p, q: jnp.dot(p, q, preferred_element_type=F32)
    return d(ah, bh) + (d(ah, bl) + d(al, bh))


def _dot_exact_lhs(a_bf, b):
    return sum(jnp.dot(a_bf, p, preferred_element_type=F32) for p in _split3(b))


def _ada_kernel(c_ref, w_ref, b_ref, o_ref):
    c = c_ref[...]
    cond = c * jax.nn.sigmoid(c)
    o_ref[...] = _dot_f32(cond, w_ref[...]) + b_ref[...]


def _ada(c, ada_w, ada_b):
    bsz, d = c.shape
    n = ada_w.shape[1]
    return pl.pallas_call(
        _ada_kernel,
        grid=(n // d,),
        in_specs=[pl.BlockSpec((bsz, d), lambda j: (0, 0)),
                  pl.BlockSpec((d, d), lambda j: (0, j)),
                  pl.BlockSpec((1, d), lambda j: (0, j))],
        out_specs=pl.BlockSpec((bsz, d), lambda j: (0, j)),
        out_shape=jax.ShapeDtypeStruct((bsz, n), F32),
        name="ada",
    )(c, ada_w, ada_b.reshape(1, n))


def _col_chunks(width, step):
    return [(c0, min(step, width - c0)) for c0 in range(0, width, step)]


def _inproj_kernel(x_ref, ada_ref, nw_ref, w_ref, mu_ref, sh_ref, rest_ref, carry_ref,
                   *, n_shift, n_rest, tiles_per_seq):
    i = pl.program_id(0)
    x = x_ref[...]
    tm = x.shape[0]
    ms = jnp.mean(x * x, axis=-1, keepdims=True)
    ada = ada_ref[0]
    h = x * lax.rsqrt(ms + NORM_EPS) * nw_ref[...] * (1.0 + ada[1:2]) + ada[0:1]
    hb = h.astype(BF16)
    row0 = lax.broadcasted_iota(I32, (tm, 1), 0) == 0
    seq_start = (i % tiles_per_seq) == 0
    for c0, cw in _col_chunks(n_shift, 1024):
        p = jnp.dot(hb, w_ref[:, c0:c0 + cw], preferred_element_type=F32)
        prev_last = jnp.where(seq_start, 0.0, carry_ref[:, c0:c0 + cw])
        carry_ref[:, c0:c0 + cw] = p[tm - 1:tm, :]
        prev = jnp.where(row0, prev_last, pltpu.roll(p, 1, axis=0))
        sh_ref[:, c0:c0 + cw] = p + (prev - p) * mu_ref[:, c0:c0 + cw]
    for c0, cw in _col_chunks(n_rest, 1024):
        rest_ref[:, c0:c0 + cw] = jnp.dot(hb, w_ref[:, n_shift + c0:n_shift + c0 + cw],
                                          preferred_element_type=F32)


def _inproj(x2, ada3, norm_w, w_bf, mu, seq, n_shift):
    t, d = x2.shape
    n_rest = w_bf.shape[1] - n_shift
    tm = ROW_TILE
    tiles_per_seq = seq // tm
    kern = functools.partial(_inproj_kernel, n_shift=n_shift, n_rest=n_rest,
                             tiles_per_seq=tiles_per_seq)
    return pl.pallas_call(
        kern,
        grid=(t // tm,),
        in_specs=[pl.BlockSpec((tm, d), lambda i: (i, 0)),
                  pl.BlockSpec((1, N_ADA, d), lambda i: (i // tiles_per_seq, 0, 0)),
                  pl.BlockSpec((1, d), lambda i: (0, 0)),
                  pl.BlockSpec(w_bf.shape, lambda i: (0, 0), pipeline_mode=pl.Buffered(1)),
                  pl.BlockSpec((1, n_shift), lambda i: (0, 0))],
        out_specs=[pl.BlockSpec((tm, n_shift), lambda i: (i, 0)),
                   pl.BlockSpec((tm, n_rest), lambda i: (i, 0))],
        out_shape=[jax.ShapeDtypeStruct((t, n_shift), F32),
                   jax.ShapeDtypeStruct((t, n_rest), F32)],
        scratch_shapes=[pltpu.VMEM((1, n_shift), F32)],
        compiler_params=pltpu.CompilerParams(dimension_semantics=("arbitrary",),
                                             vmem_limit_bytes=VMEM_LIMIT),
        name="inproj",
    )(x2, ada3, norm_w.reshape(1, d), w_bf, mu.reshape(1, n_shift))


def _softplus(z):
    return jnp.maximum(z, 0.0) + jnp.log1p(jnp.exp(-jnp.abs(z)))


def _rwkv_heads(r, k, v, kkraw, a, cum, lw, s0, masks):
    strict, incl, diag, levels = masks
    c = r.shape[1]
    nrm = jnp.sqrt(jnp.sum(kkraw * kkraw, axis=-1, keepdims=True))
    kk = kkraw / jnp.maximum(nrm, 1e-12)
    w_in = jnp.exp(cum)
    w_ex = jnp.exp(cum - lw)
    w_inv = jnp.exp(-cum)
    w_end = jnp.exp(cum[:, c - 1:c, :] - cum)
    bb = kk * a
    rt = r * w_in
    at = -kk * w_ex
    bt = bb * w_inv
    kt = k * w_inv
    ab = _bnt(at, bt)
    ak = _bnt(at, kt)
    rb = _bnt(rt, bt)
    rk = _bnt(rt, kt)
    lab = jnp.where(strict, ab, 0.0)
    inv = jnp.where(levels[0], lab, 0.0) + jnp.where(diag, 1.0, 0.0)
    for lvl in levels[1:]:
        inv = inv + _bnn(_bnn(inv, jnp.where(lvl, lab, 0.0)), inv)
    rhs = _bnt(at, s0) + _bnn(jnp.where(strict, ak, 0.0), v)
    u = _bnn(inv, rhs)
    y = _bnt(rt, s0) + _bnn(jnp.where(incl, rb, 0.0), u) + _bnn(jnp.where(incl, rk, 0.0), v)
    s_new = s0 * w_in[:, c - 1:c, :] + _btn(u, bb * w_end) + _btn(v, k * w_end)
    return y, s_new


def _rwkv_kernel(pr_ref, pk_ref, pv_ref, tail_ref, w0_ref, wup_ref, a0_ref, aup_ref, gup_ref,
                 kk_ref, ka_ref, rk_ref, lnw_ref, lnb_ref, o_ref, state_ref):
    j = pl.program_id(1)
    c = pr_ref.shape[0]
    n_heads = pr_ref.shape[1] // RWKV_HEAD

    @pl.when(j == 0)
    def _():
        state_ref[...] = jnp.zeros_like(state_ref)

    tail = tail_ref[...]
    pw = tail[:, :DECAY_LORA]
    pa = tail[:, DECAY_LORA:DECAY_LORA + ICLR_LORA]
    pg = tail[:, DECAY_LORA + ICLR_LORA:]
    pk = pk_ref[...]
    w_log = -_softplus(-(w0_ref[...] + _dot_x3(jnp.tanh(pw), wup_ref[...]))) - 0.5
    lw = -jnp.exp(w_log)
    a = jax.nn.sigmoid(a0_ref[...] + _dot_x3(pa, aup_ref[...]))
    row = lax.broadcasted_iota(I32, (c, c), 0)
    col = lax.broadcasted_iota(I32, (c, c), 1)
    strict = row > col
    incl = row >= col
    diag = row == col
    cum = _dot_exact_lhs(incl.astype(BF16), lw)
    k = pk * (1.0 + (a - 1.0) * ka_ref[...])
    kkraw = pk * kk_ref[...]
    g = _dot_x3(jax.nn.sigmoid(pg), gup_ref[...])

    levels = []
    shift = 0
    while (1 << shift) < c:
        levels.append((jnp.right_shift(row, shift + 1) == jnp.right_shift(col, shift + 1))
                      & ((jnp.right_shift(row, shift) & 1) == 1)
                      & ((jnp.right_shift(col, shift) & 1) == 0))
        shift += 1
    masks = (strict, incl, diag, levels)

    def heads(x):
        return jnp.stack([x[:, h * RWKV_HEAD:(h + 1) * RWKV_HEAD] for h in range(n_heads)], axis=0)

    r3, k3, v3 = heads(pr_ref[...]), heads(k), heads(pv_ref[...])
    y, s_new = _rwkv_heads(r3, k3, v3, heads(kkraw), heads(a), heads(cum), heads(lw),
                           state_ref[...], masks)
    state_ref[...] = s_new
    mu = jnp.mean(y, axis=-1, keepdims=True)
    var = jnp.mean(jnp.square(y - mu), axis=-1, keepdims=True)
    yn = (y - mu) * lax.rsqrt(var + LNX_EPS) * lnw_ref[...] + lnb_ref[...]
    bonus = jnp.sum(r3 * k3 * rk_ref[...], axis=-1, keepdims=True) * v3
    out = (yn + bonus) * heads(g)
    for h in range(n_heads):
        o_ref[:, h * RWKV_HEAD:(h + 1) * RWKV_HEAD] = out[h]


def _rwkv(sh, bsz, seq, width, params):
    (w0, w_up, a0, a_up, g_up, k_k, k_a, r_k, lnx_w, lnx_b) = params
    c = RWKV_CHUNK
    nc = seq // c
    heads = width // RWKV_HEAD
    tail_w = DECAY_LORA + ICLR_LORA + GATE_LORA
    tail_blk = (3 * width) // tail_w
    row = lambda b, j: b * nc + j
    vec = lambda a: a.reshape(1, width)
    per_head = lambda a: a.reshape(heads, 1, RWKV_HEAD)
    const = lambda shape: pl.BlockSpec(shape, lambda b, j: (0,) * len(shape))
    return pl.pallas_call(
        _rwkv_kernel,
        grid=(bsz, nc),
        in_specs=[pl.BlockSpec((c, width), lambda b, j: (row(b, j), 0)),
                  pl.BlockSpec((c, width), lambda b, j: (row(b, j), 1)),
                  pl.BlockSpec((c, width), lambda b, j: (row(b, j), 2)),
                  pl.BlockSpec((c, tail_w), lambda b, j: (row(b, j), tail_blk)),
                  const((1, width)), const((DECAY_LORA, width)),
                  const((1, width)), const((ICLR_LORA, width)), const((GATE_LORA, width)),
                  const((1, width)), const((1, width)),
                  const((heads, 1, RWKV_HEAD)), const((heads, 1, RWKV_HEAD)),
                  const((heads, 1, RWKV_HEAD))],
        out_specs=pl.BlockSpec((c, width), lambda b, j: (row(b, j), 0)),
        out_shape=jax.ShapeDtypeStruct((bsz * seq, width), F32),
        scratch_shapes=[pltpu.VMEM((heads, RWKV_HEAD, RWKV_HEAD), F32)],
        compiler_params=pltpu.CompilerParams(dimension_semantics=("arbitrary", "arbitrary"),
                                             vmem_limit_bytes=VMEM_LIMIT),
        name="rwkv",
    )(sh, sh, sh, sh, vec(w0), w_up, vec(a0), a_up, g_up, vec(k_k), vec(k_a), per_head(r_k),
      per_head(lnx_w), per_head(lnx_b))


def _head_rms(x, w):
    return x * lax.rsqrt(jnp.mean(x * x, axis=-1, keepdims=True) + NORM_EPS) * w


def _attn_kernel(sink_ref, q_ref, kvc_ref, kvp_ref, qw_ref, kw_ref, o_ref):
    blk = pl.program_id(1)
    q = q_ref[...]
    kvc = kvc_ref[...]
    kvp = kvp_ref[...]
    kv_w = ATTN_KV_HEADS * ATTN_HEAD
    qi = lax.broadcasted_iota(I32, (BLOCK, 2 * BLOCK), 0)
    kj = lax.broadcasted_iota(I32, (BLOCK, 2 * BLOCK), 1)
    rel = kj - qi
    valid = (rel >= BLOCK - WINDOW + 1) & (rel <= BLOCK) & (blk * BLOCK - BLOCK + kj >= 0)
    for g in range(ATTN_KV_HEADS):
        ks = slice(g * ATTN_HEAD, (g + 1) * ATTN_HEAD)
        vs = slice(kv_w + g * ATTN_HEAD, kv_w + (g + 1) * ATTN_HEAD)
        kband = _head_rms(jnp.concatenate([kvp[:, ks], kvc[:, ks]], axis=0), kw_ref[...])
        vband = jnp.concatenate([kvp[:, vs], kvc[:, vs]], axis=0)
        for n in range(ATTN_GROUP):
            hq = g * ATTN_GROUP + n
            qh = _head_rms(q[:, hq * ATTN_HEAD:(hq + 1) * ATTN_HEAD], qw_ref[...])
            s = jnp.where(valid, _nt(qh, kband) * ATTN_SCALE, NEG_INF)
            sink = sink_ref[hq]
            m = jnp.maximum(jnp.max(s, axis=-1, keepdims=True), sink)
            p = jnp.exp(s - m)
            denom = jnp.sum(p, axis=-1, keepdims=True) + jnp.exp(sink - m)
            o_ref[:, hq * ATTN_HEAD:(hq + 1) * ATTN_HEAD] = _nn(p, vband) / denom


def _attn(rest, bsz, seq, width, q_norm_w, k_norm_w, sinks):
    nb = seq // BLOCK
    kv_w = 2 * ATTN_KV_HEADS * ATTN_HEAD
    kv_blk = (3 * width) // kv_w
    return pl.pallas_call(
        _attn_kernel,
        grid=(bsz, nb),
        in_specs=[pl.BlockSpec(memory_space=pltpu.SMEM),
                  pl.BlockSpec((BLOCK, width), lambda b, j: (b * nb + j, 0)),
                  pl.BlockSpec((BLOCK, kv_w), lambda b, j: (b * nb + j, kv_blk)),
                  pl.BlockSpec((BLOCK, kv_w), lambda b, j: (b * nb + jnp.maximum(j - 1, 0), kv_blk)),
                  pl.BlockSpec((1, ATTN_HEAD), lambda b, j: (0, 0)),
                  pl.BlockSpec((1, ATTN_HEAD), lambda b, j: (0, 0))],
        out_specs=pl.BlockSpec((BLOCK, width), lambda b, j: (b * nb + j, 0)),
        out_shape=jax.ShapeDtypeStruct((bsz * seq, width), F32),
        compiler_params=pltpu.CompilerParams(dimension_semantics=("arbitrary", "arbitrary"),
                                             vmem_limit_bytes=VMEM_LIMIT),
        name="attn",
    )(sinks, rest, rest, rest, q_norm_w.reshape(1, ATTN_HEAD), k_norm_w.reshape(1, ATTN_HEAD))


def _outproj_kernel(ya_ref, yb_ref, ga_ref, gb_ref, x_ref, ada_ref, wo_ref, nw_ref, wq_ref,
                    k1_ref, k2_ref, x1_ref, h2_ref, sct_ref):
    ada = ada_ref[0]
    mixed = jax.nn.sigmoid(ga_ref[...]) * ya_ref[...] + jax.nn.sigmoid(gb_ref[...]) * yb_ref[...]
    x1 = x_ref[...] + ada[2:3] * _nn(mixed, wo_ref[...])
    x1_ref[...] = x1
    ms = jnp.mean(x1 * x1, axis=-1, keepdims=True)
    h2 = x1 * lax.rsqrt(ms + NORM_EPS) * nw_ref[...] * (1.0 + ada[4:5]) + ada[3:4]
    h2_ref[...] = h2
    q = _nn(h2, wq_ref[...])
    for ch in range(q.shape[1] // PEER_HALF):
        keys = k1_ref if ch % 2 == 0 else k2_ref
        cs = slice(ch * PEER_HALF, (ch + 1) * PEER_HALF)
        sct_ref[cs, :] = _nt(keys[...], q[:, cs])


def _outproj(ya, yb, rest, x2, ada3, w_out_bf, norm_w, wq_bf, keys_1, keys_2, seq):
    t, d = x2.shape
    tm = ROW_TILE
    tiles_per_seq = seq // tm
    qd = wq_bf.shape[1]
    rowblk = lambda c: pl.BlockSpec((tm, d), lambda i: (i, c))
    const = lambda shape: pl.BlockSpec(shape, lambda i: (0, 0))
    return pl.pallas_call(
        _outproj_kernel,
        grid=(t // tm,),
        in_specs=[rowblk(0), rowblk(0), rowblk(1), rowblk(2), rowblk(0),
                  pl.BlockSpec((1, N_ADA, d), lambda i: (i // tiles_per_seq, 0, 0)),
                  const((d, d)), const((1, d)), const((d, qd)),
                  const(keys_1.shape), const(keys_2.shape)],
        out_specs=[rowblk(0), rowblk(0), pl.BlockSpec((qd, tm), lambda i: (0, i))],
        out_shape=[jax.ShapeDtypeStruct((t, d), F32), jax.ShapeDtypeStruct((t, d), F32),
                   jax.ShapeDtypeStruct((qd, t), F32)],
        compiler_params=pltpu.CompilerParams(dimension_semantics=("arbitrary",),
                                             vmem_limit_bytes=VMEM_LIMIT),
        name="outproj",
    )(ya, yb, rest, rest, x2, ada3, w_out_bf, norm_w.reshape(1, d), wq_bf, keys_1, keys_2)


def _top16_rows(s, payload=None):
    n = s.shape[0]
    iota = lax.broadcasted_iota(I32, s.shape, 0).astype(F32)
    vals, picks = [], []
    for _ in range(PEER_TOPK):
        m = jnp.max(s, axis=0, keepdims=True)
        i = jnp.min(jnp.where(s == m, iota, float(n)), axis=0, keepdims=True)
        hit = iota == i
        s = jnp.where(hit, -jnp.inf, s)
        vals.append(m)
        picks.append(i if payload is None else
                     jnp.max(jnp.where(hit, payload, -1.0), axis=0, keepdims=True))
    return jnp.concatenate(vals, axis=0), jnp.concatenate(picks, axis=0)


def _topk_kernel(sct_ref, idx_ref, idx_tm_ref, gate_ref):
    keep = [PEER_TOPK // (i + 1) for i in range(PEER_TOPK)]
    pad = (-sum(keep)) % SUBLANES
    idx_all, gate_all = [], []
    for h in range(PEER_HEADS):
        base = h * 2 * N_KEYS
        v1, i1 = _top16_rows(sct_ref[base:base + N_KEYS, :])
        v2, i2 = _top16_rows(sct_ref[base + N_KEYS:base + 2 * N_KEYS, :])
        cand = [v1[i:i + 1, :] + v2[0:keep[i], :] for i in range(PEER_TOPK)]
        cidx = [i1[i:i + 1, :] * float(N_KEYS) + i2[0:keep[i], :] for i in range(PEER_TOPK)]
        if pad:
            cand.append(jnp.full((pad, v1.shape[1]), -jnp.inf, F32))
            cidx.append(jnp.zeros((pad, v1.shape[1]), F32))
        sc, idx = _top16_rows(jnp.concatenate(cand, axis=0), jnp.concatenate(cidx, axis=0))
        idx_all.append(idx.astype(I32))
        ex = jnp.exp(sc - sc[0:1, :])
        gate_all.append(ex / jnp.sum(ex, axis=0, keepdims=True))
    idx = jnp.concatenate(idx_all, axis=0)
    idx_ref[...] = idx
    idx_tm_ref[...] = idx.T
    gate_ref[...] = jnp.concatenate(gate_all, axis=0).T


def _topk(sct):
    qd, t = sct.shape
    tk = TOPK_TILE
    ne = PEER_HEADS * PEER_TOPK
    return pl.pallas_call(
        _topk_kernel,
        grid=(t // tk,),
        in_specs=[pl.BlockSpec((qd, tk), lambda i: (0, i))],
        out_specs=[pl.BlockSpec((ne, tk), lambda i: (0, i)),
                   pl.BlockSpec((tk, ne), lambda i: (i, 0)),
                   pl.BlockSpec((tk, ne), lambda i: (i, 0))],
        out_shape=[jax.ShapeDtypeStruct((ne, t), I32), jax.ShapeDtypeStruct((t, ne), I32),
                   jax.ShapeDtypeStruct((t, ne), F32)],
        compiler_params=pltpu.CompilerParams(dimension_semantics=("arbitrary",),
                                             vmem_limit_bytes=VMEM_LIMIT),
        name="topk",
    )(sct)


def _cast_kernel(x_ref, o_ref):
    o_ref[...] = x_ref[...].astype(BF16)


def _table_bf16(tab):
    n, d = tab.shape
    out = pl.pallas_call(
        _cast_kernel,
        grid=(n // PACK_TILE,),
        in_specs=[pl.BlockSpec((PACK_TILE, d), lambda i: (i, 0))],
        out_specs=pl.BlockSpec((PACK_TILE, d), lambda i: (i, 0)),
        out_shape=jax.ShapeDtypeStruct((n, d), BF16),
        name="table_cast",
    )(tab)
    return out.reshape(n, d // LANES, LANES), lax.bitcast_convert_type(out.reshape(n, d // 2, 2), I32)


def _gelu_exact(x):
    return 0.5 * x * (1.0 + lax.erf(x * (2.0 ** -0.5)))


def _group_mask(ne):
    sub = lax.broadcasted_iota(I32, (SUBLANES, ne * SUBLANES), 0)
    col = lax.broadcasted_iota(I32, (SUBLANES, ne * SUBLANES), 1)
    return sub == (col & (SUBLANES - 1))


def _gather_rows(idx_ref, tab_ref, t, rows_ref):
    picks = idx_ref.at[:, pl.ds(t, 1)]
    for e in range(idx_ref.shape[0]):
        rows_ref[e * SUBLANES:(e + 1) * SUBLANES, :] = tab_ref[picks[e, 0]]


def _peer_u_kernel(idx_ref, x_ref, gate_ref, tab_ref, wts_ref, *scratch):
    rows, d_s = scratch[:-1], scratch[-1]
    tb = x_ref.shape[0]
    ne = gate_ref.shape[1]
    mask = _group_mask(ne)

    def reduce(t, rows_ref):
        hi, lo = _split2(x_ref[t])
        lhs = jnp.concatenate([hi, lo], axis=0)
        z = lax.dot_general(lhs, rows_ref[...], (((1,), (1,)), ((), ())),
                            preferred_element_type=F32)
        zz = z[:SUBLANES] + z[SUBLANES:]
        d_s[pl.ds(t, 1), :] = jnp.sum(jnp.where(mask, zz, 0.0), axis=0, keepdims=True)

    def group(i, carry):
        for q, rows_ref in enumerate(rows):
            _gather_rows(idx_ref, tab_ref, len(rows) * i + q, rows_ref)
        for q, rows_ref in enumerate(rows):
            reduce(len(rows) * i + q, rows_ref)
        return carry

    lax.fori_loop(0, tb // len(rows), group, 0)
    r = lax.broadcasted_iota(I32, (ne * SUBLANES, ne), 0)
    c = lax.broadcasted_iota(I32, (ne * SUBLANES, ne), 1)
    fold = (jnp.right_shift(r, 3) == c).astype(BF16)
    dots = sum(jnp.dot(p, fold, preferred_element_type=F32) for p in _split3(d_s[...]))
    wts_ref[...] = gate_ref[...] * _gelu_exact(dots)


def _peer_u(idx_t, h3, gate, tab3, t):
    ne = gate.shape[1]
    tb = PEER_TILE
    return pl.pallas_call(
        _peer_u_kernel,
        grid=(t // tb,),
        in_specs=[pl.BlockSpec((ne, tb), lambda i: (0, i), memory_space=pltpu.SMEM),
                  pl.BlockSpec((tb, SUBLANES, LANES), lambda i: (i, 0, 0)),
                  pl.BlockSpec((tb, ne), lambda i: (i, 0)),
                  pl.BlockSpec(tab3.shape, lambda i: (0, 0, 0), pipeline_mode=pl.Buffered(1))],
        out_specs=pl.BlockSpec((tb, ne), lambda i: (i, 0)),
        out_shape=jax.ShapeDtypeStruct((t, ne), F32),
        scratch_shapes=[pltpu.VMEM((ne * SUBLANES, LANES), BF16)] * PEER_GROUP
                       + [pltpu.VMEM((tb, ne * SUBLANES), F32)],
        compiler_params=pltpu.CompilerParams(dimension_semantics=("arbitrary",),
                                             vmem_limit_bytes=VMEM_LIMIT),
        name="peer_u",
    )(idx_t, h3, gate, tab3)


def _peer_v_kernel(idx_ref, wts_ref, x1_ref, gt_ref, tab_ref, o_ref, *scratch):
    rows, w8_s = scratch[:-1], scratch[-1]
    tb = x1_ref.shape[0]
    ne = wts_ref.shape[1]
    mask = _group_mask(ne)
    gt = gt_ref[0]
    r = lax.broadcasted_iota(I32, (ne, ne * SUBLANES), 0)
    c = lax.broadcasted_iota(I32, (ne, ne * SUBLANES), 1)
    spread = (r == jnp.right_shift(c, 3)).astype(BF16)
    w8_s[...] = sum(jnp.dot(p, spread, preferred_element_type=F32) for p in _split3(wts_ref[...]))

    def combine(t, rows_ref):
        wm = jnp.where(mask, w8_s[pl.ds(t, 1), :], 0.0)
        hi, lo = _split2(wm)
        res = jnp.dot(jnp.concatenate([hi, lo], axis=0), rows_ref[...],
                      preferred_element_type=F32)
        o_ref[t] = x1_ref[t] + gt * (res[:SUBLANES] + res[SUBLANES:])

    def group(i, carry):
        for q, rows_ref in enumerate(rows):
            _gather_rows(idx_ref, tab_ref, len(rows) * i + q, rows_ref)
        for q, rows_ref in enumerate(rows):
            combine(len(rows) * i + q, rows_ref)
        return carry

    lax.fori_loop(0, tb // len(rows), group, 0)


def _peer_v(idx_t, wts, x13, gt3, tab3, seq):
    t, ne = wts.shape
    tb = PEER_TILE
    tiles_per_seq = seq // tb
    return pl.pallas_call(
        _peer_v_kernel,
        grid=(t // tb,),
        in_specs=[pl.BlockSpec((ne, tb), lambda i: (0, i), memory_space=pltpu.SMEM),
                  pl.BlockSpec((tb, ne), lambda i: (i, 0)),
                  pl.BlockSpec((tb, SUBLANES, LANES), lambda i: (i, 0, 0)),
                  pl.BlockSpec((1, SUBLANES, LANES), lambda i: (i // tiles_per_seq, 0, 0)),
                  pl.BlockSpec(tab3.shape, lambda i: (0, 0, 0), pipeline_mode=pl.Buffered(1))],
        out_specs=pl.BlockSpec((tb, SUBLANES, LANES), lambda i: (i, 0, 0)),
        out_shape=jax.ShapeDtypeStruct((t, SUBLANES, LANES), F32),
        scratch_shapes=[pltpu.VMEM((ne * SUBLANES, LANES), BF16)] * PEER_GROUP
                       + [pltpu.VMEM((tb, ne * SUBLANES), F32)],
        compiler_params=pltpu.CompilerParams(dimension_semantics=("arbitrary",),
                                             vmem_limit_bytes=VMEM_LIMIT),
        name="peer_v",
    )(idx_t, wts, x13, gt3, tab3)


def _sc_mesh():
    info = plsc.get_sparse_core_info()
    mesh = plsc.VectorSubcoreMesh(core_axis_name="c", subcore_axis_name="s")
    return mesh, info.num_cores * info.num_subcores, info.num_subcores


def _sc_gather_chunks(tab_hbm, idx_v, bufs, sems, n_chunks, consume):
    rows = bufs[0].shape[0]

    def start(ch):
        return pltpu.async_copy(tab_hbm.at[idx_v.at[pl.ds(ch * rows, rows)]],
                                bufs[ch % 2], sems[ch % 2])

    copy = start(0)
    for ch in range(n_chunks):
        nxt = start(ch + 1) if ch + 1 < n_chunks else None
        copy.wait()
        consume(ch, bufs[ch % 2])
        copy = nxt


def _sc_unpack(words):
    even = lax.bitcast_convert_type(jnp.left_shift(words, 16), F32)
    odd = lax.bitcast_convert_type(jnp.bitwise_and(words, -65536), F32)
    return even, odd


def _peer_u_sc(idx_tm, h2, utab_w, tok_base, tsc):
    ne = idx_tm.shape[1]
    dw = utab_w.shape[1]
    d = 2 * dw
    mesh, nw, nsub = _sc_mesh()
    tps = tsc // nw
    rows = SC_CHUNK_U
    assert tsc % nw == 0 and ne % rows == 0 and dw % SC_LANES == 0

    @functools.partial(
        pl.kernel, mesh=mesh, out_type=jax.ShapeDtypeStruct((tsc, ne * SC_LANES), F32),
        scratch_types=[pltpu.VMEM((ne,), I32), pltpu.VMEM((d,), F32),
                       pltpu.VMEM((dw,), F32), pltpu.VMEM((dw,), F32),
                       pltpu.VMEM((rows, dw), I32), pltpu.VMEM((rows, dw), I32),
                       pltpu.VMEM((ne * SC_LANES,), F32),
                       pltpu.SemaphoreType.DMA, pltpu.SemaphoreType.DMA],
        compiler_params=pltpu.CompilerParams(needs_layout_passes=False), name="peer_u_sc")
    def k(idx_hbm, x_hbm, u_hbm, o_hbm, idx_v, x_v, xe_v, xo_v, rows_a, rows_b, dp_v, sem_a, sem_b):
        wid = lax.axis_index("c") * nsub + lax.axis_index("s")
        zero = jnp.zeros((SC_LANES,), F32)
        lane2 = 2 * lax.iota(I32, SC_LANES)

        @pl.loop(0, tps)
        def _(t):
            loc = wid * tps + t
            pltpu.sync_copy(idx_hbm.at[tok_base + loc], idx_v)
            pltpu.sync_copy(x_hbm.at[tok_base + loc], x_v)
            for j in range(dw // SC_LANES):
                sl = pl.ds(j * SC_LANES, SC_LANES)
                xe_v[sl] = plsc.load_gather(x_v, [lane2 + 2 * SC_LANES * j])
                xo_v[sl] = plsc.load_gather(x_v, [lane2 + (2 * SC_LANES * j + 1)])
            for e in range(ne):
                dp_v[pl.ds(e * SC_LANES, SC_LANES)] = zero

            def consume(ch, rows_v):
                @pl.loop(0, dw, step=SC_LANES)
                def _(c):
                    sl = pl.ds(pl.multiple_of(c, SC_LANES), SC_LANES)
                    xe = xe_v[sl]
                    xo = xo_v[sl]

                    @plsc.parallel_loop(0, rows, 1, unroll=8)
                    def _(r):
                        even, odd = _sc_unpack(rows_v[r, sl])
                        dst = pl.ds(pl.multiple_of((ch * rows + r) * SC_LANES, SC_LANES), SC_LANES)
                        plsc.addupdate(dp_v.at[dst], even * xe + odd * xo)

            _sc_gather_chunks(u_hbm, idx_v, (rows_a, rows_b), (sem_a, sem_b), ne // rows, consume)
            pltpu.sync_copy(dp_v, o_hbm.at[loc])

    return k(idx_tm, h2, utab_w)


def _sc_weights_kernel(dp_ref, gate_ref, wts_tc_ref, o_ref, wts_tc_out_ref, wts_ref):
    del wts_tc_ref, wts_tc_out_ref
    ne = gate_ref.shape[1]
    r = lax.broadcasted_iota(I32, (ne * SC_LANES, ne), 0)
    c = lax.broadcasted_iota(I32, (ne * SC_LANES, ne), 1)
    fold = (jnp.right_shift(r, 4) == c).astype(BF16)
    dots = sum(jnp.dot(p, fold, preferred_element_type=F32) for p in _split3(dp_ref[...]))
    wts = gate_ref[...] * _gelu_exact(dots)
    wts_ref[...] = wts
    r = lax.broadcasted_iota(I32, (ne, ne * SC_LANES), 0)
    c = lax.broadcasted_iota(I32, (ne, ne * SC_LANES), 1)
    spread = (r == jnp.right_shift(c, 4)).astype(BF16)
    o_ref[...] = sum(jnp.dot(p, spread, preferred_element_type=F32) for p in _split3(wts))


def _sc_weights(dparts, gate, wts_tc, tok_base):
    tsc, w = dparts.shape
    ne = gate.shape[1]
    tb = ROW_TILE
    blk0 = tok_base // tb
    return pl.pallas_call(
        _sc_weights_kernel,
        grid=(tsc // tb,),
        in_specs=[pl.BlockSpec((tb, w), lambda i: (i, 0)),
                  pl.BlockSpec((tb, ne), lambda i: (i + blk0, 0)),
                  pl.BlockSpec(memory_space=pl.ANY)],
        out_specs=[pl.BlockSpec((tb, w), lambda i: (i, 0)),
                   pl.BlockSpec(memory_space=pl.ANY),
                   pl.BlockSpec((tb, ne), lambda i: (i, 0))],
        out_shape=[jax.ShapeDtypeStruct((tsc, w), F32),
                   jax.ShapeDtypeStruct(wts_tc.shape, wts_tc.dtype),
                   jax.ShapeDtypeStruct((tsc, ne), F32)],
        input_output_aliases={2: 1},
        name="sc_weights",
    )(dparts, gate, wts_tc)


def _peer_v_sc(idx_tm, w16, x1, gt2, vtab_w, seq, tok_base, tsc):
    w_off = w16.shape[0] - tsc
    ne = idx_tm.shape[1]
    dw = vtab_w.shape[1]
    d = 2 * dw
    mesh, nw, nsub = _sc_mesh()
    tps = tsc // nw
    shift = seq.bit_length() - 1
    rows = SC_CHUNK_V
    assert (1 << shift) == seq and tsc % nw == 0 and ne % rows == 0 and dw % SC_LANES == 0

    @functools.partial(
        pl.kernel, mesh=mesh, out_type=jax.ShapeDtypeStruct((tsc, d), F32),
        scratch_types=[pltpu.VMEM((ne,), I32), pltpu.VMEM((ne * SC_LANES,), F32),
                       pltpu.VMEM((rows, dw), I32), pltpu.VMEM((rows, dw), I32),
                       pltpu.VMEM((dw,), F32), pltpu.VMEM((dw,), F32),
                       pltpu.VMEM((d,), F32), pltpu.VMEM((d,), F32), pltpu.VMEM((d,), F32),
                       pltpu.SemaphoreType.DMA, pltpu.SemaphoreType.DMA],
        compiler_params=pltpu.CompilerParams(needs_layout_passes=False), name="peer_v_sc")
    def k(idx_hbm, w_hbm, x_hbm, gt_hbm, v_hbm, o_hbm, idx_v, w_v, rows_a, rows_b, acce_v, acco_v,
          x_v, gt_v, out_v, sem_a, sem_b):
        wid = lax.axis_index("c") * nsub + lax.axis_index("s")
        zero = jnp.zeros((SC_LANES,), F32)
        lane2 = 2 * lax.iota(I32, SC_LANES)

        @pl.loop(0, tps)
        def _(t):
            loc = wid * tps + t
            tok = tok_base + loc
            pltpu.sync_copy(idx_hbm.at[tok], idx_v)
            pltpu.sync_copy(w_hbm.at[w_off + loc], w_v)
            pltpu.sync_copy(x_hbm.at[tok], x_v)
            pltpu.sync_copy(gt_hbm.at[jnp.right_shift(tok, shift)], gt_v)
            for j in range(dw // SC_LANES):
                acce_v[pl.ds(j * SC_LANES, SC_LANES)] = zero
                acco_v[pl.ds(j * SC_LANES, SC_LANES)] = zero

            def consume(ch, rows_v):
                @pl.loop(0, rows)
                def _(e):
                    wv = w_v[pl.ds((ch * rows + e) * SC_LANES, SC_LANES)]

                    @plsc.parallel_loop(0, dw, SC_LANES, unroll=8)
                    def _(c):
                        sl = pl.ds(pl.multiple_of(c, SC_LANES), SC_LANES)
                        even, odd = _sc_unpack(rows_v[e, sl])
                        plsc.addupdate(acce_v.at[sl], wv * even)
                        plsc.addupdate(acco_v.at[sl], wv * odd)

            _sc_gather_chunks(v_hbm, idx_v, (rows_a, rows_b), (sem_a, sem_b), ne // rows, consume)
            for j in range(dw // SC_LANES):
                sl = pl.ds(j * SC_LANES, SC_LANES)
                ie = lane2 + 2 * SC_LANES * j
                io = ie + 1
                plsc.store_scatter(out_v, [ie], plsc.load_gather(x_v, [ie])
                                   + plsc.load_gather(gt_v, [ie]) * acce_v[sl])
                plsc.store_scatter(out_v, [io], plsc.load_gather(x_v, [io])
                                   + plsc.load_gather(gt_v, [io]) * acco_v[sl])
            pltpu.sync_copy(out_v, o_hbm.at[loc])

    return k(idx_tm, w16, x1, gt2, vtab_w)


def _perm_in_columns(width, kv_width):
    n_shift = 3 * width + DECAY_LORA + ICLR_LORA + GATE_LORA
    aq = n_shift
    akv = aq + width
    ga = akv + 2 * kv_width
    gb = ga + width
    return n_shift, np.concatenate([np.arange(0, n_shift), np.arange(aq, aq + width),
                                    np.arange(ga, ga + width), np.arange(gb, gb + width),
                                    np.arange(akv, akv + 2 * kv_width)])


def _layer(x, ada, norm1_w, norm2_w, w_in, shift_mu, w0, w_lora_up, a0, a_lora_up, g_lora_up,
           k_k, k_a, r_k, lnx_w, lnx_b, q_norm_w, k_norm_w, sinks, w_out, peer_w_q,
           peer_keys_1, peer_keys_2, peer_u, peer_v):
    bsz, seq, d = x.shape
    t = bsz * seq
    x2 = x.reshape(t, d)
    ada3 = ada.reshape(bsz, N_ADA, d)
    n_shift, perm = _perm_in_columns(d, ATTN_KV_HEADS * ATTN_HEAD)
    w_bf = w_in[:, perm].astype(BF16)
    sh, rest = _inproj(x2, ada3, norm1_w, w_bf, shift_mu, seq, n_shift)
    ya = _rwkv(sh, bsz, seq, d, (w0, w_lora_up, a0, a_lora_up, g_lora_up, k_k, k_a,
                                 r_k.reshape(-1), lnx_w, lnx_b))
    yb = _attn(rest, bsz, seq, d, q_norm_w, k_norm_w, sinks)
    x1, h2, sct = _outproj(ya, yb, rest, x2, ada3, w_out.astype(BF16), norm2_w,
                           peer_w_q.astype(BF16), peer_keys_1, peer_keys_2, seq)
    idx_t, idx_tm, gate = _topk(sct)
    sub = d // LANES
    gt2 = ada3[:, 5, :]
    tu, tv = t - SC_TOKENS_U, t - SC_TOKENS_V
    u3, uw = _table_bf16(peer_u)
    v3, vw = _table_bf16(peer_v)
    wts_tc = _peer_u(idx_t, h2.reshape(t, sub, LANES), gate, u3, tu)
    dparts = _peer_u_sc(idx_tm, h2, uw, tu, SC_TOKENS_U)
    w16, wts_tc, wts_sc = _sc_weights(dparts, gate, wts_tc, tu)
    wts = jnp.concatenate([wts_tc, wts_sc[:tv - tu]], axis=0)
    out_tc = _peer_v(idx_t, wts, x1.reshape(t, sub, LANES), gt2.reshape(bsz, sub, LANES), v3, seq)
    out_sc = _peer_v_sc(idx_tm, w16, x1, gt2, vw, seq, tv, SC_TOKENS_V)
    return jnp.concatenate([out_tc.reshape(tv, d), out_sc], axis=0).reshape(bsz, seq, d)


def kernel(x, c, ada_w, ada_b, norm1_w, norm2_w, w_in, shift_mu, w0, w_lora_up, a0, a_lora_up, g_lora_up, k_k, k_a, r_k, lnx_w, lnx_b, q_norm_w, k_norm_w, sinks, w_out, peer_w_q, peer_keys_1, peer_keys_2, peer_u, peer_v):
    depth = ada_w.shape[0]
    for l in range(depth):
        ada = _ada(c, ada_w[l], ada_b[l])
        x = _layer(x, ada, norm1_w[l], norm2_w[l], w_in[l], shift_mu[l], w0[l], w_lora_up[l],
                   a0[l], a_lora_up[l], g_lora_up[l], k_k[l], k_a[l], r_k[l], lnx_w[l], lnx_b[l],
                   q_norm_w[l], k_norm_w[l], sinks[l], w_out[l], peer_w_q[l], peer_keys_1[l],
                   peer_keys_2[l], peer_u[l], peer_v[l])
    return x
```

```python
import functools

import numpy as np
import jax
import jax.numpy as jnp
from jax import lax
from jax.experimental import pallas as pl
from jax.experimental.pallas import tpu as pltpu
from jax.experimental.pallas import tpu_sc as plsc

F32 = jnp.float32
BF16 = jnp.bfloat16
I32 = jnp.int32
HIGHEST = lax.Precision.HIGHEST

RWKV_HEAD = 64
DECAY_LORA = 64
ICLR_LORA = 64
GATE_LORA = 128
LNX_EPS = 64e-5
ATTN_HEAD = 64
ATTN_Q_HEADS = 16
ATTN_KV_HEADS = 2
ATTN_GROUP = ATTN_Q_HEADS // ATTN_KV_HEADS
WINDOW = 128
BLOCK = 128
ATTN_SCALE = ATTN_HEAD ** -0.5
NEG_INF = -1e30
N_KEYS = 128
PEER_HEADS = 8
PEER_HALF = 128
PEER_TOPK = 16
NORM_EPS = 1e-6
N_ADA = 6

LANES = 128
SUBLANES = 8
VMEM_LIMIT = 56 * 1024 * 1024

RWKV_CHUNK = 64
ROW_TILE = 256
TOPK_TILE = 128
PEER_TILE = 128
PEER_GROUP = 16
SC_TOKENS_U = 24832
SC_TOKENS_V = 20736
SC_CHUNK_U = 64
SC_CHUNK_V = 32
SC_LANES = 16
PACK_TILE = 512


def _nt(a, b):
    return lax.dot_general(a.astype(BF16), b.astype(BF16), (((1,), (1,)), ((), ())),
                           preferred_element_type=F32)


def _nn(a, b):
    return jnp.dot(a.astype(BF16), b.astype(BF16), preferred_element_type=F32)


def _dot_f32(a, b):
    return jnp.dot(a, b, precision=HIGHEST, preferred_element_type=F32)


def _bnt(a, b):
    return lax.dot_general(a.astype(BF16), b.astype(BF16), (((2,), (2,)), ((0,), (0,))),
                           preferred_element_type=F32)


def _bnn(a, b):
    return lax.dot_general(a.astype(BF16), b.astype(BF16), (((2,), (1,)), ((0,), (0,))),
                           preferred_element_type=F32)


def _btn(a, b):
    return lax.dot_general(a.astype(BF16), b.astype(BF16), (((1,), (1,)), ((0,), (0,))),
                           preferred_element_type=F32)


def _split2(x):
    hi = x.astype(BF16)
    lo = (x - hi.astype(F32)).astype(BF16)
    return hi, lo


def _split3(x):
    hi = x.astype(BF16)
    r1 = x - hi.astype(F32)
    mid = r1.astype(BF16)
    lo = (r1 - mid.astype(F32)).astype(BF16)
    return hi, mid, lo


def _dot_x3(a, b):
    ah, al = _split2(a)
    bh, bl = _split2(b)
    d = lambda p, q: jnp.dot(p, q, preferred_element_type=F32)
    return d(ah, bh) + (d(ah, bl) + d(al, bh))


def _dot_exact_lhs(a_bf, b):
    return sum(jnp.dot(a_bf, p, preferred_element_type=F32) for p in _split3(b))


def _ada_kernel(c_ref, w_ref, b_ref, o_ref):
    c = c_ref[...]
    cond = c * jax.nn.sigmoid(c)
    o_ref[...] = _dot_f32(cond, w_ref[...]) + b_ref[...]


def _ada(c, ada_w, ada_b):
    bsz, d = c.shape
    n = ada_w.shape[1]
    return pl.pallas_call(
        _ada_kernel,
        grid=(n // d,),
        in_specs=[pl.BlockSpec((bsz, d), lambda j: (0, 0)),
                  pl.BlockSpec((d, d), lambda j: (0, j)),
                  pl.BlockSpec((1, d), lambda j: (0, j))],
        out_specs=pl.BlockSpec((bsz, d), lambda j: (0, j)),
        out_shape=jax.ShapeDtypeStruct((bsz, n), F32),
        name="ada",
    )(c, ada_w, ada_b.reshape(1, n))


def _col_chunks(width, step):
    return [(c0, min(step, width - c0)) for c0 in range(0, width, step)]


def _inproj_kernel(x_ref, ada_ref, nw_ref, w_ref, mu_ref, sh_ref, rest_ref, carry_ref,
                   *, n_shift, n_rest, tiles_per_seq):
    i = pl.program_id(0)
    x = x_ref[...]
    tm = x.shape[0]
    ms = jnp.mean(x * x, axis=-1, keepdims=True)
    ada = ada_ref[0]
    h = x * lax.rsqrt(ms + NORM_EPS) * nw_ref[...] * (1.0 + ada[1:2]) + ada[0:1]
    hb = h.astype(BF16)
    row0 = lax.broadcasted_iota(I32, (tm, 1), 0) == 0
    seq_start = (i % tiles_per_seq) == 0
    for c0, cw in _col_chunks(n_shift, 1024):
        p = jnp.dot(hb, w_ref[:, c0:c0 + cw], preferred_element_type=F32)
        prev_last = jnp.where(seq_start, 0.0, carry_ref[:, c0:c0 + cw])
        carry_ref[:, c0:c0 + cw] = p[tm - 1:tm, :]
        prev = jnp.where(row0, prev_last, pltpu.roll(p, 1, axis=0))
        sh_ref[:, c0:c0 + cw] = p + (prev - p) * mu_ref[:, c0:c0 + cw]
    for c0, cw in _col_chunks(n_rest, 1024):
        rest_ref[:, c0:c0 + cw] = jnp.dot(hb, w_ref[:, n_shift + c0:n_shift + c0 + cw],
                                          preferred_element_type=F32)


def _inproj(x2, ada3, norm_w, w_bf, mu, seq, n_shift):
    t, d = x2.shape
    n_rest = w_bf.shape[1] - n_shift
    tm = ROW_TILE
    tiles_per_seq = seq // tm
    kern = functools.partial(_inproj_kernel, n_shift=n_shift, n_rest=n_rest,
                             tiles_per_seq=tiles_per_seq)
    return pl.pallas_call(
        kern,
        grid=(t // tm,),
        in_specs=[pl.BlockSpec((tm, d), lambda i: (i, 0)),
                  pl.BlockSpec((1, N_ADA, d), lambda i: (i // tiles_per_seq, 0, 0)),
                  pl.BlockSpec((1, d), lambda i: (0, 0)),
                  pl.BlockSpec(w_bf.shape, lambda i: (0, 0), pipeline_mode=pl.Buffered(1)),
                  pl.BlockSpec((1, n_shift), lambda i: (0, 0))],
        out_specs=[pl.BlockSpec((tm, n_shift), lambda i: (i, 0)),
                   pl.BlockSpec((tm, n_rest), lambda i: (i, 0))],
        out_shape=[jax.ShapeDtypeStruct((t, n_shift), F32),
                   jax.ShapeDtypeStruct((t, n_rest), F32)],
        scratch_shapes=[pltpu.VMEM((1, n_shift), F32)],
        compiler_params=pltpu.CompilerParams(dimension_semantics=("arbitrary",),
                                             vmem_limit_bytes=VMEM_LIMIT),
        name="inproj",
    )(x2, ada3, norm_w.reshape(1, d), w_bf, mu.reshape(1, n_shift))


def _softplus(z):
    return jnp.maximum(z, 0.0) + jnp.log1p(jnp.exp(-jnp.abs(z)))


def _rwkv_heads(r, k, v, kkraw, a, cum, lw, s0, masks):
    strict, incl, diag, levels = masks
    c = r.shape[1]
    nrm = jnp.sqrt(jnp.sum(kkraw * kkraw, axis=-1, keepdims=True))
    kk = kkraw / jnp.maximum(nrm, 1e-12)
    w_in = jnp.exp(cum)
    w_ex = jnp.exp(cum - lw)
    w_inv = jnp.exp(-cum)
    w_end = jnp.exp(cum[:, c - 1:c, :] - cum)
    bb = kk * a
    rt = r * w_in
    at = -kk * w_ex
    bt = bb * w_inv
    kt = k * w_inv
    ab = _bnt(at, bt)
    ak = _bnt(at, kt)
    rb = _bnt(rt, bt)
    rk = _bnt(rt, kt)
    lab = jnp.where(strict, ab, 0.0)
    inv = jnp.where(levels[0], lab, 0.0) + jnp.where(diag, 1.0, 0.0)
    for lvl in levels[1:]:
        inv = inv + _bnn(_bnn(inv, jnp.where(lvl, lab, 0.0)), inv)
    rhs = _bnt(at, s0) + _bnn(jnp.where(strict, ak, 0.0), v)
    u = _bnn(inv, rhs)
    y = _bnt(rt, s0) + _bnn(jnp.where(incl, rb, 0.0), u) + _bnn(jnp.where(incl, rk, 0.0), v)
    s_new = s0 * w_in[:, c - 1:c, :] + _btn(u, bb * w_end) + _btn(v, k * w_end)
    return y, s_new


def _rwkv_kernel(pr_ref, pk_ref, pv_ref, tail_ref, w0_ref, wup_ref, a0_ref, aup_ref, gup_ref,
                 kk_ref, ka_ref, rk_ref, lnw_ref, lnb_ref, o_ref, state_ref):
    j = pl.program_id(1)
    c = pr_ref.shape[0]
    n_heads = pr_ref.shape[1] // RWKV_HEAD

    @pl.when(j == 0)
    def _():
        state_ref[...] = jnp.zeros_like(state_ref)

    tail = tail_ref[...]
    pw = tail[:, :DECAY_LORA]
    pa = tail[:, DECAY_LORA:DECAY_LORA + ICLR_LORA]
    pg = tail[:, DECAY_LORA + ICLR_LORA:]
    pk = pk_ref[...]
    w_log = -_softplus(-(w0_ref[...] + _dot_x3(jnp.tanh(pw), wup_ref[...]))) - 0.5
    lw = -jnp.exp(w_log)
    a = jax.nn.sigmoid(a0_ref[...] + _dot_x3(pa, aup_ref[...]))
    row = lax.broadcasted_iota(I32, (c, c), 0)
    col = lax.broadcasted_iota(I32, (c, c), 1)
    strict = row > col
    incl = row >= col
    diag = row == col
    cum = _dot_exact_lhs(incl.astype(BF16), lw)
    k = pk * (1.0 + (a - 1.0) * ka_ref[...])
    kkraw = pk * kk_ref[...]
    g = _dot_x3(jax.nn.sigmoid(pg), gup_ref[...])

    levels = []
    shift = 0
    while (1 << shift) < c:
        levels.append((jnp.right_shift(row, shift + 1) == jnp.right_shift(col, shift + 1))
                      & ((jnp.right_shift(row, shift) & 1) == 1)
                      & ((jnp.right_shift(col, shift) & 1) == 0))
        shift += 1
    masks = (strict, incl, diag, levels)

    def heads(x):
        return jnp.stack([x[:, h * RWKV_HEAD:(h + 1) * RWKV_HEAD] for h in range(n_heads)], axis=0)

    r3, k3, v3 = heads(pr_ref[...]), heads(k), heads(pv_ref[...])
    y, s_new = _rwkv_heads(r3, k3, v3, heads(kkraw), heads(a), heads(cum), heads(lw),
                           state_ref[...], masks)
    state_ref[...] = s_new
    mu = jnp.mean(y, axis=-1, keepdims=True)
    var = jnp.mean(jnp.square(y - mu), axis=-1, keepdims=True)
    yn = (y - mu) * lax.rsqrt(var + LNX_EPS) * lnw_ref[...] + lnb_ref[...]
    bonus = jnp.sum(r3 * k3 * rk_ref[...], axis=-1, keepdims=True) * v3
    out = (yn + bonus) * heads(g)
    for h in range(n_heads):
        o_ref[:, h * RWKV_HEAD:(h + 1) * RWKV_HEAD] = out[h]


def _rwkv(sh, bsz, seq, width, params):
    (w0, w_up, a0, a_up, g_up, k_k, k_a, r_k, lnx_w, lnx_b) = params
    c = RWKV_CHUNK
    nc = seq // c
    heads = width // RWKV_HEAD
    tail_w = DECAY_LORA + ICLR_LORA + GATE_LORA
    tail_blk = (3 * width) // tail_w
    row = lambda b, j: b * nc + j
    vec = lambda a: a.reshape(1, width)
    per_head = lambda a: a.reshape(heads, 1, RWKV_HEAD)
    const = lambda shape: pl.BlockSpec(shape, lambda b, j: (0,) * len(shape))
    return pl.pallas_call(
        _rwkv_kernel,
        grid=(bsz, nc),
        in_specs=[pl.BlockSpec((c, width), lambda b, j: (row(b, j), 0)),
                  pl.BlockSpec((c, width), lambda b, j: (row(b, j), 1)),
                  pl.BlockSpec((c, width), lambda b, j: (row(b, j), 2)),
                  pl.BlockSpec((c, tail_w), lambda b, j: (row(b, j), tail_blk)),
                  const((1, width)), const((DECAY_LORA, width)),
                  const((1, width)), const((ICLR_LORA, width)), const((GATE_LORA, width)),
                  const((1, width)), const((1, width)),
                  const((heads, 1, RWKV_HEAD)), const((heads, 1, RWKV_HEAD)),
                  const((heads, 1, RWKV_HEAD))],
        out_specs=pl.BlockSpec((c, width), lambda b, j: (row(b, j), 0)),
        out_shape=jax.ShapeDtypeStruct((bsz * seq, width), F32),
        scratch_shapes=[pltpu.VMEM((heads, RWKV_HEAD, RWKV_HEAD), F32)],
        compiler_params=pltpu.CompilerParams(dimension_semantics=("arbitrary", "arbitrary"),
                                             vmem_limit_bytes=VMEM_LIMIT),
        name="rwkv",
    )(sh, sh, sh, sh, vec(w0), w_up, vec(a0), a_up, g_up, vec(k_k), vec(k_a), per_head(r_k),
      per_head(lnx_w), per_head(lnx_b))


def _head_rms(x, w):
    return x * lax.rsqrt(jnp.mean(x * x, axis=-1, keepdims=True) + NORM_EPS) * w


def _attn_kernel(sink_ref, q_ref, kvc_ref, kvp_ref, qw_ref, kw_ref, o_ref):
    blk = pl.program_id(1)
    q = q_ref[...]
    kvc = kvc_ref[...]
    kvp = kvp_ref[...]
    kv_w = ATTN_KV_HEADS * ATTN_HEAD
    qi = lax.broadcasted_iota(I32, (BLOCK, 2 * BLOCK), 0)
    kj = lax.broadcasted_iota(I32, (BLOCK, 2 * BLOCK), 1)
    rel = kj - qi
    valid = (rel >= BLOCK - WINDOW + 1) & (rel <= BLOCK) & (blk * BLOCK - BLOCK + kj >= 0)
    for g in range(ATTN_KV_HEADS):
        ks = slice(g * ATTN_HEAD, (g + 1) * ATTN_HEAD)
        vs = slice(kv_w + g * ATTN_HEAD, kv_w + (g + 1) * ATTN_HEAD)
        kband = _head_rms(jnp.concatenate([kvp[:, ks], kvc[:, ks]], axis=0), kw_ref[...])
        vband = jnp.concatenate([kvp[:, vs], kvc[:, vs]], axis=0)
        for n in range(ATTN_GROUP):
            hq = g * ATTN_GROUP + n
            qh = _head_rms(q[:, hq * ATTN_HEAD:(hq + 1) * ATTN_HEAD], qw_ref[...])
            s = jnp.where(valid, _nt(qh, kband) * ATTN_SCALE, NEG_INF)
            sink = sink_ref[hq]
            m = jnp.maximum(jnp.max(s, axis=-1, keepdims=True), sink)
            p = jnp.exp(s - m)
            denom = jnp.sum(p, axis=-1, keepdims=True) + jnp.exp(sink - m)
            o_ref[:, hq * ATTN_HEAD:(hq + 1) * ATTN_HEAD] = _nn(p, vband) / denom


def _attn(rest, bsz, seq, width, q_norm_w, k_norm_w, sinks):
    nb = seq // BLOCK
    kv_w = 2 * ATTN_KV_HEADS * ATTN_HEAD
    kv_blk = (3 * width) // kv_w
    return pl.pallas_call(
        _attn_kernel,
        grid=(bsz, nb),
        in_specs=[pl.BlockSpec(memory_space=pltpu.SMEM),
                  pl.BlockSpec((BLOCK, width), lambda b, j: (b * nb + j, 0)),
                  pl.BlockSpec((BLOCK, kv_w), lambda b, j: (b * nb + j, kv_blk)),
                  pl.BlockSpec((BLOCK, kv_w), lambda b, j: (b * nb + jnp.maximum(j - 1, 0), kv_blk)),
                  pl.BlockSpec((1, ATTN_HEAD), lambda b, j: (0, 0)),
                  pl.BlockSpec((1, ATTN_HEAD), lambda b, j: (0, 0))],
        out_specs=pl.BlockSpec((BLOCK, width), lambda b, j: (b * nb + j, 0)),
        out_shape=jax.ShapeDtypeStruct((bsz * seq, width), F32),
        compiler_params=pltpu.CompilerParams(dimension_semantics=("arbitrary", "arbitrary"),
                                             vmem_limit_bytes=VMEM_LIMIT),
        name="attn",
    )(sinks, rest, rest, rest, q_norm_w.reshape(1, ATTN_HEAD), k_norm_w.reshape(1, ATTN_HEAD))


def _outproj_kernel(ya_ref, yb_ref, ga_ref, gb_ref, x_ref, ada_ref, wo_ref, nw_ref, wq_ref,
                    k1_ref, k2_ref, x1_ref, h2_ref, sct_ref):
    ada = ada_ref[0]
    mixed = jax.nn.sigmoid(ga_ref[...]) * ya_ref[...] + jax.nn.sigmoid(gb_ref[...]) * yb_ref[...]
    x1 = x_ref[...] + ada[2:3] * _nn(mixed, wo_ref[...])
    x1_ref[...] = x1
    ms = jnp.mean(x1 * x1, axis=-1, keepdims=True)
    h2 = x1 * lax.rsqrt(ms + NORM_EPS) * nw_ref[...] * (1.0 + ada[4:5]) + ada[3:4]
    h2_ref[...] = h2
    q = _nn(h2, wq_ref[...])
    for ch in range(q.shape[1] // PEER_HALF):
        keys = k1_ref if ch % 2 == 0 else k2_ref
        cs = slice(ch * PEER_HALF, (ch + 1) * PEER_HALF)
        sct_ref[cs, :] = _nt(keys[...], q[:, cs])


def _outproj(ya, yb, rest, x2, ada3, w_out_bf, norm_w, wq_bf, keys_1, keys_2, seq):
    t, d = x2.shape
    tm = ROW_TILE
    tiles_per_seq = seq // tm
    qd = wq_bf.shape[1]
    rowblk = lambda c: pl.BlockSpec((tm, d), lambda i: (i, c))
    const = lambda shape: pl.BlockSpec(shape, lambda i: (0, 0))
    return pl.pallas_call(
        _outproj_kernel,
        grid=(t // tm,),
        in_specs=[rowblk(0), rowblk(0), rowblk(1), rowblk(2), rowblk(0),
                  pl.BlockSpec((1, N_ADA, d), lambda i: (i // tiles_per_seq, 0, 0)),
                  const((d, d)), const((1, d)), const((d, qd)),
                  const(keys_1.shape), const(keys_2.shape)],
        out_specs=[rowblk(0), rowblk(0), pl.BlockSpec((qd, tm), lambda i: (0, i))],
        out_shape=[jax.ShapeDtypeStruct((t, d), F32), jax.ShapeDtypeStruct((t, d), F32),
                   jax.ShapeDtypeStruct((qd, t), F32)],
        compiler_params=pltpu.CompilerParams(dimension_semantics=("arbitrary",),
                                             vmem_limit_bytes=VMEM_LIMIT),
        name="outproj",
    )(ya, yb, rest, rest, x2, ada3, w_out_bf, norm_w.reshape(1, d), wq_bf, keys_1, keys_2)


def _top16_rows(s, payload=None):
    n = s.shape[0]
    iota = lax.broadcasted_iota(I32, s.shape, 0).astype(F32)
    vals, picks = [], []
    for _ in range(PEER_TOPK):
        m = jnp.max(s, axis=0, keepdims=True)
        i = jnp.min(jnp.where(s == m, iota, float(n)), axis=0, keepdims=True)
        hit = iota == i
        s = jnp.where(hit, -jnp.inf, s)
        vals.append(m)
        picks.append(i if payload is None else
                     jnp.max(jnp.where(hit, payload, -1.0), axis=0, keepdims=True))
    return jnp.concatenate(vals, axis=0), jnp.concatenate(picks, axis=0)


def _topk_kernel(sct_ref, idx_ref, idx_tm_ref, gate_ref):
    keep = [PEER_TOPK // (i + 1) for i in range(PEER_TOPK)]
    pad = (-sum(keep)) % SUBLANES
    idx_all, gate_all = [], []
    for h in range(PEER_HEADS):
        base = h * 2 * N_KEYS
        v1, i1 = _top16_rows(sct_ref[base:base + N_KEYS, :])
        v2, i2 = _top16_rows(sct_ref[base + N_KEYS:base + 2 * N_KEYS, :])
        cand = [v1[i:i + 1, :] + v2[0:keep[i], :] for i in range(PEER_TOPK)]
        cidx = [i1[i:i + 1, :] * float(N_KEYS) + i2[0:keep[i], :] for i in range(PEER_TOPK)]
        if pad:
            cand.append(jnp.full((pad, v1.shape[1]), -jnp.inf, F32))
            cidx.append(jnp.zeros((pad, v1.shape[1]), F32))
        sc, idx = _top16_rows(jnp.concatenate(cand, axis=0), jnp.concatenate(cidx, axis=0))
        idx_all.append(idx.astype(I32))
        ex = jnp.exp(sc - sc[0:1, :])
        gate_all.append(ex / jnp.sum(ex, axis=0, keepdims=True))
    idx = jnp.concatenate(idx_all, axis=0)
    idx_ref[...] = idx
    idx_tm_ref[...] = idx.T
    gate_ref[...] = jnp.concatenate(gate_all, axis=0).T


def _topk(sct):
    qd, t = sct.shape
    tk = TOPK_TILE
    ne = PEER_HEADS * PEER_TOPK
    return pl.pallas_call(
        _topk_kernel,
        grid=(t // tk,),
        in_specs=[pl.BlockSpec((qd, tk), lambda i: (0, i))],
        out_specs=[pl.BlockSpec((ne, tk), lambda i: (0, i)),
                   pl.BlockSpec((tk, ne), lambda i: (i, 0)),
                   pl.BlockSpec((tk, ne), lambda i: (i, 0))],
        out_shape=[jax.ShapeDtypeStruct((ne, t), I32), jax.ShapeDtypeStruct((t, ne), I32),
                   jax.ShapeDtypeStruct((t, ne), F32)],
        compiler_params=pltpu.CompilerParams(dimension_semantics=("arbitrary",),
                                             vmem_limit_bytes=VMEM_LIMIT),
        name="topk",
    )(sct)


def _cast_kernel(x_ref, o_ref):
    o_ref[...] = x_ref[...].astype(BF16)


def _table_bf16(tab):
    n, d = tab.shape
    out = pl.pallas_call(
        _cast_kernel,
        grid=(n // PACK_TILE,),
        in_specs=[pl.BlockSpec((PACK_TILE, d), lambda i: (i, 0))],
        out_specs=pl.BlockSpec((PACK_TILE, d), lambda i: (i, 0)),
        out_shape=jax.ShapeDtypeStruct((n, d), BF16),
        name="table_cast",
    )(tab)
    return out.reshape(n, d // LANES, LANES), lax.bitcast_convert_type(out.reshape(n, d // 2, 2), I32)


def _gelu_exact(x):
    return 0.5 * x * (1.0 + lax.erf(x * (2.0 ** -0.5)))


def _group_mask(ne):
    sub = lax.broadcasted_iota(I32, (SUBLANES, ne * SUBLANES), 0)
    col = lax.broadcasted_iota(I32, (SUBLANES, ne * SUBLANES), 1)
    return sub == (col & (SUBLANES - 1))


def _gather_rows(idx_ref, tab_ref, t, rows_ref):
    picks = idx_ref.at[:, pl.ds(t, 1)]
    for e in range(idx_ref.shape[0]):
        rows_ref[e * SUBLANES:(e + 1) * SUBLANES, :] = tab_ref[picks[e, 0]]


def _peer_u_kernel(idx_ref, x_ref, gate_ref, tab_ref, wts_ref, *scratch):
    rows, d_s = scratch[:-1], scratch[-1]
    tb = x_ref.shape[0]
    ne = gate_ref.shape[1]
    mask = _group_mask(ne)

    def reduce(t, rows_ref):
        hi, lo = _split2(x_ref[t])
        lhs = jnp.concatenate([hi, lo], axis=0)
        z = lax.dot_general(lhs, rows_ref[...], (((1,), (1,)), ((), ())),
                            preferred_element_type=F32)
        zz = z[:SUBLANES] + z[SUBLANES:]
        d_s[pl.ds(t, 1), :] = jnp.sum(jnp.where(mask, zz, 0.0), axis=0, keepdims=True)

    def group(i, carry):
        for q, rows_ref in enumerate(rows):
            _gather_rows(idx_ref, tab_ref, len(rows) * i + q, rows_ref)
        for q, rows_ref in enumerate(rows):
            reduce(len(rows) * i + q, rows_ref)
        return carry

    lax.fori_loop(0, tb // len(rows), group, 0)
    r = lax.broadcasted_iota(I32, (ne * SUBLANES, ne), 0)
    c = lax.broadcasted_iota(I32, (ne * SUBLANES, ne), 1)
    fold = (jnp.right_shift(r, 3) == c).astype(BF16)
    dots = sum(jnp.dot(p, fold, preferred_element_type=F32) for p in _split3(d_s[...]))
    wts_ref[...] = gate_ref[...] * _gelu_exact(dots)


def _peer_u(idx_t, h3, gate, tab3, t):
    ne = gate.shape[1]
    tb = PEER_TILE
    return pl.pallas_call(
        _peer_u_kernel,
        grid=(t // tb,),
        in_specs=[pl.BlockSpec((ne, tb), lambda i: (0, i), memory_space=pltpu.SMEM),
                  pl.BlockSpec((tb, SUBLANES, LANES), lambda i: (i, 0, 0)),
                  pl.BlockSpec((tb, ne), lambda i: (i, 0)),
                  pl.BlockSpec(tab3.shape, lambda i: (0, 0, 0), pipeline_mode=pl.Buffered(1))],
        out_specs=pl.BlockSpec((tb, ne), lambda i: (i, 0)),
        out_shape=jax.ShapeDtypeStruct((t, ne), F32),
        scratch_shapes=[pltpu.VMEM((ne * SUBLANES, LANES), BF16)] * PEER_GROUP
                       + [pltpu.VMEM((tb, ne * SUBLANES), F32)],
        compiler_params=pltpu.CompilerParams(dimension_semantics=("arbitrary",),
                                             vmem_limit_bytes=VMEM_LIMIT),
        name="peer_u",
    )(idx_t, h3, gate, tab3)


def _peer_v_kernel(idx_ref, wts_ref, x1_ref, gt_ref, tab_ref, o_ref, *scratch):
    rows, w8_s = scratch[:-1], scratch[-1]
    tb = x1_ref.shape[0]
    ne = wts_ref.shape[1]
    mask = _group_mask(ne)
    gt = gt_ref[0]
    r = lax.broadcasted_iota(I32, (ne, ne * SUBLANES), 0)
    c = lax.broadcasted_iota(I32, (ne, ne * SUBLANES), 1)
    spread = (r == jnp.right_shift(c, 3)).astype(BF16)
    w8_s[...] = sum(jnp.dot(p, spread, preferred_element_type=F32) for p in _split3(wts_ref[...]))

    def combine(t, rows_ref):
        wm = jnp.where(mask, w8_s[pl.ds(t, 1), :], 0.0)
        hi, lo = _split2(wm)
        res = jnp.dot(jnp.concatenate([hi, lo], axis=0), rows_ref[...],
                      preferred_element_type=F32)
        o_ref[t] = x1_ref[t] + gt * (res[:SUBLANES] + res[SUBLANES:])

    def group(i, carry):
        for q, rows_ref in enumerate(rows):
            _gather_rows(idx_ref, tab_ref, len(rows) * i + q, rows_ref)
        for q, rows_ref in enumerate(rows):
            combine(len(rows) * i + q, rows_ref)
        return carry

    lax.fori_loop(0, tb // len(rows), group, 0)


def _peer_v(idx_t, wts, x13, gt3, tab3, seq):
    t, ne = wts.shape
    tb = PEER_TILE
    tiles_per_seq = seq // tb
    return pl.pallas_call(
        _peer_v_kernel,
        grid=(t // tb,),
        in_specs=[pl.BlockSpec((ne, tb), lambda i: (0, i), memory_space=pltpu.SMEM),
                  pl.BlockSpec((tb, ne), lambda i: (i, 0)),
                  pl.BlockSpec((tb, SUBLANES, LANES), lambda i: (i, 0, 0)),
                  pl.BlockSpec((1, SUBLANES, LANES), lambda i: (i // tiles_per_seq, 0, 0)),
                  pl.BlockSpec(tab3.shape, lambda i: (0, 0, 0), pipeline_mode=pl.Buffered(1))],
        out_specs=pl.BlockSpec((tb, SUBLANES, LANES), lambda i: (i, 0, 0)),
        out_shape=jax.ShapeDtypeStruct((t, SUBLANES, LANES), F32),
        scratch_shapes=[pltpu.VMEM((ne * SUBLANES, LANES), BF16)] * PEER_GROUP
                       + [pltpu.VMEM((tb, ne * SUBLANES), F32)],
        compiler_params=pltpu.CompilerParams(dimension_semantics=("arbitrary",),
                                             vmem_limit_bytes=VMEM_LIMIT),
        name="peer_v",
    )(idx_t, wts, x13, gt3, tab3)


def _sc_mesh():
    info = plsc.get_sparse_core_info()
    mesh = plsc.VectorSubcoreMesh(core_axis_name="c", subcore_axis_name="s")
    return mesh, info.num_cores * info.num_subcores, info.num_subcores


def _sc_gather_chunks(tab_hbm, idx_v, bufs, sems, n_chunks, consume):
    rows = bufs[0].shape[0]

    def start(ch):
        return pltpu.async_copy(tab_hbm.at[idx_v.at[pl.ds(ch * rows, rows)]],
                                bufs[ch % 2], sems[ch % 2])

    copy = start(0)
    for ch in range(n_chunks):
        nxt = start(ch + 1) if ch + 1 < n_chunks else None
        copy.wait()
        consume(ch, bufs[ch % 2])
        copy = nxt


def _sc_unpack(words):
    even = lax.bitcast_convert_type(jnp.left_shift(words, 16), F32)
    odd = lax.bitcast_convert_type(jnp.bitwise_and(words, -65536), F32)
    return even, odd


def _peer_u_sc(idx_tm, h2, utab_w, tok_base, tsc):
    ne = idx_tm.shape[1]
    dw = utab_w.shape[1]
    d = 2 * dw
    mesh, nw, nsub = _sc_mesh()
    tps = tsc // nw
    rows = SC_CHUNK_U
    assert tsc % nw == 0 and ne % rows == 0 and dw % SC_LANES == 0

    @functools.partial(
        pl.kernel, mesh=mesh, out_type=jax.ShapeDtypeStruct((tsc, ne * SC_LANES), F32),
        scratch_types=[pltpu.VMEM((ne,), I32), pltpu.VMEM((d,), F32),
                       pltpu.VMEM((dw,), F32), pltpu.VMEM((dw,), F32),
                       pltpu.VMEM((rows, dw), I32), pltpu.VMEM((rows, dw), I32),
                       pltpu.VMEM((ne * SC_LANES,), F32),
                       pltpu.SemaphoreType.DMA, pltpu.SemaphoreType.DMA],
        compiler_params=pltpu.CompilerParams(needs_layout_passes=False), name="peer_u_sc")
    def k(idx_hbm, x_hbm, u_hbm, o_hbm, idx_v, x_v, xe_v, xo_v, rows_a, rows_b, dp_v, sem_a, sem_b):
        wid = lax.axis_index("c") * nsub + lax.axis_index("s")
        zero = jnp.zeros((SC_LANES,), F32)
        lane2 = 2 * lax.iota(I32, SC_LANES)

        @pl.loop(0, tps)
        def _(t):
            loc = wid * tps + t
            pltpu.sync_copy(idx_hbm.at[tok_base + loc], idx_v)
            pltpu.sync_copy(x_hbm.at[tok_base + loc], x_v)
            for j in range(dw // SC_LANES):
                sl = pl.ds(j * SC_LANES, SC_LANES)
                xe_v[sl] = plsc.load_gather(x_v, [lane2 + 2 * SC_LANES * j])
                xo_v[sl] = plsc.load_gather(x_v, [lane2 + (2 * SC_LANES * j + 1)])
            for e in range(ne):
                dp_v[pl.ds(e * SC_LANES, SC_LANES)] = zero

            def consume(ch, rows_v):
                @pl.loop(0, dw, step=SC_LANES)
                def _(c):
                    sl = pl.ds(pl.multiple_of(c, SC_LANES), SC_LANES)
                    xe = xe_v[sl]
                    xo = xo_v[sl]

                    @plsc.parallel_loop(0, rows, 1, unroll=8)
                    def _(r):
                        even, odd = _sc_unpack(rows_v[r, sl])
                        dst = pl.ds(pl.multiple_of((ch * rows + r) * SC_LANES, SC_LANES), SC_LANES)
                        plsc.addupdate(dp_v.at[dst], even * xe + odd * xo)

            _sc_gather_chunks(u_hbm, idx_v, (rows_a, rows_b), (sem_a, sem_b), ne // rows, consume)
            pltpu.sync_copy(dp_v, o_hbm.at[loc])

    return k(idx_tm, h2, utab_w)


def _sc_weights_kernel(dp_ref, gate_ref, wts_tc_ref, o_ref, wts_tc_out_ref, wts_ref):
    del wts_tc_ref, wts_tc_out_ref
    ne = gate_ref.shape[1]
    r = lax.broadcasted_iota(I32, (ne * SC_LANES, ne), 0)
    c = lax.broadcasted_iota(I32, (ne * SC_LANES, ne), 1)
    fold = (jnp.right_shift(r, 4) == c).astype(BF16)
    dots = sum(jnp.dot(p, fold, preferred_element_type=F32) for p in _split3(dp_ref[...]))
    wts = gate_ref[...] * _gelu_exact(dots)
    wts_ref[...] = wts
    r = lax.broadcasted_iota(I32, (ne, ne * SC_LANES), 0)
    c = lax.broadcasted_iota(I32, (ne, ne * SC_LANES), 1)
    spread = (r == jnp.right_shift(c, 4)).astype(BF16)
    o_ref[...] = sum(jnp.dot(p, spread, preferred_element_type=F32) for p in _split3(wts))


def _sc_weights(dparts, gate, wts_tc, tok_base):
    tsc, w = dparts.shape
    ne = gate.shape[1]
    tb = ROW_TILE
    blk0 = tok_base // tb
    return pl.pallas_call(
        _sc_weights_kernel,
        grid=(tsc // tb,),
        in_specs=[pl.BlockSpec((tb, w), lambda i: (i, 0)),
                  pl.BlockSpec((tb, ne), lambda i: (i + blk0, 0)),
                  pl.BlockSpec(memory_space=pl.ANY)],
        out_specs=[pl.BlockSpec((tb, w), lambda i: (i, 0)),
                   pl.BlockSpec(memory_space=pl.ANY),
                   pl.BlockSpec((tb, ne), lambda i: (i, 0))],
        out_shape=[jax.ShapeDtypeStruct((tsc, w), F32),
                   jax.ShapeDtypeStruct(wts_tc.shape, wts_tc.dtype),
                   jax.ShapeDtypeStruct((tsc, ne), F32)],
        input_output_aliases={2: 1},
        name="sc_weights",
    )(dparts, gate, wts_tc)


def _peer_v_sc(idx_tm, w16, x1, gt2, vtab_w, seq, tok_base, tsc):
    w_off = w16.shape[0] - tsc
    ne = idx_tm.shape[1]
    dw = vtab_w.shape[1]
    d = 2 * dw
    mesh, nw, nsub = _sc_mesh()
    tps = tsc // nw
    shift = seq.bit_length() - 1
    rows = SC_CHUNK_V
    assert (1 << shift) == seq and tsc % nw == 0 and ne % rows == 0 and dw % SC_LANES == 0

    @functools.partial(
        pl.kernel, mesh=mesh, out_type=jax.ShapeDtypeStruct((tsc, d), F32),
        scratch_types=[pltpu.VMEM((ne,), I32), pltpu.VMEM((ne * SC_LANES,), F32),
                       pltpu.VMEM((rows, dw), I32), pltpu.VMEM((rows, dw), I32),
                       pltpu.VMEM((dw,), F32), pltpu.VMEM((dw,), F32),
                       pltpu.VMEM((d,), F32), pltpu.VMEM((d,), F32), pltpu.VMEM((d,), F32),
                       pltpu.SemaphoreType.DMA, pltpu.SemaphoreType.DMA],
        compiler_params=pltpu.CompilerParams(needs_layout_passes=False), name="peer_v_sc")
    def k(idx_hbm, w_hbm, x_hbm, gt_hbm, v_hbm, o_hbm, idx_v, w_v, rows_a, rows_b, acce_v, acco_v,
          x_v, gt_v, out_v, sem_a, sem_b):
        wid = lax.axis_index("c") * nsub + lax.axis_index("s")
        zero = jnp.zeros((SC_LANES,), F32)
        lane2 = 2 * lax.iota(I32, SC_LANES)

        @pl.loop(0, tps)
        def _(t):
            loc = wid * tps + t
            tok = tok_base + loc
            pltpu.sync_copy(idx_hbm.at[tok], idx_v)
            pltpu.sync_copy(w_hbm.at[w_off + loc], w_v)
            pltpu.sync_copy(x_hbm.at[tok], x_v)
            pltpu.sync_copy(gt_hbm.at[jnp.right_shift(tok, shift)], gt_v)
            for j in range(dw // SC_LANES):
                acce_v[pl.ds(j * SC_LANES, SC_LANES)] = zero
                acco_v[pl.ds(j * SC_LANES, SC_LANES)] = zero

            def consume(ch, rows_v):
                @pl.loop(0, rows)
                def _(e):
                    wv = w_v[pl.ds((ch * rows + e) * SC_LANES, SC_LANES)]

                    @plsc.parallel_loop(0, dw, SC_LANES, unroll=8)
                    def _(c):
                        sl = pl.ds(pl.multiple_of(c, SC_LANES), SC_LANES)
                        even, odd = _sc_unpack(rows_v[e, sl])
                        plsc.addupdate(acce_v.at[sl], wv * even)
                        plsc.addupdate(acco_v.at[sl], wv * odd)

            _sc_gather_chunks(v_hbm, idx_v, (rows_a, rows_b), (sem_a, sem_b), ne // rows, consume)
            for j in range(dw // SC_LANES):
                sl = pl.ds(j * SC_LANES, SC_LANES)
                ie = lane2 + 2 * SC_LANES * j
                io = ie + 1
                plsc.store_scatter(out_v, [ie], plsc.load_gather(x_v, [ie])
                                   + plsc.load_gather(gt_v, [ie]) * acce_v[sl])
                plsc.store_scatter(out_v, [io], plsc.load_gather(x_v, [io])
                                   + plsc.load_gather(gt_v, [io]) * acco_v[sl])
            pltpu.sync_copy(out_v, o_hbm.at[loc])

    return k(idx_tm, w16, x1, gt2, vtab_w)


def _perm_in_columns(width, kv_width):
    n_shift = 3 * width + DECAY_LORA + ICLR_LORA + GATE_LORA
    aq = n_shift
    akv = aq + width
    ga = akv + 2 * kv_width
    gb = ga + width
    return n_shift, np.concatenate([np.arange(0, n_shift), np.arange(aq, aq + width),
                                    np.arange(ga, ga + width), np.arange(gb, gb + width),
                                    np.arange(akv, akv + 2 * kv_width)])


def _layer(x, ada, norm1_w, norm2_w, w_in, shift_mu, w0, w_lora_up, a0, a_lora_up, g_lora_up,
           k_k, k_a, r_k, lnx_w, lnx_b, q_norm_w, k_norm_w, sinks, w_out, peer_w_q,
           peer_keys_1, peer_keys_2, peer_u, peer_v):
    bsz, seq, d = x.shape
    t = bsz * seq
    x2 = x.reshape(t, d)
    ada3 = ada.reshape(bsz, N_ADA, d)
    n_shift, perm = _perm_in_columns(d, ATTN_KV_HEADS * ATTN_HEAD)
    w_bf = w_in[:, perm].astype(BF16)
    sh, rest = _inproj(x2, ada3, norm1_w, w_bf, shift_mu, seq, n_shift)
    ya = _rwkv(sh, bsz, seq, d, (w0, w_lora_up, a0, a_lora_up, g_lora_up, k_k, k_a,
                                 r_k.reshape(-1), lnx_w, lnx_b))
    yb = _attn(rest, bsz, seq, d, q_norm_w, k_norm_w, sinks)
    x1, h2, sct = _outproj(ya, yb, rest, x2, ada3, w_out.astype(BF16), norm2_w,
                           peer_w_q.astype(BF16), peer_keys_1, peer_keys_2, seq)
    idx_t, idx_tm, gate = _topk(sct)
    sub = d // LANES
    gt2 = ada3[:, 5, :]
    tu, tv = t - SC_TOKENS_U, t - SC_TOKENS_V
    u3, uw = _table_bf16(peer_u)
    v3, vw = _table_bf16(peer_v)
    wts_tc = _peer_u(idx_t, h2.reshape(t, sub, LANES), gate, u3, tu)
    dparts = _peer_u_sc(idx_tm, h2, uw, tu, SC_TOKENS_U)
    w16, wts_tc, wts_sc = _sc_weights(dparts, gate, wts_tc, tu)
    wts = jnp.concatenate([wts_tc, wts_sc[:tv - tu]], axis=0)
    out_tc = _peer_v(idx_t, wts, x1.reshape(t, sub, LANES), gt2.reshape(bsz, sub, LANES), v3, seq)
    out_sc = _peer_v_sc(idx_tm, w16, x1, gt2, vw, seq, tv, SC_TOKENS_V)
    return jnp.concatenate([out_tc.reshape(tv, d), out_sc], axis=0).reshape(bsz, seq, d)


def kernel(x, c, ada_w, ada_b, norm1_w, norm2_w, w_in, shift_mu, w0, w_lora_up, a0, a_lora_up, g_lora_up, k_k, k_a, r_k, lnx_w, lnx_b, q_norm_w, k_norm_w, sinks, w_out, peer_w_q, peer_keys_1, peer_keys_2, peer_u, peer_v):
    depth = ada_w.shape[0]
    for l in range(depth):
        ada = _ada(c, ada_w[l], ada_b[l])
        x = _layer(x, ada, norm1_w[l], norm2_w[l], w_in[l], shift_mu[l], w0[l], w_lora_up[l],
                   a0[l], a_lora_up[l], g_lora_up[l], k_k[l], k_a[l], r_k[l], lnx_w[l], lnx_b[l],
                   q_norm_w[l], k_norm_w[l], sinks[l], w_out[l], peer_w_q[l], peer_keys_1[l],
                   peer_keys_2[l], peer_u[l], peer_v[l])
    return x
```

```python
import functools

import numpy as np
import jax
import jax.numpy as jnp
from jax import lax
from jax.experimental import pallas as pl
from jax.experimental.pallas import tpu as pltpu
from jax.experimental.pallas import tpu_sc as plsc

F32 = jnp.float32
BF16 = jnp.bfloat16
I32 = jnp.int32
HIGHEST = lax.Precision.HIGHEST

RWKV_HEAD = 64
DECAY_LORA = 64
ICLR_LORA = 64
GATE_LORA = 128
LNX_EPS = 64e-5
ATTN_HEAD = 64
ATTN_Q_HEADS = 16
ATTN_KV_HEADS = 2
ATTN_GROUP = ATTN_Q_HEADS // ATTN_KV_HEADS
WINDOW = 128
BLOCK = 128
ATTN_SCALE = ATTN_HEAD ** -0.5
NEG_INF = -1e30
N_KEYS = 128
PEER_HEADS = 8
PEER_HALF = 128
PEER_TOPK = 16
NORM_EPS = 1e-6
N_ADA = 6

LANES = 128
SUBLANES = 8
VMEM_LIMIT = 56 * 1024 * 1024

RWKV_CHUNK = 64
ROW_TILE = 256
TOPK_TILE = 128
PEER_TILE = 128
PEER_GROUP = 16
SC_TOKENS_U = 24832
SC_TOKENS_V = 20736
SC_CHUNK_U = 64
SC_CHUNK_V = 32
SC_LANES = 16
PACK_TILE = 512


def _nt(a, b):
    return lax.dot_general(a.astype(BF16), b.astype(BF16), (((1,), (1,)), ((), ())),
                           preferred_element_type=F32)


def _nn(a, b):
    return jnp.dot(a.astype(BF16), b.astype(BF16), preferred_element_type=F32)


def _dot_f32(a, b):
    return jnp.dot(a, b, precision=HIGHEST, preferred_element_type=F32)


def _bnt(a, b):
    return lax.dot_general(a.astype(BF16), b.astype(BF16), (((2,), (2,)), ((0,), (0,))),
                           preferred_element_type=F32)


def _bnn(a, b):
    return lax.dot_general(a.astype(BF16), b.astype(BF16), (((2,), (1,)), ((0,), (0,))),
                           preferred_element_type=F32)


def _btn(a, b):
    return lax.dot_general(a.astype(BF16), b.astype(BF16), (((1,), (1,)), ((0,), (0,))),
                           preferred_element_type=F32)


def _split2(x):
    hi = x.astype(BF16)
    lo = (x - hi.astype(F32)).astype(BF16)
    return hi, lo


def _split3(x):
    hi = x.astype(BF16)
    r1 = x - hi.astype(F32)
    mid = r1.astype(BF16)
    lo = (r1 - mid.astype(F32)).astype(BF16)
    return hi, mid, lo


def _dot_x3(a, b):
    ah, al = _split2(a)
    bh, bl = _split2(b)
    d = lambda p, q: jnp.dot(p, q, preferred_element_type=F32)
    return d(ah, bh) + (d(ah, bl) + d(al, bh))


def _dot_exact_lhs(a_bf, b):
    return sum(jnp.dot(a_bf, p, preferred_element_type=F32) for p in _split3(b))


def _ada_kernel(c_ref, w_ref, b_ref, o_ref):
    c = c_ref[...]
    cond = c * jax.nn.sigmoid(c)
    o_ref[...] = _dot_f32(cond, w_ref[...]) + b_ref[...]


def _ada(c, ada_w, ada_b):
    bsz, d = c.shape
    n = ada_w.shape[1]
    return pl.pallas_call(
        _ada_kernel,
        grid=(n // d,),
        in_specs=[pl.BlockSpec((bsz, d), lambda j: (0, 0)),
                  pl.BlockSpec((d, d), lambda j: (0, j)),
                  pl.BlockSpec((1, d), lambda j: (0, j))],
        out_specs=pl.BlockSpec((bsz, d), lambda j: (0, j)),
        out_shape=jax.ShapeDtypeStruct((bsz, n), F32),
        name="ada",
    )(c, ada_w, ada_b.reshape(1, n))


def _col_chunks(width, step):
    return [(c0, min(step, width - c0)) for c0 in range(0, width, step)]


def _inproj_kernel(x_ref, ada_ref, nw_ref, w_ref, mu_ref, sh_ref, rest_ref, carry_ref,
                   *, n_shift, n_rest, tiles_per_seq):
    i = pl.program_id(0)
    x = x_ref[...]
    tm = x.shape[0]
    ms = jnp.mean(x * x, axis=-1, keepdims=True)
    ada = ada_ref[0]
    h = x * lax.rsqrt(ms + NORM_EPS) * nw_ref[...] * (1.0 + ada[1:2]) + ada[0:1]
    hb = h.astype(BF16)
    row0 = lax.broadcasted_iota(I32, (tm, 1), 0) == 0
    seq_start = (i % tiles_per_seq) == 0
    for c0, cw in _col_chunks(n_shift, 1024):
        p = jnp.dot(hb, w_ref[:, c0:c0 + cw], preferred_element_type=F32)
        prev_last = jnp.where(seq_start, 0.0, carry_ref[:, c0:c0 + cw])
        carry_ref[:, c0:c0 + cw] = p[tm - 1:tm, :]
        prev = jnp.where(row0, prev_last, pltpu.roll(p, 1, axis=0))
        sh_ref[:, c0:c0 + cw] = p + (prev - p) * mu_ref[:, c0:c0 + cw]
    for c0, cw in _col_chunks(n_rest, 1024):
        rest_ref[:, c0:c0 + cw] = jnp.dot(hb, w_ref[:, n_shift + c0:n_shift + c0 + cw],
                                          preferred_element_type=F32)


def _inproj(x2, ada3, norm_w, w_bf, mu, seq, n_shift):
    t, d = x2.shape
    n_rest = w_bf.shape[1] - n_shift
    tm = ROW_TILE
    tiles_per_seq = seq // tm
    kern = functools.partial(_inproj_kernel, n_shift=n_shift, n_rest=n_rest,
                             tiles_per_seq=tiles_per_seq)
    return pl.pallas_call(
        kern,
        grid=(t // tm,),
        in_specs=[pl.BlockSpec((tm, d), lambda i: (i, 0)),
                  pl.BlockSpec((1, N_ADA, d), lambda i: (i // tiles_per_seq, 0, 0)),
                  pl.BlockSpec((1, d), lambda i: (0, 0)),
                  pl.BlockSpec(w_bf.shape, lambda i: (0, 0), pipeline_mode=pl.Buffered(1)),
                  pl.BlockSpec((1, n_shift), lambda i: (0, 0))],
        out_specs=[pl.BlockSpec((tm, n_shift), lambda i: (i, 0)),
                   pl.BlockSpec((tm, n_rest), lambda i: (i, 0))],
        out_shape=[jax.ShapeDtypeStruct((t, n_shift), F32),
                   jax.ShapeDtypeStruct((t, n_rest), F32)],
        scratch_shapes=[pltpu.VMEM((1, n_shift), F32)],
        compiler_params=pltpu.CompilerParams(dimension_semantics=("arbitrary",),
                                             vmem_limit_bytes=VMEM_LIMIT),
        name="inproj",
    )(x2, ada3, norm_w.reshape(1, d), w_bf, mu.reshape(1, n_shift))


def _softplus(z):
    return jnp.maximum(z, 0.0) + jnp.log1p(jnp.exp(-jnp.abs(z)))


def _rwkv_heads(r, k, v, kkraw, a, cum, lw, s0, masks):
    strict, incl, diag, levels = masks
    c = r.shape[1]
    nrm = jnp.sqrt(jnp.sum(kkraw * kkraw, axis=-1, keepdims=True))
    kk = kkraw / jnp.maximum(nrm, 1e-12)
    w_in = jnp.exp(cum)
    w_ex = jnp.exp(cum - lw)
    w_inv = jnp.exp(-cum)
    w_end = jnp.exp(cum[:, c - 1:c, :] - cum)
    bb = kk * a
    rt = r * w_in
    at = -kk * w_ex
    bt = bb * w_inv
    kt = k * w_inv
    ab = _bnt(at, bt)
    ak = _bnt(at, kt)
    rb = _bnt(rt, bt)
    rk = _bnt(rt, kt)
    lab = jnp.where(strict, ab, 0.0)
    inv = jnp.where(levels[0], lab, 0.0) + jnp.where(diag, 1.0, 0.0)
    for lvl in levels[1:]:
        inv = inv + _bnn(_bnn(inv, jnp.where(lvl, lab, 0.0)), inv)
    rhs = _bnt(at, s0) + _bnn(jnp.where(strict, ak, 0.0), v)
    u = _bnn(inv, rhs)
    y = _bnt(rt, s0) + _bnn(jnp.where(incl, rb, 0.0), u) + _bnn(jnp.where(incl, rk, 0.0), v)
    s_new = s0 * w_in[:, c - 1:c, :] + _btn(u, bb * w_end) + _btn(v, k * w_end)
    return y, s_new


def _rwkv_kernel(pr_ref, pk_ref, pv_ref, tail_ref, w0_ref, wup_ref, a0_ref, aup_ref, gup_ref,
                 kk_ref, ka_ref, rk_ref, lnw_ref, lnb_ref, o_ref, state_ref):
    j = pl.program_id(1)
    c = pr_ref.shape[0]
    n_heads = pr_ref.shape[1] // RWKV_HEAD

    @pl.when(j == 0)
    def _():
        state_ref[...] = jnp.zeros_like(state_ref)

    tail = tail_ref[...]
    pw = tail[:, :DECAY_LORA]
    pa = tail[:, DECAY_LORA:DECAY_LORA + ICLR_LORA]
    pg = tail[:, DECAY_LORA + ICLR_LORA:]
    pk = pk_ref[...]
    w_log = -_softplus(-(w0_ref[...] + _dot_x3(jnp.tanh(pw), wup_ref[...]))) - 0.5
    lw = -jnp.exp(w_log)
    a = jax.nn.sigmoid(a0_ref[...] + _dot_x3(pa, aup_ref[...]))
    row = lax.broadcasted_iota(I32, (c, c), 0)
    col = lax.broadcasted_iota(I32, (c, c), 1)
    strict = row > col
    incl = row >= col
    diag = row == col
    cum = _dot_exact_lhs(incl.astype(BF16), lw)
    k = pk * (1.0 + (a - 1.0) * ka_ref[...])
    kkraw = pk * kk_ref[...]
    g = _dot_x3(jax.nn.sigmoid(pg), gup_ref[...])

    levels = []
    shift = 0
    while (1 << shift) < c:
        levels.append((jnp.right_shift(row, shift + 1) == jnp.right_shift(col, shift + 1))
                      & ((jnp.right_shift(row, shift) & 1) == 1)
                      & ((jnp.right_shift(col, shift) & 1) == 0))
        shift += 1
    masks = (strict, incl, diag, levels)

    def heads(x):
        return jnp.stack([x[:, h * RWKV_HEAD:(h + 1) * RWKV_HEAD] for h in range(n_heads)], axis=0)

    r3, k3, v3 = heads(pr_ref[...]), heads(k), heads(pv_ref[...])
    y, s_new = _rwkv_heads(r3, k3, v3, heads(kkraw), heads(a), heads(cum), heads(lw),
                           state_ref[...], masks)
    state_ref[...] = s_new
    mu = jnp.mean(y, axis=-1, keepdims=True)
    var = jnp.mean(jnp.square(y - mu), axis=-1, keepdims=True)
    yn = (y - mu) * lax.rsqrt(var + LNX_EPS) * lnw_ref[...] + lnb_ref[...]
    bonus = jnp.sum(r3 * k3 * rk_ref[...], axis=-1, keepdims=True) * v3
    out = (yn + bonus) * heads(g)
    for h in range(n_heads):
        o_ref[:, h * RWKV_HEAD:(h + 1) * RWKV_HEAD] = out[h]


def _rwkv(sh, bsz, seq, width, params):
    (w0, w_up, a0, a_up, g_up, k_k, k_a, r_k, lnx_w, lnx_b) = params
    c = RWKV_CHUNK
    nc = seq // c
    heads = width // RWKV_HEAD
    tail_w = DECAY_LORA + ICLR_LORA + GATE_LORA
    tail_blk = (3 * width) // tail_w
    row = lambda b, j: b * nc + j
    vec = lambda a: a.reshape(1, width)
    per_head = lambda a: a.reshape(heads, 1, RWKV_HEAD)
    const = lambda shape: pl.BlockSpec(shape, lambda b, j: (0,) * len(shape))
    return pl.pallas_call(
        _rwkv_kernel,
        grid=(bsz, nc),
        in_specs=[pl.BlockSpec((c, width), lambda b, j: (row(b, j), 0)),
                  pl.BlockSpec((c, width), lambda b, j: (row(b, j), 1)),
                  pl.BlockSpec((c, width), lambda b, j: (row(b, j), 2)),
                  pl.BlockSpec((c, tail_w), lambda b, j: (row(b, j), tail_blk)),
                  const((1, width)), const((DECAY_LORA, width)),
                  const((1, width)), const((ICLR_LORA, width)), const((GATE_LORA, width)),
                  const((1, width)), const((1, width)),
                  const((heads, 1, RWKV_HEAD)), const((heads, 1, RWKV_HEAD)),
                  const((heads, 1, RWKV_HEAD))],
        out_specs=pl.BlockSpec((c, width), lambda b, j: (row(b, j), 0)),
        out_shape=jax.ShapeDtypeStruct((bsz * seq, width), F32),
        scratch_shapes=[pltpu.VMEM((heads, RWKV_HEAD, RWKV_HEAD), F32)],
        compiler_params=pltpu.CompilerParams(dimension_semantics=("arbitrary", "arbitrary"),
                                             vmem_limit_bytes=VMEM_LIMIT),
        name="rwkv",
    )(sh, sh, sh, sh, vec(w0), w_up, vec(a0), a_up, g_up, vec(k_k), vec(k_a), per_head(r_k),
      per_head(lnx_w), per_head(lnx_b))


def _head_rms(x, w):
    return x * lax.rsqrt(jnp.mean(x * x, axis=-1, keepdims=True) + NORM_EPS) * w


def _attn_kernel(sink_ref, q_ref, kvc_ref, kvp_ref, qw_ref, kw_ref, o_ref):
    blk = pl.program_id(1)
    q = q_ref[...]
    kvc = kvc_ref[...]
    kvp = kvp_ref[...]
    kv_w = ATTN_KV_HEADS * ATTN_HEAD
    rows = ATTN_GROUP * BLOCK
    qi = lax.broadcasted_iota(I32, (rows, 2 * BLOCK), 0) & (BLOCK - 1)
    kj = lax.broadcasted_iota(I32, (rows, 2 * BLOCK), 1)
    rel = kj - qi
    valid = (rel >= BLOCK - WINDOW + 1) & (rel <= BLOCK) & (blk * BLOCK - BLOCK + kj >= 0)
    for g in range(ATTN_KV_HEADS):
        ks = slice(g * ATTN_HEAD, (g + 1) * ATTN_HEAD)
        vs = slice(kv_w + g * ATTN_HEAD, kv_w + (g + 1) * ATTN_HEAD)
        kband = _head_rms(jnp.concatenate([kvp[:, ks], kvc[:, ks]], axis=0), kw_ref[...])
        vband = jnp.concatenate([kvp[:, vs], kvc[:, vs]], axis=0)
        heads = [g * ATTN_GROUP + n for n in range(ATTN_GROUP)]
        qs = jnp.concatenate([q[:, hq * ATTN_HEAD:(hq + 1) * ATTN_HEAD] for hq in heads], axis=0)
        qs = _head_rms(qs, qw_ref[...])
        sink = jnp.concatenate([jnp.full((BLOCK, 1), sink_ref[hq], F32) for hq in heads], axis=0)
        s = jnp.where(valid, _nt(qs, kband) * ATTN_SCALE, NEG_INF)
        m = jnp.maximum(jnp.max(s, axis=-1, keepdims=True), sink)
        p = jnp.exp(s - m)
        denom = jnp.sum(p, axis=-1, keepdims=True) + jnp.exp(sink - m)
        o = _nn(p, vband) / denom
        for n, hq in enumerate(heads):
            o_ref[:, hq * ATTN_HEAD:(hq + 1) * ATTN_HEAD] = o[n * BLOCK:(n + 1) * BLOCK, :]


def _attn(rest, bsz, seq, width, q_norm_w, k_norm_w, sinks):
    nb = seq // BLOCK
    kv_w = 2 * ATTN_KV_HEADS * ATTN_HEAD
    kv_blk = (3 * width) // kv_w
    return pl.pallas_call(
        _attn_kernel,
        grid=(bsz, nb),
        in_specs=[pl.BlockSpec(memory_space=pltpu.SMEM),
                  pl.BlockSpec((BLOCK, width), lambda b, j: (b * nb + j, 0)),
                  pl.BlockSpec((BLOCK, kv_w), lambda b, j: (b * nb + j, kv_blk)),
                  pl.BlockSpec((BLOCK, kv_w), lambda b, j: (b * nb + jnp.maximum(j - 1, 0), kv_blk)),
                  pl.BlockSpec((1, ATTN_HEAD), lambda b, j: (0, 0)),
                  pl.BlockSpec((1, ATTN_HEAD), lambda b, j: (0, 0))],
        out_specs=pl.BlockSpec((BLOCK, width), lambda b, j: (b * nb + j, 0)),
        out_shape=jax.ShapeDtypeStruct((bsz * seq, width), F32),
        compiler_params=pltpu.CompilerParams(dimension_semantics=("arbitrary", "arbitrary"),
                                             vmem_limit_bytes=VMEM_LIMIT),
        name="attn",
    )(sinks, rest, rest, rest, q_norm_w.reshape(1, ATTN_HEAD), k_norm_w.reshape(1, ATTN_HEAD))


def _outproj_kernel(ya_ref, yb_ref, ga_ref, gb_ref, x_ref, ada_ref, wo_ref, nw_ref, wq_ref,
                    k1_ref, k2_ref, x1_ref, h2_ref, sct_ref):
    ada = ada_ref[0]
    mixed = jax.nn.sigmoid(ga_ref[...]) * ya_ref[...] + jax.nn.sigmoid(gb_ref[...]) * yb_ref[...]
    x1 = x_ref[...] + ada[2:3] * _nn(mixed, wo_ref[...])
    x1_ref[...] = x1
    ms = jnp.mean(x1 * x1, axis=-1, keepdims=True)
    h2 = x1 * lax.rsqrt(ms + NORM_EPS) * nw_ref[...] * (1.0 + ada[4:5]) + ada[3:4]
    h2_ref[...] = h2
    q = _nn(h2, wq_ref[...])
    for ch in range(q.shape[1] // PEER_HALF):
        keys = k1_ref if ch % 2 == 0 else k2_ref
        cs = slice(ch * PEER_HALF, (ch + 1) * PEER_HALF)
        sct_ref[cs, :] = _nt(keys[...], q[:, cs])


def _outproj(ya, yb, rest, x2, ada3, w_out_bf, norm_w, wq_bf, keys_1, keys_2, seq):
    t, d = x2.shape
    tm = ROW_TILE
    tiles_per_seq = seq // tm
    qd = wq_bf.shape[1]
    rowblk = lambda c: pl.BlockSpec((tm, d), lambda i: (i, c))
    const = lambda shape: pl.BlockSpec(shape, lambda i: (0, 0))
    return pl.pallas_call(
        _outproj_kernel,
        grid=(t // tm,),
        in_specs=[rowblk(0), rowblk(0), rowblk(1), rowblk(2), rowblk(0),
                  pl.BlockSpec((1, N_ADA, d), lambda i: (i // tiles_per_seq, 0, 0)),
                  const((d, d)), const((1, d)), const((d, qd)),
                  const(keys_1.shape), const(keys_2.shape)],
        out_specs=[rowblk(0), rowblk(0), pl.BlockSpec((qd, tm), lambda i: (0, i))],
        out_shape=[jax.ShapeDtypeStruct((t, d), F32), jax.ShapeDtypeStruct((t, d), F32),
                   jax.ShapeDtypeStruct((qd, t), F32)],
        compiler_params=pltpu.CompilerParams(dimension_semantics=("arbitrary",),
                                             vmem_limit_bytes=VMEM_LIMIT),
        name="outproj",
    )(ya, yb, rest, rest, x2, ada3, w_out_bf, norm_w.reshape(1, d), wq_bf, keys_1, keys_2)


def _top16_rows(s, payload=None):
    n = s.shape[0]
    iota = lax.broadcasted_iota(I32, s.shape, 0).astype(F32)
    vals, picks = [], []
    for _ in range(PEER_TOPK):
        m = jnp.max(s, axis=0, keepdims=True)
        i = jnp.min(jnp.where(s == m, iota, float(n)), axis=0, keepdims=True)
        hit = iota == i
        s = jnp.where(hit, -jnp.inf, s)
        vals.append(m)
        picks.append(i if payload is None else
                     jnp.max(jnp.where(hit, payload, -1.0), axis=0, keepdims=True))
    return jnp.concatenate(vals, axis=0), jnp.concatenate(picks, axis=0)


def _topk_kernel(sct_ref, idx_ref, idx_tm_ref, gate_ref):
    keep = [PEER_TOPK // (i + 1) for i in range(PEER_TOPK)]
    pad = (-sum(keep)) % SUBLANES
    idx_all, gate_all = [], []
    for h in range(PEER_HEADS):
        base = h * 2 * N_KEYS
        v1, i1 = _top16_rows(sct_ref[base:base + N_KEYS, :])
        v2, i2 = _top16_rows(sct_ref[base + N_KEYS:base + 2 * N_KEYS, :])
        cand = [v1[i:i + 1, :] + v2[0:keep[i], :] for i in range(PEER_TOPK)]
        cidx = [i1[i:i + 1, :] * float(N_KEYS) + i2[0:keep[i], :] for i in range(PEER_TOPK)]
        if pad:
            cand.append(jnp.full((pad, v1.shape[1]), -jnp.inf, F32))
            cidx.append(jnp.zeros((pad, v1.shape[1]), F32))
        sc, idx = _top16_rows(jnp.concatenate(cand, axis=0), jnp.concatenate(cidx, axis=0))
        idx_all.append(idx.astype(I32))
        ex = jnp.exp(sc - sc[0:1, :])
        gate_all.append(ex / jnp.sum(ex, axis=0, keepdims=True))
    idx = jnp.concatenate(idx_all, axis=0)
    idx_ref[...] = idx
    idx_tm_ref[...] = idx.T
    gate_ref[...] = jnp.concatenate(gate_all, axis=0).T


def _topk(sct):
    qd, t = sct.shape
    tk = TOPK_TILE
    ne = PEER_HEADS * PEER_TOPK
    return pl.pallas_call(
        _topk_kernel,
        grid=(t // tk,),
        in_specs=[pl.BlockSpec((qd, tk), lambda i: (0, i))],
        out_specs=[pl.BlockSpec((ne, tk), lambda i: (0, i)),
                   pl.BlockSpec((tk, ne), lambda i: (i, 0)),
                   pl.BlockSpec((tk, ne), lambda i: (i, 0))],
        out_shape=[jax.ShapeDtypeStruct((ne, t), I32), jax.ShapeDtypeStruct((t, ne), I32),
                   jax.ShapeDtypeStruct((t, ne), F32)],
        compiler_params=pltpu.CompilerParams(dimension_semantics=("arbitrary",),
                                             vmem_limit_bytes=VMEM_LIMIT),
        name="topk",
    )(sct)


def _cast_kernel(x_ref, o_ref):
    o_ref[...] = x_ref[...].astype(BF16)


def _table_bf16(tab):
    n, d = tab.shape
    out = pl.pallas_call(
        _cast_kernel,
        grid=(n // PACK_TILE,),
        in_specs=[pl.BlockSpec((PACK_TILE, d), lambda i: (i, 0))],
        out_specs=pl.BlockSpec((PACK_TILE, d), lambda i: (i, 0)),
        out_shape=jax.ShapeDtypeStruct((n, d), BF16),
        name="table_cast",
    )(tab)
    return out.reshape(n, d // LANES, LANES), lax.bitcast_convert_type(out.reshape(n, d // 2, 2), I32)


def _gelu_exact(x):
    return 0.5 * x * (1.0 + lax.erf(x * (2.0 ** -0.5)))


def _group_mask(ne):
    sub = lax.broadcasted_iota(I32, (SUBLANES, ne * SUBLANES), 0)
    col = lax.broadcasted_iota(I32, (SUBLANES, ne * SUBLANES), 1)
    return sub == (col & (SUBLANES - 1))


def _gather_rows(idx_ref, tab_ref, t, rows_ref):
    picks = idx_ref.at[:, pl.ds(t, 1)]
    for e in range(idx_ref.shape[0]):
        rows_ref[e * SUBLANES:(e + 1) * SUBLANES, :] = tab_ref[picks[e, 0]]


def _peer_u_kernel(idx_ref, x_ref, gate_ref, tab_ref, wts_ref, *scratch):
    rows, d_s = scratch[:-1], scratch[-1]
    tb = x_ref.shape[0]
    ne = gate_ref.shape[1]
    mask = _group_mask(ne)

    def reduce(t, rows_ref):
        hi, lo = _split2(x_ref[t])
        lhs = jnp.concatenate([hi, lo], axis=0)
        z = lax.dot_general(lhs, rows_ref[...], (((1,), (1,)), ((), ())),
                            preferred_element_type=F32)
        zz = z[:SUBLANES] + z[SUBLANES:]
        d_s[pl.ds(t, 1), :] = jnp.sum(jnp.where(mask, zz, 0.0), axis=0, keepdims=True)

    def group(i, carry):
        for q, rows_ref in enumerate(rows):
            _gather_rows(idx_ref, tab_ref, len(rows) * i + q, rows_ref)
        for q, rows_ref in enumerate(rows):
            reduce(len(rows) * i + q, rows_ref)
        return carry

    lax.fori_loop(0, tb // len(rows), group, 0)
    r = lax.broadcasted_iota(I32, (ne * SUBLANES, ne), 0)
    c = lax.broadcasted_iota(I32, (ne * SUBLANES, ne), 1)
    fold = (jnp.right_shift(r, 3) == c).astype(BF16)
    dots = sum(jnp.dot(p, fold, preferred_element_type=F32) for p in _split3(d_s[...]))
    wts_ref[...] = gate_ref[...] * _gelu_exact(dots)


def _peer_u(idx_t, h3, gate, tab3, t):
    ne = gate.shape[1]
    tb = PEER_TILE
    return pl.pallas_call(
        _peer_u_kernel,
        grid=(t // tb,),
        in_specs=[pl.BlockSpec((ne, tb), lambda i: (0, i), memory_space=pltpu.SMEM),
                  pl.BlockSpec((tb, SUBLANES, LANES), lambda i: (i, 0, 0)),
                  pl.BlockSpec((tb, ne), lambda i: (i, 0)),
                  pl.BlockSpec(tab3.shape, lambda i: (0, 0, 0), pipeline_mode=pl.Buffered(1))],
        out_specs=pl.BlockSpec((tb, ne), lambda i: (i, 0)),
        out_shape=jax.ShapeDtypeStruct((t, ne), F32),
        scratch_shapes=[pltpu.VMEM((ne * SUBLANES, LANES), BF16)] * PEER_GROUP
                       + [pltpu.VMEM((tb, ne * SUBLANES), F32)],
        compiler_params=pltpu.CompilerParams(dimension_semantics=("arbitrary",),
                                             vmem_limit_bytes=VMEM_LIMIT),
        name="peer_u",
    )(idx_t, h3, gate, tab3)


def _peer_v_kernel(idx_ref, wts_ref, x1_ref, gt_ref, tab_ref, o_ref, *scratch):
    rows, w8_s = scratch[:-1], scratch[-1]
    tb = x1_ref.shape[0]
    ne = wts_ref.shape[1]
    mask = _group_mask(ne)
    gt = gt_ref[0]
    r = lax.broadcasted_iota(I32, (ne, ne * SUBLANES), 0)
    c = lax.broadcasted_iota(I32, (ne, ne * SUBLANES), 1)
    spread = (r == jnp.right_shift(c, 3)).astype(BF16)
    w8_s[...] = sum(jnp.dot(p, spread, preferred_element_type=F32) for p in _split3(wts_ref[...]))

    def combine(t, rows_ref):
        wm = jnp.where(mask, w8_s[pl.ds(t, 1), :], 0.0)
        hi, lo = _split2(wm)
        res = jnp.dot(jnp.concatenate([hi, lo], axis=0), rows_ref[...],
                      preferred_element_type=F32)
        o_ref[t] = x1_ref[t] + gt * (res[:SUBLANES] + res[SUBLANES:])

    def group(i, carry):
        for q, rows_ref in enumerate(rows):
            _gather_rows(idx_ref, tab_ref, len(rows) * i + q, rows_ref)
        for q, rows_ref in enumerate(rows):
            combine(len(rows) * i + q, rows_ref)
        return carry

    lax.fori_loop(0, tb // len(rows), group, 0)


def _peer_v(idx_t, wts, x13, gt3, tab3, seq):
    t, ne = wts.shape
    tb = PEER_TILE
    tiles_per_seq = seq // tb
    return pl.pallas_call(
        _peer_v_kernel,
        grid=(t // tb,),
        in_specs=[pl.BlockSpec((ne, tb), lambda i: (0, i), memory_space=pltpu.SMEM),
                  pl.BlockSpec((tb, ne), lambda i: (i, 0)),
                  pl.BlockSpec((tb, SUBLANES, LANES), lambda i: (i, 0, 0)),
                  pl.BlockSpec((1, SUBLANES, LANES), lambda i: (i // tiles_per_seq, 0, 0)),
                  pl.BlockSpec(tab3.shape, lambda i: (0, 0, 0), pipeline_mode=pl.Buffered(1))],
        out_specs=pl.BlockSpec((tb, SUBLANES, LANES), lambda i: (i, 0, 0)),
        out_shape=jax.ShapeDtypeStruct((t, SUBLANES, LANES), F32),
        scratch_shapes=[pltpu.VMEM((ne * SUBLANES, LANES), BF16)] * PEER_GROUP
                       + [pltpu.VMEM((tb, ne * SUBLANES), F32)],
        compiler_params=pltpu.CompilerParams(dimension_semantics=("arbitrary",),
                                             vmem_limit_bytes=VMEM_LIMIT),
        name="peer_v",
    )(idx_t, wts, x13, gt3, tab3)


def _sc_mesh():
    info = plsc.get_sparse_core_info()
    mesh = plsc.VectorSubcoreMesh(core_axis_name="c", subcore_axis_name="s")
    return mesh, info.num_cores * info.num_subcores, info.num_subcores


def _sc_gather_chunks(tab_hbm, idx_v, bufs, sems, n_chunks, consume):
    rows = bufs[0].shape[0]

    def start(ch):
        return pltpu.async_copy(tab_hbm.at[idx_v.at[pl.ds(ch * rows, rows)]],
                                bufs[ch % 2], sems[ch % 2])

    copy = start(0)
    for ch in range(n_chunks):
        nxt = start(ch + 1) if ch + 1 < n_chunks else None
        copy.wait()
        consume(ch, bufs[ch % 2])
        copy = nxt


def _sc_unpack(words):
    even = lax.bitcast_convert_type(jnp.left_shift(words, 16), F32)
    odd = lax.bitcast_convert_type(jnp.bitwise_and(words, -65536), F32)
    return even, odd


def _peer_u_sc(idx_tm, h2, utab_w, tok_base, tsc):
    ne = idx_tm.shape[1]
    dw = utab_w.shape[1]
    d = 2 * dw
    mesh, nw, nsub = _sc_mesh()
    tps = tsc // nw
    rows = SC_CHUNK_U
    assert tsc % nw == 0 and ne % rows == 0 and dw % SC_LANES == 0

    @functools.partial(
        pl.kernel, mesh=mesh, out_type=jax.ShapeDtypeStruct((tsc, ne * SC_LANES), F32),
        scratch_types=[pltpu.VMEM((ne,), I32), pltpu.VMEM((d,), F32),
                       pltpu.VMEM((dw,), F32), pltpu.VMEM((dw,), F32),
                       pltpu.VMEM((rows, dw), I32), pltpu.VMEM((rows, dw), I32),
                       pltpu.VMEM((ne * SC_LANES,), F32),
                       pltpu.SemaphoreType.DMA, pltpu.SemaphoreType.DMA],
        compiler_params=pltpu.CompilerParams(needs_layout_passes=False), name="peer_u_sc")
    def k(idx_hbm, x_hbm, u_hbm, o_hbm, idx_v, x_v, xe_v, xo_v, rows_a, rows_b, dp_v, sem_a, sem_b):
        wid = lax.axis_index("c") * nsub + lax.axis_index("s")
        zero = jnp.zeros((SC_LANES,), F32)
        lane2 = 2 * lax.iota(I32, SC_LANES)

        @pl.loop(0, tps)
        def _(t):
            loc = wid * tps + t
            pltpu.sync_copy(idx_hbm.at[tok_base + loc], idx_v)
            pltpu.sync_copy(x_hbm.at[tok_base + loc], x_v)
            for j in range(dw // SC_LANES):
                sl = pl.ds(j * SC_LANES, SC_LANES)
                xe_v[sl] = plsc.load_gather(x_v, [lane2 + 2 * SC_LANES * j])
                xo_v[sl] = plsc.load_gather(x_v, [lane2 + (2 * SC_LANES * j + 1)])
            for e in range(ne):
                dp_v[pl.ds(e * SC_LANES, SC_LANES)] = zero

            def consume(ch, rows_v):
                @pl.loop(0, dw, step=SC_LANES)
                def _(c):
                    sl = pl.ds(pl.multiple_of(c, SC_LANES), SC_LANES)
                    xe = xe_v[sl]
                    xo = xo_v[sl]

                    @plsc.parallel_loop(0, rows, 1, unroll=8)
                    def _(r):
                        even, odd = _sc_unpack(rows_v[r, sl])
                        dst = pl.ds(pl.multiple_of((ch * rows + r) * SC_LANES, SC_LANES), SC_LANES)
                        plsc.addupdate(dp_v.at[dst], even * xe + odd * xo)

            _sc_gather_chunks(u_hbm, idx_v, (rows_a, rows_b), (sem_a, sem_b), ne // rows, consume)
            pltpu.sync_copy(dp_v, o_hbm.at[loc])

    return k(idx_tm, h2, utab_w)


def _sc_weights_kernel(dp_ref, gate_ref, wts_tc_ref, o_ref, wts_tc_out_ref, wts_ref):
    del wts_tc_ref, wts_tc_out_ref
    ne = gate_ref.shape[1]
    r = lax.broadcasted_iota(I32, (ne * SC_LANES, ne), 0)
    c = lax.broadcasted_iota(I32, (ne * SC_LANES, ne), 1)
    fold = (jnp.right_shift(r, 4) == c).astype(BF16)
    dots = sum(jnp.dot(p, fold, preferred_element_type=F32) for p in _split3(dp_ref[...]))
    wts = gate_ref[...] * _gelu_exact(dots)
    wts_ref[...] = wts
    r = lax.broadcasted_iota(I32, (ne, ne * SC_LANES), 0)
    c = lax.broadcasted_iota(I32, (ne, ne * SC_LANES), 1)
    spread = (r == jnp.right_shift(c, 4)).astype(BF16)
    o_ref[...] = sum(jnp.dot(p, spread, preferred_element_type=F32) for p in _split3(wts))


def _sc_weights(dparts, gate, wts_tc, tok_base):
    tsc, w = dparts.shape
    ne = gate.shape[1]
    tb = ROW_TILE
    blk0 = tok_base // tb
    return pl.pallas_call(
        _sc_weights_kernel,
        grid=(tsc // tb,),
        in_specs=[pl.BlockSpec((tb, w), lambda i: (i, 0)),
                  pl.BlockSpec((tb, ne), lambda i: (i + blk0, 0)),
                  pl.BlockSpec(memory_space=pl.ANY)],
        out_specs=[pl.BlockSpec((tb, w), lambda i: (i, 0)),
                   pl.BlockSpec(memory_space=pl.ANY),
                   pl.BlockSpec((tb, ne), lambda i: (i, 0))],
        out_shape=[jax.ShapeDtypeStruct((tsc, w), F32),
                   jax.ShapeDtypeStruct(wts_tc.shape, wts_tc.dtype),
                   jax.ShapeDtypeStruct((tsc, ne), F32)],
        input_output_aliases={2: 1},
        name="sc_weights",
    )(dparts, gate, wts_tc)


def _peer_v_sc(idx_tm, w16, x1, gt2, vtab_w, seq, tok_base, tsc):
    w_off = w16.shape[0] - tsc
    ne = idx_tm.shape[1]
    dw = vtab_w.shape[1]
    d = 2 * dw
    mesh, nw, nsub = _sc_mesh()
    tps = tsc // nw
    shift = seq.bit_length() - 1
    rows = SC_CHUNK_V
    assert (1 << shift) == seq and tsc % nw == 0 and ne % rows == 0 and dw % SC_LANES == 0

    @functools.partial(
        pl.kernel, mesh=mesh, out_type=jax.ShapeDtypeStruct((tsc, d), F32),
        scratch_types=[pltpu.VMEM((ne,), I32), pltpu.VMEM((ne * SC_LANES,), F32),
                       pltpu.VMEM((rows, dw), I32), pltpu.VMEM((rows, dw), I32),
                       pltpu.VMEM((dw,), F32), pltpu.VMEM((dw,), F32),
                       pltpu.VMEM((d,), F32), pltpu.VMEM((d,), F32), pltpu.VMEM((d,), F32),
                       pltpu.SemaphoreType.DMA, pltpu.SemaphoreType.DMA],
        compiler_params=pltpu.CompilerParams(needs_layout_passes=False), name="peer_v_sc")
    def k(idx_hbm, w_hbm, x_hbm, gt_hbm, v_hbm, o_hbm, idx_v, w_v, rows_a, rows_b, acce_v, acco_v,
          x_v, gt_v, out_v, sem_a, sem_b):
        wid = lax.axis_index("c") * nsub + lax.axis_index("s")
        zero = jnp.zeros((SC_LANES,), F32)
        lane2 = 2 * lax.iota(I32, SC_LANES)

        @pl.loop(0, tps)
        def _(t):
            loc = wid * tps + t
            tok = tok_base + loc
            pltpu.sync_copy(idx_hbm.at[tok], idx_v)
            pltpu.sync_copy(w_hbm.at[w_off + loc], w_v)
            pltpu.sync_copy(x_hbm.at[tok], x_v)
            pltpu.sync_copy(gt_hbm.at[jnp.right_shift(tok, shift)], gt_v)
            for j in range(dw // SC_LANES):
                acce_v[pl.ds(j * SC_LANES, SC_LANES)] = zero
                acco_v[pl.ds(j * SC_LANES, SC_LANES)] = zero

            def consume(ch, rows_v):
                @pl.loop(0, rows)
                def _(e):
                    wv = w_v[pl.ds((ch * rows + e) * SC_LANES, SC_LANES)]

                    @plsc.parallel_loop(0, dw, SC_LANES, unroll=8)
                    def _(c):
                        sl = pl.ds(pl.multiple_of(c, SC_LANES), SC_LANES)
                        even, odd = _sc_unpack(rows_v[e, sl])
                        plsc.addupdate(acce_v.at[sl], wv * even)
                        plsc.addupdate(acco_v.at[sl], wv * odd)

            _sc_gather_chunks(v_hbm, idx_v, (rows_a, rows_b), (sem_a, sem_b), ne // rows, consume)
            for j in range(dw // SC_LANES):
                sl = pl.ds(j * SC_LANES, SC_LANES)
                ie = lane2 + 2 * SC_LANES * j
                io = ie + 1
                plsc.store_scatter(out_v, [ie], plsc.load_gather(x_v, [ie])
                                   + plsc.load_gather(gt_v, [ie]) * acce_v[sl])
                plsc.store_scatter(out_v, [io], plsc.load_gather(x_v, [io])
                                   + plsc.load_gather(gt_v, [io]) * acco_v[sl])
            pltpu.sync_copy(out_v, o_hbm.at[loc])

    return k(idx_tm, w16, x1, gt2, vtab_w)


def _perm_in_columns(width, kv_width):
    n_shift = 3 * width + DECAY_LORA + ICLR_LORA + GATE_LORA
    aq = n_shift
    akv = aq + width
    ga = akv + 2 * kv_width
    gb = ga + width
    return n_shift, np.concatenate([np.arange(0, n_shift), np.arange(aq, aq + width),
                                    np.arange(ga, ga + width), np.arange(gb, gb + width),
                                    np.arange(akv, akv + 2 * kv_width)])


def _layer(x, ada, norm1_w, norm2_w, w_in, shift_mu, w0, w_lora_up, a0, a_lora_up, g_lora_up,
           k_k, k_a, r_k, lnx_w, lnx_b, q_norm_w, k_norm_w, sinks, w_out, peer_w_q,
           peer_keys_1, peer_keys_2, peer_u, peer_v):
    bsz, seq, d = x.shape
    t = bsz * seq
    x2 = x.reshape(t, d)
    ada3 = ada.reshape(bsz, N_ADA, d)
    n_shift, perm = _perm_in_columns(d, ATTN_KV_HEADS * ATTN_HEAD)
    w_bf = w_in[:, perm].astype(BF16)
    sh, rest = _inproj(x2, ada3, norm1_w, w_bf, shift_mu, seq, n_shift)
    ya = _rwkv(sh, bsz, seq, d, (w0, w_lora_up, a0, a_lora_up, g_lora_up, k_k, k_a,
                                 r_k.reshape(-1), lnx_w, lnx_b))
    yb = _attn(rest, bsz, seq, d, q_norm_w, k_norm_w, sinks)
    x1, h2, sct = _outproj(ya, yb, rest, x2, ada3, w_out.astype(BF16), norm2_w,
                           peer_w_q.astype(BF16), peer_keys_1, peer_keys_2, seq)
    idx_t, idx_tm, gate = _topk(sct)
    sub = d // LANES
    gt2 = ada3[:, 5, :]
    tu, tv = t - SC_TOKENS_U, t - SC_TOKENS_V
    u3, uw = _table_bf16(peer_u)
    v3, vw = _table_bf16(peer_v)
    wts_tc = _peer_u(idx_t, h2.reshape(t, sub, LANES), gate, u3, tu)
    dparts = _peer_u_sc(idx_tm, h2, uw, tu, SC_TOKENS_U)
    w16, wts_tc, wts_sc = _sc_weights(dparts, gate, wts_tc, tu)
    wts = jnp.concatenate([wts_tc, wts_sc[:tv - tu]], axis=0)
    out_tc = _peer_v(idx_t, wts, x1.reshape(t, sub, LANES), gt2.reshape(bsz, sub, LANES), v3, seq)
    out_sc = _peer_v_sc(idx_tm, w16, x1, gt2, vw, seq, tv, SC_TOKENS_V)
    return jnp.concatenate([out_tc.reshape(tv, d), out_sc], axis=0).reshape(bsz, seq, d)


def kernel(x, c, ada_w, ada_b, norm1_w, norm2_w, w_in, shift_mu, w0, w_lora_up, a0, a_lora_up, g_lora_up, k_k, k_a, r_k, lnx_w, lnx_b, q_norm_w, k_norm_w, sinks, w_out, peer_w_q, peer_keys_1, peer_keys_2, peer_u, peer_v):
    depth = ada_w.shape[0]
    for l in range(depth):
        ada = _ada(c, ada_w[l], ada_b[l])
        x = _layer(x, ada, norm1_w[l], norm2_w[l], w_in[l], shift_mu[l], w0[l], w_lora_up[l],
                   a0[l], a_lora_up[l], g_lora_up[l], k_k[l], k_a[l], r_k[l], lnx_w[l], lnx_b[l],
                   q_norm_w[l], k_norm_w[l], sinks[l], w_out[l], peer_w_q[l], peer_keys_1[l],
                   peer_keys_2[l], peer_u[l], peer_v[l])
    return x
```
